```python
import math
import jax, jax.numpy as jnp
from jax import lax
import numpy as np

D_MODEL = 1024
BATCH = 16
SEQ = 256
DEPTH = 4
DEC_BATCH = 2
DEC_SEQ = 4096
PAST_LEN = 512

GRID_W = 64
NORM_EPS = 1e-6
N_BRANCH = 3
BRANCH_DIM = 512
RWKV_HEADS = 8
RWKV_N = 64
RWKV_DIM = RWKV_HEADS * RWKV_N
RWKV_LORA_W = 64
RWKV_LORA_A = 64
RWKV_LORA_G = 128
RWKV_GN_EPS = 64e-5
RWKV_COLS = 3 * RWKV_DIM + 2 * RWKV_LORA_W + 2 * RWKV_LORA_A + RWKV_LORA_G
RWKV_SPLITS = (RWKV_DIM, 2 * RWKV_DIM, 3 * RWKV_DIM, 3 * RWKV_DIM + 2 * RWKV_LORA_W,
               3 * RWKV_DIM + 2 * RWKV_LORA_W + 2 * RWKV_LORA_A)
SSD_HEADS = 8
SSD_P = 64
SSD_DIM = SSD_HEADS * SSD_P
SSD_GROUPS = 2
SSD_N = 64
SSD_CONV = 5
SSD_CHUNK = 128
SSD_CONV_DIM = SSD_DIM + 2 * SSD_GROUPS * SSD_N
SSD_COLS = SSD_DIM + SSD_CONV_DIM + 2 * SSD_HEADS
SSD_SPLITS = (SSD_DIM, SSD_DIM + SSD_CONV_DIM)
ATT_HEADS = 8
ATT_KV_HEADS = 2
ATT_GROUP = ATT_HEADS // ATT_KV_HEADS
ATT_HD = 64
ATT_DIM = ATT_HEADS * ATT_HD
ATT_WINDOW = 128
ATT_BLOCK = 128
ATT_SCALE = ATT_HD ** -0.5
ATT_COLS = ATT_DIM + 2 * ATT_KV_HEADS * ATT_HD
ATT_SPLITS = (ATT_DIM, ATT_DIM + ATT_KV_HEADS * ATT_HD)
ROPE_BASE = 10000.0
ROPE_AXIS_DIM = ATT_HD // 2
GATE_COLS = N_BRANCH * D_MODEL
IN_COLS = RWKV_COLS + SSD_COLS + ATT_COLS + GATE_COLS
IN_SPLITS = (RWKV_COLS, RWKV_COLS + SSD_COLS, RWKV_COLS + SSD_COLS + ATT_COLS)
MOE_GROUPS = 4
MOE_PER_GROUP = 8
MOE_EXPERTS = MOE_GROUPS * MOE_PER_GROUP
MOE_TOPK = 2
D_EXPERT = 256

kernel_name = 'hybrid_flow_backbone_step'

F32 = jnp.float32


def rmsnorm(x, w):
    xf = x.astype(F32)
    y = xf * lax.rsqrt(jnp.mean(xf * xf, axis=-1, keepdims=True) + NORM_EPS)
    return (y * w).astype(x.dtype)


def centred_shift(p):
    prev = jnp.pad(p[:, :-1], ((0, 0), (1, 0), (0, 0)))
    nxt = jnp.pad(p[:, 1:], ((0, 0), (0, 1), (0, 0)))
    return 0.5 * (prev + nxt)


def axial_rope(rows):
    row = jnp.repeat(jnp.arange(rows, dtype=F32), GRID_W)
    col = jnp.tile(jnp.arange(GRID_W, dtype=F32), rows)
    inv = 1.0 / (ROPE_BASE ** (jnp.arange(0, ROPE_AXIS_DIM, 2, dtype=F32) / ROPE_AXIS_DIM))
    ang_r = row[:, None] * inv[None, :]
    ang_c = col[:, None] * inv[None, :]
    return (jnp.cos(ang_r), jnp.sin(ang_r), jnp.cos(ang_c), jnp.sin(ang_c))


def rotate_half(x, cos, sin):
    x1, x2 = jnp.split(x, 2, axis=-1)
    return jnp.concatenate([x1 * cos - x2 * sin, x1 * sin + x2 * cos], axis=-1)


def apply_axial_rope(x, rope):
    cr, sr, cc, sc = rope
    shape = (1, x.shape[1]) + (1,) * (x.ndim - 3) + (cr.shape[-1],)
    xf = x.astype(F32)
    out = jnp.concatenate([rotate_half(xf[..., :ROPE_AXIS_DIM], cr.reshape(shape), sr.reshape(shape)),
                           rotate_half(xf[..., ROPE_AXIS_DIM:], cc.reshape(shape), sc.reshape(shape))], axis=-1)
    return out.astype(x.dtype)


def sink_softmax(s, sink):
    sb = sink.astype(F32).reshape(ATT_KV_HEADS, ATT_GROUP, 1, 1)
    m = jnp.maximum(jnp.max(s, axis=-1, keepdims=True), sb)
    e = jnp.exp(s - m)
    return e / (jnp.sum(e, axis=-1, keepdims=True) + jnp.exp(sb - m))


def rwkv_scan(r, w, k, v, kk, b, s0, reverse):
    def step(S, inp):
        r_t, w_t, k_t, v_t, kk_t, b_t = inp
        sa = jnp.einsum('bhvk,bhk->bhv', S, kk_t)
        S = S * w_t[:, :, None, :] - sa[..., None] * b_t[:, :, None, :] + v_t[..., None] * k_t[:, :, None, :]
        return S, jnp.einsum('bhvk,bhk->bhv', S, r_t)
    xs = tuple(jnp.moveaxis(t, 1, 0) for t in (r, w, k, v, kk, b))
    S, ys = lax.scan(step, s0, xs, reverse=reverse)
    return jnp.moveaxis(ys, 0, 1), S


def rwkv_branch(p, lp, s0):
    B_, L, _ = p.shape
    H, N = RWKV_HEADS, RWKV_N
    p = p + lp['rwkv_mu'] * (centred_shift(p) - p)
    r, k, v, wd, ad, gd = jnp.split(p, RWKV_SPLITS, axis=-1)
    wd = wd.reshape(B_, L, 2, RWKV_LORA_W)
    ad = ad.reshape(B_, L, 2, RWKV_LORA_A)
    w_log = -jax.nn.softplus(-(lp['rwkv_w0'] + jnp.einsum('bldr,drc->bldc', jnp.tanh(wd), lp['rwkv_w2'])).astype(F32)) - 0.5
    decay = jnp.exp(-jnp.exp(w_log))
    a = jax.nn.sigmoid(lp['rwkv_a0'] + jnp.einsum('bldr,drc->bldc', ad, lp['rwkv_a2']))
    g = jnp.einsum('blr,rc->blc', jax.nn.sigmoid(gd), lp['rwkv_g2'])
    kk = (k * lp['rwkv_k_k']).reshape(B_, L, H, N).astype(F32)
    kk = kk / jnp.maximum(jnp.sqrt(jnp.sum(kk * kk, axis=-1, keepdims=True)), 1e-12)
    k_dir = k[:, :, None, :] * (1 + (a - 1) * lp['rwkv_k_a'])
    hd = lambda t: t.reshape(t.shape[:2] + (H, N)).astype(F32)
    rf, vf = hd(r), hd(v)
    if s0 is None:
        s0 = jnp.zeros((B_, 2, H, N, N), F32)
    outs, finals = [], []
    for d, rev in ((0, False), (1, True)):
        kd, adir = hd(k_dir[:, :, d]), hd(a[:, :, d])
        y, S = rwkv_scan(rf, hd(decay[:, :, d]), kd, vf, kk, kk * adir, s0[:, d].astype(F32), rev)
        bonus = jnp.sum(rf * kd * lp['rwkv_r_k'], axis=-1, keepdims=True) * vf
        outs.append(y + bonus)
        finals.append(S)
    y = outs[0] + outs[1]
    mu = jnp.mean(y, axis=-1, keepdims=True)
    var = jnp.mean((y - mu) ** 2, axis=-1, keepdims=True)
    y = ((y - mu) * lax.rsqrt(var + RWKV_GN_EPS)).reshape(B_, L, RWKV_DIM)
    y = (y * lp['rwkv_lnx_w'] + lp['rwkv_lnx_b']) * g
    return y.astype(p.dtype), jnp.stack(finals, axis=1).astype(p.dtype)


def ssd_chunked(x, dt, A, Bm, Cm, h0):
    Bsz, L, H, P = x.shape
    Q = SSD_CHUNK
    nc = L // Q
    rep = H // SSD_GROUPS
    xc = x.astype(F32).reshape(Bsz, nc, Q, H, P)
    dtc = dt.astype(F32).reshape(Bsz, nc, Q, H)
    Bc = jnp.repeat(Bm.astype(F32), rep, axis=2).reshape(Bsz, nc, Q, H, SSD_N)
    Cc = jnp.repeat(Cm.astype(F32), rep, axis=2).reshape(Bsz, nc, Q, H, SSD_N)
    a_cum = jnp.cumsum(dtc * A, axis=2)
    seg = a_cum[:, :, :, None, :] - a_cum[:, :, None, :, :]
    causal = jnp.tril(jnp.ones((Q, Q), dtype=bool))
    decay_in = jnp.exp(jnp.where(causal[:, :, None], seg, -jnp.inf))
    scores = jnp.einsum('bcqhn,bckhn->bcqkh', Cc, Bc) * decay_in * dtc[:, :, None, :, :]
    y_diag = jnp.einsum('bcqkh,bckhp->bcqhp', scores, xc)
    to_end = jnp.exp(a_cum[:, :, -1:, :] - a_cum) * dtc
    chunk_states = jnp.einsum('bckhn,bckh,bckhp->bchpn', Bc, to_end, xc)
    chunk_decay = jnp.exp(a_cum[:, :, -1, :])

    def step(h, inp):
        dec, st = inp
        return h * dec[:, :, None, None] + st, h
    h_last, h_in = lax.scan(step, h0.astype(F32), (jnp.moveaxis(chunk_decay, 1, 0), jnp.moveaxis(chunk_states, 1, 0)))
    h_in = jnp.moveaxis(h_in, 0, 1)
    y_off = jnp.einsum('bcqhn,bchpn,bcqh->bcqhp', Cc, h_in, jnp.exp(a_cum))
    return (y_diag + y_off).reshape(Bsz, L, H, P), h_last


def ssd_branch(p, lp, h0):
    B_, L, _ = p.shape
    z, xbc, dt_raw = jnp.split(p, SSD_SPLITS, axis=-1)
    xbc = lax.conv_general_dilated(xbc, lp['ssd_conv_w'][:, None, :].astype(xbc.dtype), window_strides=(1,),
                                   padding=((SSD_CONV // 2, SSD_CONV // 2),),
                                   dimension_numbers=('NWC', 'WIO', 'NWC'),
                                   feature_group_count=SSD_CONV_DIM) + lp['ssd_conv_b']
    xbc = jax.nn.silu(xbc)
    x, Bm, Cm = jnp.split(xbc, (SSD_DIM, SSD_DIM + SSD_GROUPS * SSD_N), axis=-1)
    x = x.reshape(B_, L, SSD_HEADS, SSD_P)
    Bm = Bm.reshape(B_, L, SSD_GROUPS, SSD_N)
    Cm = Cm.reshape(B_, L, SSD_GROUPS, SSD_N)
    dt = jax.nn.softplus((dt_raw.reshape(B_, L, 2, SSD_HEADS) + lp['ssd_dt_bias']).astype(F32))
    A = -jnp.exp(lp['ssd_a_log'].astype(F32))
    if h0 is None:
        h0 = jnp.zeros((B_, 2, SSD_HEADS, SSD_P, SSD_N), F32)
    flip = lambda t: jnp.flip(t, axis=1)
    y_f, h_f = ssd_chunked(x, dt[:, :, 0], A[0], Bm, Cm, h0[:, 0])
    y_b, h_b = ssd_chunked(flip(x), flip(dt[:, :, 1]), A[1], flip(Bm), flip(Cm), h0[:, 1])
    y = y_f + flip(y_b) + lp['ssd_d'][:, None] * x.astype(F32)
    y = y.reshape(B_, L, SSD_DIM) * jax.nn.silu(z.astype(F32))
    y = rmsnorm(y, lp['ssd_norm_w'])
    return y.astype(p.dtype), jnp.stack([h_f, h_b], axis=1).astype(p.dtype)


def context_attention(q, k, v, sink):
    B_, L = q.shape[:2]
    nb = L // ATT_BLOCK
    qb = jnp.moveaxis(q.reshape(B_, nb, ATT_BLOCK, ATT_KV_HEADS, ATT_GROUP, ATT_HD), 1, 0)

    def one_block(qblk):
        s = jnp.einsum('bqhgd,bkhd->bhgqk', qblk, k).astype(F32) * ATT_SCALE
        pr = sink_softmax(s, sink).astype(v.dtype)
        return jnp.einsum('bhgqk,bkhd->bqhgd', pr, v)
    o = lax.map(one_block, qb)
    return jnp.moveaxis(o, 0, 1).reshape(B_, L, ATT_DIM)


def latent_attention(q, k, v, k_ctx, v_ctx, sink):
    B_, L = q.shape[:2]
    Q = ATT_BLOCK
    nb = L // Q
    qb = q.reshape(B_, nb, Q, ATT_KV_HEADS, ATT_GROUP, ATT_HD)

    def band(t):
        tp = jnp.pad(t, ((0, 0), (Q, Q), (0, 0), (0, 0))).reshape(B_, nb + 2, Q, ATT_KV_HEADS, ATT_HD)
        return jnp.concatenate([tp[:, :-2], tp[:, 1:-1], tp[:, 2:]], axis=2)
    kb, vb = band(k), band(v)
    qpos = jnp.arange(nb)[:, None] * Q + jnp.arange(Q)[None, :]
    kpos = (jnp.arange(nb)[:, None] - 1) * Q + jnp.arange(3 * Q)[None, :]
    rel = kpos[:, None, :] - qpos[:, :, None]
    valid = (jnp.abs(rel) <= ATT_WINDOW) & (kpos[:, None, :] >= 0) & (kpos[:, None, :] < L)
    s_loc = jnp.einsum('bnqhgd,bnkhd->bnhgqk', qb, kb).astype(F32) * ATT_SCALE
    s_loc = jnp.where(valid[None, :, None, None], s_loc, -jnp.inf)
    s_ctx = jnp.einsum('bnqhgd,bkhd->bnhgqk', qb, k_ctx).astype(F32) * ATT_SCALE
    pr = sink_softmax(jnp.concatenate([s_loc, s_ctx], axis=-1), sink).astype(v.dtype)
    o = (jnp.einsum('bnhgqk,bnkhd->bnqhgd', pr[..., :3 * Q], vb)
         + jnp.einsum('bnhgqk,bkhd->bnqhgd', pr[..., 3 * Q:], v_ctx.astype(v.dtype)))
    return o.reshape(B_, L, ATT_DIM)


def attention_branch(p, lp, rope, ctx_k, ctx_v):
    B_, L, _ = p.shape
    q, k, v = jnp.split(p, ATT_SPLITS, axis=-1)
    q = q.reshape(B_, L, ATT_KV_HEADS, ATT_GROUP, ATT_HD)
    k = k.reshape(B_, L, ATT_KV_HEADS, ATT_HD)
    v = v.reshape(B_, L, ATT_KV_HEADS, ATT_HD)
    if rope is None:
        y = context_attention(q, k, v, lp['att_sink'])
    else:
        y = latent_attention(apply_axial_rope(q, rope), apply_axial_rope(k, rope), v, ctx_k, ctx_v, lp['att_sink'])
    return y.astype(p.dtype), k, v


def token_mix(xn, lp, rope, ctx_k, ctx_v, s_rwkv, s_ssd):
    B_, L, _ = xn.shape
    proj = jnp.einsum('bld,dc->blc', xn, lp['w_in'])
    p_rwkv, p_ssd, p_att, p_gate = jnp.split(proj, IN_SPLITS, axis=-1)
    y_a, s_r = rwkv_branch(p_rwkv, lp, s_rwkv)
    y_b, s_s = ssd_branch(p_ssd, lp, s_ssd)
    y_c, k, v = attention_branch(p_att, lp, rope, ctx_k, ctx_v)
    branches = jnp.stack([y_a, y_b, y_c], axis=2)
    widened = jnp.einsum('blic,icd->blid', branches, lp['w_branch'])
    gates = jax.nn.sigmoid(p_gate.reshape(B_, L, N_BRANCH, D_MODEL))
    y = jnp.einsum('bld,de->ble', jnp.sum(gates * widened, axis=2), lp['w_out'])
    return y, k, v, s_r, s_s


def hier_moe(xn, lp):
    B_, L, D = xn.shape
    x = xn.reshape(B_ * L, D)
    g_logits = jnp.einsum('td,dg->tg', x, lp['moe_w_group']).astype(F32)
    g_prob = jax.nn.softmax(g_logits, axis=-1)
    g_top, g_sel = lax.top_k(g_logits, 1)
    g_w = jnp.take_along_axis(g_prob, g_sel, axis=1)
    g_onehot = jax.nn.one_hot(g_sel[:, 0], MOE_GROUPS, dtype=F32)
    e_logits = jnp.einsum('td,dge->tge', x, lp['moe_w_expert']).astype(F32)
    e_in = jnp.einsum('tge,tg->te', e_logits, g_onehot)
    top_v, top_i = lax.top_k(e_in, MOE_TOPK)
    top_w = jax.nn.softmax(top_v, axis=-1) * g_w
    expert_id = g_sel * MOE_PER_GROUP + top_i
    gate = jnp.sum(jax.nn.one_hot(expert_id, MOE_EXPERTS, dtype=F32) * top_w[..., None], axis=1)
    h1 = jnp.einsum('td,edf->tef', x, lp['moe_w1'])
    h3 = jnp.einsum('td,edf->tef', x, lp['moe_w3'])
    h = jax.nn.silu(h1) * h3 * gate[:, :, None].astype(x.dtype)
    y = jnp.einsum('tef,efd->td', h, lp['moe_w2'])
    return y.reshape(B_, L, D)


def trunk_layer(h, cond, lp, rope, ctx_k, ctx_v, s_rwkv, s_ssd):
    mod = jnp.matmul(jax.nn.silu(cond), lp['w_ada']) + lp['b_ada']
    if mod.ndim == 2:
        mod = mod[:, None, :]
    sh1, sc1, g1, sh2, sc2, g2 = jnp.split(mod, 6, axis=-1)
    xn = rmsnorm(h, lp['norm1_w']) * (1 + sc1) + sh1
    y, k, v, s_r, s_s = token_mix(xn, lp, rope, ctx_k, ctx_v, s_rwkv, s_ssd)
    h = h + g1 * y
    xn = rmsnorm(h, lp['norm2_w']) * (1 + sc2) + sh2
    h = h + g2 * hier_moe(xn, lp)
    return h, k, v, s_r, s_s


def setup_inputs(seed: int = 0) -> dict:
    key = jax.random.key(seed)
    ks = jax.random.split(key, 40)
    nrm = lambda k, shape, s: s * jax.random.normal(k, shape, jnp.float32)
    D = D_MODEL
    dt0 = jnp.exp(jax.random.uniform(ks[21], (DEPTH, 2, SSD_HEADS), jnp.float32, math.log(1e-3), math.log(1e-1)))
    return {
        'x_prompt': nrm(ks[0], (BATCH, SEQ, D), 1.0),
        'x_sample': nrm(ks[1], (DEC_BATCH, DEC_SEQ, D), 1.0),
        'cache_attn_k': nrm(ks[2], (DEC_BATCH, DEPTH, PAST_LEN, ATT_KV_HEADS, ATT_HD), 1.0),
        'cache_attn_v': nrm(ks[3], (DEC_BATCH, DEPTH, PAST_LEN, ATT_KV_HEADS, ATT_HD), 1.0),
        'state_rwkv': nrm(ks[4], (DEC_BATCH, DEPTH, 2, RWKV_HEADS, RWKV_N, RWKV_N), 0.5),
        'state_ssd': nrm(ks[5], (DEC_BATCH, DEPTH, 2, SSD_HEADS, SSD_P, SSD_N), 0.5),
        'c': nrm(ks[6], (DEC_BATCH, D), 1.0),
        'c_ctx': nrm(ks[7], (D,), 1.0),
        'w_ada': nrm(ks[8], (DEPTH, D, 6 * D), 0.5 * D ** -0.5),
        'b_ada': nrm(ks[9], (DEPTH, 6 * D), 0.02),
        'norm1_w': 1.0 + nrm(ks[10], (DEPTH, D), 0.02),
        'norm2_w': 1.0 + nrm(ks[11], (DEPTH, D), 0.02),
        'w_in': nrm(ks[12], (DEPTH, D, IN_COLS), D ** -0.5),
        'rwkv_mu': jax.random.uniform(ks[13], (DEPTH, RWKV_COLS), jnp.float32),
        'rwkv_w0': nrm(ks[14], (DEPTH, 2, RWKV_DIM), 1.0) - 1.0,
        'rwkv_w2': nrm(ks[15], (DEPTH, 2, RWKV_LORA_W, RWKV_DIM), 0.5 * RWKV_LORA_W ** -0.5),
        'rwkv_a0': nrm(ks[16], (DEPTH, 2, RWKV_DIM), 0.5),
        'rwkv_a2': nrm(ks[17], (DEPTH, 2, RWKV_LORA_A, RWKV_DIM), 0.5 * RWKV_LORA_A ** -0.5),
        'rwkv_g2': nrm(ks[18], (DEPTH, RWKV_LORA_G, RWKV_DIM), RWKV_LORA_G ** -0.5),
        'rwkv_k_k': 1.0 + nrm(ks[19], (DEPTH, RWKV_DIM), 0.1),
        'rwkv_k_a': 1.0 + nrm(ks[20], (DEPTH, RWKV_DIM), 0.1),
        'rwkv_r_k': nrm(ks[22], (DEPTH, RWKV_HEADS, RWKV_N), 0.1),
        'rwkv_lnx_w': 1.0 + nrm(ks[23], (DEPTH, RWKV_DIM), 0.02),
        'rwkv_lnx_b': nrm(ks[24], (DEPTH, RWKV_DIM), 0.02),
        'ssd_conv_w': nrm(ks[25], (DEPTH, SSD_CONV, SSD_CONV_DIM), SSD_CONV ** -0.5),
        'ssd_conv_b': nrm(ks[26], (DEPTH, SSD_CONV_DIM), 0.02),
        'ssd_dt_bias': dt0 + jnp.log(-jnp.expm1(-dt0)),
        'ssd_a_log': jnp.log(jax.random.uniform(ks[27], (DEPTH, 2, SSD_HEADS), jnp.float32, 1.0, 16.0)),
        'ssd_d': 1.0 + nrm(ks[28], (DEPTH, SSD_HEADS), 0.1),
        'ssd_norm_w': 1.0 + nrm(ks[29], (DEPTH, SSD_DIM), 0.02),
        'att_sink': nrm(ks[30], (DEPTH, ATT_HEADS), 0.5),
        'w_branch': nrm(ks[31], (DEPTH, N_BRANCH, BRANCH_DIM, D), BRANCH_DIM ** -0.5),
        'w_out': nrm(ks[32], (DEPTH, D, D), D ** -0.5),
        'moe_w_group': nrm(ks[33], (DEPTH, D, MOE_GROUPS), D ** -0.5),
        'moe_w_expert': nrm(ks[34], (DEPTH, D, MOE_GROUPS, MOE_PER_GROUP), D ** -0.5),
        'moe_w1': nrm(ks[35], (DEPTH, MOE_EXPERTS, D, D_EXPERT), D ** -0.5),
        'moe_w3': nrm(ks[36], (DEPTH, MOE_EXPERTS, D, D_EXPERT), D ** -0.5),
        'moe_w2': nrm(ks[37], (DEPTH, MOE_EXPERTS, D_EXPERT, D), D_EXPERT ** -0.5),
        'final_norm_w': 1.0 + nrm(ks[38], (D,), 0.02),
    }


def reference(x_prompt, x_sample, cache_attn_k, cache_attn_v, state_rwkv, state_ssd, c, c_ctx,
              w_ada, b_ada, norm1_w, norm2_w, w_in, rwkv_mu, rwkv_w0, rwkv_w2, rwkv_a0, rwkv_a2, rwkv_g2,
              rwkv_k_k, rwkv_k_a, rwkv_r_k, rwkv_lnx_w, rwkv_lnx_b, ssd_conv_w, ssd_conv_b, ssd_dt_bias,
              ssd_a_log, ssd_d, ssd_norm_w, att_sink, w_branch, w_out, moe_w_group, moe_w_expert,
              moe_w1, moe_w3, moe_w2, final_norm_w):
    params = dict(w_ada=w_ada, b_ada=b_ada, norm1_w=norm1_w, norm2_w=norm2_w, w_in=w_in, rwkv_mu=rwkv_mu,
                  rwkv_w0=rwkv_w0, rwkv_w2=rwkv_w2, rwkv_a0=rwkv_a0, rwkv_a2=rwkv_a2, rwkv_g2=rwkv_g2,
                  rwkv_k_k=rwkv_k_k, rwkv_k_a=rwkv_k_a, rwkv_r_k=rwkv_r_k, rwkv_lnx_w=rwkv_lnx_w,
                  rwkv_lnx_b=rwkv_lnx_b, ssd_conv_w=ssd_conv_w, ssd_conv_b=ssd_conv_b, ssd_dt_bias=ssd_dt_bias,
                  ssd_a_log=ssd_a_log, ssd_d=ssd_d, ssd_norm_w=ssd_norm_w, att_sink=att_sink, w_branch=w_branch,
                  w_out=w_out, moe_w_group=moe_w_group, moe_w_expert=moe_w_expert, moe_w1=moe_w1,
                  moe_w3=moe_w3, moe_w2=moe_w2)
    rows = x_sample.shape[1] // GRID_W
    rope = axial_rope(rows)
    h_p, h_s = x_prompt, x_sample
    ks_, vs_, srs_, sss_ = [], [], [], []
    for l in range(DEPTH):
        lp = {name: arr[l] for name, arr in params.items()}
        h_p, k_c, v_c, s_r, s_s = trunk_layer(h_p, c_ctx, lp, None, None, None, None, None)
        ks_.append(k_c)
        vs_.append(v_c)
        srs_.append(s_r)
        sss_.append(s_s)
        h_s = trunk_layer(h_s, c, lp, rope, cache_attn_k[:, l], cache_attn_v[:, l],
                          state_rwkv[:, l], state_ssd[:, l])[0]
    y_prompt = rmsnorm(h_p, final_norm_w)
    y_sample = rmsnorm(h_s, final_norm_w)
    new_cache_attn_k = jnp.stack(ks_, axis=1)
    new_cache_attn_v = jnp.stack(vs_, axis=1)
    new_state_rwkv = jnp.stack(srs_, axis=1)
    new_state_ssd = jnp.stack(sss_, axis=1)
    return (y_prompt, y_sample, new_cache_attn_k, new_cache_attn_v, new_state_rwkv, new_state_ssd)
```

```python
import functools
import math

import jax
import jax.numpy as jnp
from jax import lax
from jax.experimental import pallas as pl
from jax.experimental.pallas import tpu as pltpu

F32 = jnp.float32
BF16 = jnp.bfloat16
HIGHEST = lax.Precision.HIGHEST

D_MODEL = 1024
DEPTH = 4
N_CTX = 16
N_LAT = 2
LAT_LEN = 4096
PAST_LEN = 512
GRID_W = 64
NORM_EPS = 1e-6
TB = 256
NLB = LAT_LEN // TB
NBLK = N_CTX + N_LAT * NLB
T_CTX = N_CTX * TB
T_ALL = NBLK * TB
N_SEQ = N_CTX + N_LAT
HALO = 8

HEADS = 8
HD = 64
BR = HEADS * HD
RWKV_COLS = 1920
RWKV_CHUNK = 64
RWKV_GN_EPS = 64e-5
SSD_CONV = 5
SSD_XBC = 768
SSD_PCOLS = 1536
ATT_COLS = 768
ATT_KVH = 2
ATT_GROUP = 4
ATT_WINDOW = 128
ATT_QB = 128
ATT_SCALE = HD ** -0.5
ROPE_BASE = 10000.0
GATE_COLS = 3 * D_MODEL
MOE_GROUPS = 4
MOE_PER_GROUP = 8
MOE_EXPERTS = 32
D_EXPERT = 256
LANE = 128

VMEM_LIMIT = 48 * 1024 * 1024


def _cparams(sem):
    return pltpu.CompilerParams(dimension_semantics=sem, vmem_limit_bytes=VMEM_LIMIT)


def _dot(a, b, precision=None):
    return jnp.dot(a, b, precision=precision, preferred_element_type=F32)


def _dot_nt(a, b, precision=None):
    return lax.dot_general(a, b, (((1,), (1,)), ((), ())), precision=precision, preferred_element_type=F32)


def _dot_tn(a, b, precision=None):
    return lax.dot_general(a, b, (((0,), (0,)), ((), ())), precision=precision, preferred_element_type=F32)


def _sigmoid(x):
    return 1.0 / (1.0 + jnp.exp(-x))


def _silu(x):
    return x * _sigmoid(x)


def _softplus(x):
    return jnp.maximum(x, 0.0) + jnp.log(1.0 + jnp.exp(-jnp.abs(x)))


def _blk_type(i):
    return jnp.where(i < N_CTX, 0, 1 + (i - N_CTX) // NLB)


def _blk_seq(i):
    return jnp.where(i < N_CTX, i, N_CTX + (i - N_CTX) // NLB)


def _blk_first(i):
    return jnp.logical_or(i < N_CTX, (i - N_CTX) % NLB == 0)


def _blk_last(i):
    return jnp.logical_or(i < N_CTX, (i - N_CTX) % NLB == NLB - 1)


def _scan_blk(d, i):
    return i + d * (NBLK - 1 - 2 * i)


def _scan_ends(d, blk):
    fwd = d == 0
    a, b = _blk_first(blk), _blk_last(blk)
    first = jnp.logical_or(jnp.logical_and(fwd, a), jnp.logical_and(jnp.logical_not(fwd), b))
    last = jnp.logical_or(jnp.logical_and(fwd, b), jnp.logical_and(jnp.logical_not(fwd), a))
    return first, last


def _adaln_kernel(c_ref, w_ref, b_ref, o_ref):
    o_ref[0] = _dot(_silu(c_ref[...]), w_ref[0], HIGHEST) + b_ref[0]


def _adaln(cond8, w_ada, b_ada):
    nj = 6
    return pl.pallas_call(
        _adaln_kernel,
        grid=(DEPTH, nj),
        in_specs=[pl.BlockSpec((8, D_MODEL), lambda l, j: (0, 0)),
                  pl.BlockSpec((1, D_MODEL, D_MODEL), lambda l, j: (l, 0, j)),
                  pl.BlockSpec((1, 1, D_MODEL), lambda l, j: (l, 0, j))],
        out_specs=pl.BlockSpec((1, 8, D_MODEL), lambda l, j: (l, 0, j)),
        out_shape=jax.ShapeDtypeStruct((DEPTH, 8, 6 * D_MODEL), F32),
        compiler_params=_cparams(("parallel", "parallel")),
        name="adaln",
    )(cond8, w_ada, b_ada.reshape(DEPTH, 1, 6 * D_MODEL))


def _normmod(x, nw, shift, scale):
    y = x * lax.rsqrt(jnp.mean(x * x, axis=-1, keepdims=True) + NORM_EPS) * nw
    return y * (1.0 + scale) + shift


def _normproj_kernel(h_ref, mod_ref, nw_ref, w_ref, o_ref, xn_ref, *, slot):
    @pl.when(pl.program_id(1) == 0)
    def _():
        m = mod_ref[0]
        sh = m[:, slot * D_MODEL:(slot + 1) * D_MODEL]
        sc = m[:, (slot + 1) * D_MODEL:(slot + 2) * D_MODEL]
        xn_ref[...] = _normmod(h_ref[...], nw_ref[...], sh, sc).astype(BF16)
    o_ref[...] = _dot(xn_ref[...], w_ref[...])


def _normproj(h, mod, nw, w, tn, name):
    tm = 2 * TB
    n = w.shape[1]
    return pl.pallas_call(
        functools.partial(_normproj_kernel, slot=0),
        grid=(T_ALL // tm, n // tn),
        in_specs=[pl.BlockSpec((tm, D_MODEL), lambda i, j: (i, 0)),
                  pl.BlockSpec((1, 1, 6 * D_MODEL), lambda i, j: (_blk_type(2 * i), 0, 0)),
                  pl.BlockSpec((1, D_MODEL), lambda i, j: (0, 0)),
                  pl.BlockSpec((D_MODEL, tn), lambda i, j: (0, j))],
        out_specs=pl.BlockSpec((tm, tn), lambda i, j: (i, j)),
        out_shape=jax.ShapeDtypeStruct((T_ALL, n), F32),
        scratch_shapes=[pltpu.VMEM((tm, D_MODEL), BF16)],
        compiler_params=_cparams(("parallel", "arbitrary")),
        name=name,
    )(h, mod, nw, w)


def _with_halo(prev_ref, cur_ref, next_ref, i):
    prev = jnp.where(_blk_first(i), 0.0, prev_ref[...])
    nxt = jnp.where(_blk_last(i), 0.0, next_ref[...])
    return jnp.concatenate([prev, cur_ref[...], nxt], axis=0)


def _shifted(ext, s):
    n = ext.shape[0]
    return pltpu.roll(ext, (-s) % n, axis=0)[HALO:HALO + TB]


def _seg_sum(x, width):
    ii = lax.broadcasted_iota(jnp.int32, (LANE, LANE), 0)
    jj = lax.broadcasted_iota(jnp.int32, (LANE, LANE), 1)
    shift = int(math.log2(width))
    ones = ((ii >> shift) == (jj >> shift)).astype(F32)
    parts = [_dot(x[:, c:c + LANE], ones, HIGHEST) for c in range(0, x.shape[1], LANE)]
    return jnp.concatenate(parts, axis=1)


def _rwkv_prep_kernel(pp_ref, p_ref, pn_ref, mu_ref, w0_ref, w2_ref, a0_ref, a2_ref, g2_ref, kk_w_ref, ka_ref,
                      rk_ref, r_ref, v_ref, kk_ref, g_ref, bonus_ref, lw_ref, kd_ref, b_ref):
    i = pl.program_id(0)
    ext = _with_halo(pp_ref, p_ref, pn_ref, i)
    p = p_ref[...]
    p = p + mu_ref[...] * (0.5 * (_shifted(ext, -1) + _shifted(ext, 1)) - p)
    r, k, v = p[:, 0:BR], p[:, BR:2 * BR], p[:, 2 * BR:3 * BR]
    wd, ad, gd = p[:, 1536:1664], p[:, 1664:1792], p[:, 1792:1920]
    kk = k * kk_w_ref[...]
    kk = kk / jnp.maximum(jnp.sqrt(_seg_sum(kk * kk, HD)), 1e-12)
    r_ref[...] = r
    v_ref[...] = v
    kk_ref[...] = kk
    g_ref[...] = _dot(_sigmoid(gd), g2_ref[...], HIGHEST)
    tw = jnp.tanh(wd)
    kd_sum = jnp.zeros_like(k)
    for d in range(2):
        wl = w0_ref[d:d + 1, :] + _dot(tw, w2_ref[d], HIGHEST)
        lw_ref[d] = -_sigmoid(wl) * math.exp(-0.5)
        a = _sigmoid(a0_ref[d:d + 1, :] + _dot(ad, a2_ref[d], HIGHEST))
        kd = k * (1.0 + (a - 1.0) * ka_ref[...])
        kd_ref[d] = kd
        b_ref[d] = kk * a
        kd_sum = kd_sum + kd
    bonus_ref[...] = _seg_sum(r * kd_sum * rk_ref[...], HD) * v


def _halo_specs(cols):
    per = TB // HALO
    return [pl.BlockSpec((HALO, cols), lambda i: (jnp.maximum(i * per - 1, 0), 0)),
            pl.BlockSpec((TB, cols), lambda i: (i, 0)),
            pl.BlockSpec((HALO, cols), lambda i: (jnp.minimum((i + 1) * per, T_ALL // HALO - 1), 0))]


def _full(shape):
    return pl.BlockSpec(shape, lambda *_: (0,) * len(shape))


def _rwkv_prep(p_rwkv, lp):
    tok = jax.ShapeDtypeStruct((T_ALL, BR), F32)
    tok2 = jax.ShapeDtypeStruct((2, T_ALL, BR), F32)
    spec1 = pl.BlockSpec((TB, BR), lambda i: (i, 0))
    spec2 = pl.BlockSpec((2, TB, BR), lambda i: (0, i, 0))
    return pl.pallas_call(
        _rwkv_prep_kernel,
        grid=(NBLK,),
        in_specs=_halo_specs(RWKV_COLS) + [
            _full((1, RWKV_COLS)), _full((2, BR)), _full((2, LANE, BR)), _full((2, BR)), _full((2, LANE, BR)),
            _full((LANE, BR)), _full((1, BR)), _full((1, BR)), _full((1, BR))],
        out_specs=[spec1, spec1, spec1, spec1, spec1, spec2, spec2, spec2],
        out_shape=[tok, tok, tok, tok, tok, tok2, tok2, tok2],
        compiler_params=_cparams(("parallel",)),
        name="rwkv_prep",
    )(p_rwkv, p_rwkv, p_rwkv, lp['rwkv_mu'], lp['rwkv_w0'], lp['rwkv_w2p'], lp['rwkv_a0'], lp['rwkv_a2p'],
      lp['rwkv_g2'], lp['rwkv_k_k'], lp['rwkv_k_a'], lp['rwkv_r_k'])


def _pair_bd(x):
    h0 = lax.broadcasted_iota(jnp.int32, x.shape, 1) < HD
    return jnp.concatenate([jnp.where(h0, x, 0.0), jnp.where(h0, 0.0, x)], axis=0)


def _tri_inverse(n, ii, jj):
    eye = (ii == jj).astype(F32)
    d = jnp.where((ii >> 4) == (jj >> 4), n, 0.0)
    low = n - d
    d2 = _dot(d, d, HIGHEST)
    d4 = _dot(d2, d2, HIGHEST)
    d8 = _dot(d4, d4, HIGHEST)
    x = _dot(_dot(_dot(eye - d, eye + d2, HIGHEST), eye + d4, HIGHEST), eye + d8, HIGHEST)
    m = _dot(x, low, HIGHEST)
    m2 = _dot(m, m, HIGHEST)
    return _dot(_dot(eye - m, eye + m2, HIGHEST), x, HIGHEST)


def _rwkv_scan_kernel(r_ref, v_ref, kk_ref, lw_ref, kd_ref, b_ref, s0_ref, y_ref, sf_ref, s_ref):
    d = pl.program_id(0)
    blk = _scan_blk(d, pl.program_id(1))
    first, last = _scan_ends(d, blk)
    sgn = 1 - 2 * d
    C = RWKV_CHUNK
    nchunk = TB // C

    @pl.when(first)
    def _():
        s_ref[...] = s0_ref[0, 0]

    ci = lax.broadcasted_iota(jnp.int32, (C, C), 0)
    cj = lax.broadcasted_iota(jnp.int32, (C, C), 1)
    tri = ((cj - ci) * sgn <= 0).astype(F32)
    ii = lax.broadcasted_iota(jnp.int32, (LANE, LANE), 0)
    jj = lax.broadcasted_iota(jnp.int32, (LANE, LANE), 1)
    same = (ii >> 6) == (jj >> 6)
    rel = ((jj & (C - 1)) - (ii & (C - 1))) * sgn
    before = jnp.logical_and(same, rel < 0)
    before_incl = jnp.logical_and(same, rel <= 0)
    ones_c = jnp.ones((C, LANE), F32)

    def chunk(c, carry):
        cc = c + d * (nchunk - 1 - 2 * c)
        rows = pl.ds(pl.multiple_of(cc * C, C), C)
        lw = lw_ref[0, rows, :]
        cum = _dot(tri, lw, HIGHEST)
        tot = jnp.sum(lw, axis=0, keepdims=True)
        e_neg = jnp.exp(-cum)
        e_end = jnp.exp(tot - cum)
        r = r_ref[rows, :]
        v = v_ref[rows, :]
        kk = kk_ref[rows, :]
        kd = kd_ref[0, rows, :]
        b = b_ref[0, rows, :]
        k_t, b_t = kd * e_neg, b * e_neg
        kk_h, r_h = kk * jnp.exp(cum - lw), r * jnp.exp(cum)
        k_e, b_e = kd * e_end, b * e_end
        for p in range(BR // LANE):
            sl = slice(p * LANE, (p + 1) * LANE)
            KK, R, KT, BT, V, KE, BE = (_pair_bd(x[:, sl]) for x in (kk_h, r_h, k_t, b_t, v, k_e, b_e))
            g = _dot_nt(jnp.concatenate([KK, R], axis=0), jnp.concatenate([BT, KT], axis=0), HIGHEST)
            n = jnp.where(before, g[:LANE, :LANE], 0.0)
            a1 = jnp.where(before, g[:LANE, LANE:], 0.0)
            a4 = jnp.where(before_incl, g[LANE:, :LANE], 0.0)
            a3 = jnp.where(before_incl, g[LANE:, LANE:], 0.0)
            tinv = _tri_inverse(n, ii, jj)
            s = s_ref[p]
            rhs = _dot(jnp.concatenate([KK, a1], axis=1), jnp.concatenate([s, V], axis=0), HIGHEST)
            u = -_dot(tinv, rhs, HIGHEST)
            y = _dot(jnp.concatenate([R, a3, a4], axis=1), jnp.concatenate([s, V, u], axis=0), HIGHEST)
            y_ref[0, rows, sl] = y[:C] + y[C:]
            wc = jnp.exp(_dot_tn(lw[:, sl], ones_c, HIGHEST))
            s_ref[p] = s * wc + _dot_tn(jnp.concatenate([KE, BE], axis=0), jnp.concatenate([V, u], axis=0), HIGHEST)
        return carry

    lax.fori_loop(0, nchunk, chunk, 0)

    @pl.when(last)
    def _():
        sf_ref[0, 0] = s_ref[...]


def _rwkv_scan(r, v, kk, lw, kd, b, s0):
    np_ = BR // LANE
    spec1 = pl.BlockSpec((TB, BR), lambda d, i: (_scan_blk(d, i), 0))
    spec2 = pl.BlockSpec((1, TB, BR), lambda d, i: (d, _scan_blk(d, i), 0))
    sspec = pl.BlockSpec((1, 1, np_, LANE, LANE), lambda d, i: (d, _blk_seq(_scan_blk(d, i)), 0, 0, 0))
    return pl.pallas_call(
        _rwkv_scan_kernel,
        grid=(2, NBLK),
        in_specs=[spec1, spec1, spec1, spec2, spec2, spec2, sspec],
        out_specs=[spec2, sspec],
        out_shape=[jax.ShapeDtypeStruct((2, T_ALL, BR), F32),
                   jax.ShapeDtypeStruct((2, N_SEQ, np_, LANE, LANE), F32)],
        scratch_shapes=[pltpu.VMEM((np_, LANE, LANE), F32)],
        compiler_params=_cparams(("arbitrary", "arbitrary")),
        name="rwkv_scan",
    )(r, v, kk, lw, kd, b, s0)


def _ssd_prep_kernel(pp_ref, p_ref, pn_ref, cw_ref, cb_ref, dtb_ref, x_ref, bc_ref, dt_ref):
    i = pl.program_id(0)
    ext = _with_halo(pp_ref, p_ref, pn_ref, i)[:, BR:BR + SSD_XBC]
    acc = cb_ref[...] + cw_ref[0:1, :] * _shifted(ext, -(SSD_CONV // 2))
    for j in range(1, SSD_CONV):
        acc = acc + cw_ref[j:j + 1, :] * _shifted(ext, j - SSD_CONV // 2)
    xbc = _silu(acc)
    x_ref[...] = xbc[:, :BR]
    bc_ref[...] = xbc[:, BR:]
    dt_ref[...] = _softplus(p_ref[:, BR + SSD_XBC:] + dtb_ref[...])


def _ssd_prep(p_ssd, lp):
    return pl.pallas_call(
        _ssd_prep_kernel,
        grid=(NBLK,),
        in_specs=_halo_specs(SSD_PCOLS) + [_full((8, SSD_XBC)), _full((1, SSD_XBC)), _full((1, 2 * LANE))],
        out_specs=[pl.BlockSpec((TB, BR), lambda i: (i, 0)),
                   pl.BlockSpec((TB, 2 * LANE), lambda i: (i, 0)),
                   pl.BlockSpec((TB, 2 * LANE), lambda i: (i, 0))],
        out_shape=[jax.ShapeDtypeStruct((T_ALL, BR), F32),
                   jax.ShapeDtypeStruct((T_ALL, 2 * LANE), F32),
                   jax.ShapeDtypeStruct((T_ALL, 2 * LANE), F32)],
        compiler_params=_cparams(("parallel",)),
        name="ssd_prep",
    )(p_ssd, p_ssd, p_ssd, lp['ssd_conv_w8'], lp['ssd_conv_b'], lp['ssd_dt_bias_p'])


def _ssd_scan_kernel(x_ref, bc_ref, dt_ref, alog_ref, h0_ref, y_ref, hf_ref, hs_ref):
    d = pl.program_id(0)
    blk = _scan_blk(d, pl.program_id(1))
    first, last = _scan_ends(d, blk)
    sgn = 1 - 2 * d

    @pl.when(first)
    def _():
        hs_ref[...] = h0_ref[0, 0]

    x = x_ref[...]
    bm = bc_ref[:, :LANE]
    cm = bc_ref[:, LANE:]
    dt = dt_ref[...]
    a_neg = -jnp.exp(alog_ref[0])
    dta = dt * a_neg
    qi = lax.broadcasted_iota(jnp.int32, (TB, TB), 0)
    qj = lax.broadcasted_iota(jnp.int32, (TB, TB), 1)
    before_incl = (qj - qi) * sgn <= 0
    a_cum = _dot(before_incl.astype(F32), dta, HIGHEST)
    tot = jnp.sum(dta, axis=0, keepdims=True)
    a_cum_t = a_cum.T
    dt_t = dt.T
    eh = lax.broadcasted_iota(jnp.int32, (LANE, BR), 0)
    ec = lax.broadcasted_iota(jnp.int32, (LANE, BR), 1)
    expand = ((ec >> 6) == eh).astype(F32)
    e_in = _dot(jnp.exp(a_cum), expand, HIGHEST)
    to_end = _dot(jnp.exp(tot - a_cum) * dt, expand, HIGHEST)
    dec = _dot(jnp.broadcast_to(jnp.exp(tot), (8, LANE)), expand, HIGHEST)[0:1]
    hs = hs_ref[...]
    cb16, bb16, xb16 = cm.astype(BF16), bm.astype(BF16), x.astype(BF16)
    y_off = _dot(cb16, hs.astype(BF16)) * e_in
    glane = lax.broadcasted_iota(jnp.int32, (TB, LANE), 1) >> 6
    half = lax.broadcasted_iota(jnp.int32, (TB, LANE), 1) < HD
    y_parts = []
    for g in range(2):
        cbg = _dot_nt(jnp.where(glane == g, cm, 0.0).astype(BF16), bb16)
        for pr in range(2):
            ys = []
            for hh in range(2):
                h = 4 * g + 2 * pr + hh
                seg = a_cum[:, h:h + 1] - a_cum_t[h:h + 1, :]
                decay = jnp.exp(jnp.where(before_incl, seg, -jnp.inf))
                sc = cbg * decay * dt_t[h:h + 1, :]
                p0 = (2 * g + pr) * LANE
                ys.append(_dot(sc.astype(BF16), xb16[:, p0:p0 + LANE]))
            y_parts.append(jnp.where(half, ys[0], ys[1]))
    y_ref[0] = jnp.concatenate(y_parts, axis=1) + y_off
    upd = _dot_tn(bb16, (x * to_end).astype(BF16))
    ui = lax.broadcasted_iota(jnp.int32, (LANE, BR), 0)
    uj = lax.broadcasted_iota(jnp.int32, (LANE, BR), 1)
    hs_ref[...] = hs * dec + jnp.where((ui >> 6) == (uj >> 8), upd, 0.0)

    @pl.when(last)
    def _():
        hf_ref[0, 0] = hs_ref[...]


def _ssd_scan(x, bc, dt, alog, h0):
    sspec = pl.BlockSpec((1, 1, LANE, BR), lambda d, i: (d, _blk_seq(_scan_blk(d, i)), 0, 0))
    return pl.pallas_call(
        _ssd_scan_kernel,
        grid=(2, NBLK),
        in_specs=[pl.BlockSpec((TB, BR), lambda d, i: (_scan_blk(d, i), 0)),
                  pl.BlockSpec((TB, 2 * LANE), lambda d, i: (_scan_blk(d, i), 0)),
                  pl.BlockSpec((TB, LANE), lambda d, i: (_scan_blk(d, i), d)),
                  pl.BlockSpec((1, 1, LANE), lambda d, i: (d, 0, 0)),
                  sspec],
        out_specs=[pl.BlockSpec((1, TB, BR), lambda d, i: (d, _scan_blk(d, i), 0)), sspec],
        out_shape=[jax.ShapeDtypeStruct((2, T_ALL, BR), F32),
                   jax.ShapeDtypeStruct((2, N_SEQ, LANE, BR), F32)],
        scratch_shapes=[pltpu.VMEM((LANE, BR), F32)],
        compiler_params=_cparams(("arbitrary", "arbitrary")),
        name="ssd_scan",
    )(x, bc, dt, alog, h0)


def _sink_softmax(s, sink_col):
    m = jnp.maximum(jnp.max(s, axis=-1, keepdims=True), sink_col)
    e = jnp.exp(s - m)
    return e / (jnp.sum(e, axis=-1, keepdims=True) + jnp.exp(sink_col - m))


def _gqa(q, k_all, v_all, sink_ref, valid):
    nq = q.shape[0]
    outs = []
    for kvh in range(ATT_KVH):
        kh = k_all[:, kvh * HD:(kvh + 1) * HD].astype(BF16)
        vh = v_all[:, kvh * HD:(kvh + 1) * HD].astype(BF16)
        heads = range(kvh * ATT_GROUP, (kvh + 1) * ATT_GROUP)
        qg = jnp.concatenate([q[:, h * HD:(h + 1) * HD] for h in heads], axis=0).astype(BF16)
        s = _dot_nt(qg, kh) * ATT_SCALE
        if valid is not None:
            s = jnp.where(valid, s, -jnp.inf)
        sink_col = jnp.concatenate([jnp.full((nq, 1), sink_ref[h], F32) for h in heads], axis=0)
        o = _dot(_sink_softmax(s, sink_col).astype(BF16), vh)
        outs += [o[g * nq:(g + 1) * nq] for g in range(ATT_GROUP)]
    return jnp.concatenate(outs, axis=1)


def _ctx_attn_kernel(sink_ref, p_ref, o_ref):
    p = p_ref[...]
    o_ref[...] = _gqa(p[:, :BR], p[:, BR:BR + LANE], p[:, BR + LANE:], sink_ref, None)


def _ctx_attn(p_att, sink):
    return pl.pallas_call(
        _ctx_attn_kernel,
        grid=(N_CTX,),
        in_specs=[pl.BlockSpec(memory_space=pltpu.SMEM), pl.BlockSpec((TB, ATT_COLS), lambda i: (i, 0))],
        out_specs=pl.BlockSpec((TB, BR), lambda i: (i, 0)),
        out_shape=jax.ShapeDtypeStruct((T_CTX, BR), F32),
        compiler_params=_cparams(("parallel",)),
        name="ctx_attn",
    )(sink, p_att)


def _rope(x, cos, sin_signed):
    lanes = x.shape[1]
    reps = lanes // LANE
    if reps > 1:
        cos = jnp.concatenate([cos] * reps, axis=1)
        sin_signed = jnp.concatenate([sin_signed] * reps, axis=1)
    lo = (lax.broadcasted_iota(jnp.int32, x.shape, 1) & 31) < 16
    partner = jnp.where(lo, pltpu.roll(x, lanes - 16, axis=1), pltpu.roll(x, 16, axis=1))
    return x * cos + partner * sin_signed


def _lat_attn_kernel(sink_ref, pq_ref, pp_ref, pn_ref, cq_ref, sq_ref, cp_ref, sp_ref, cn_ref, sn_ref,
                     kc_ref, vc_ref, o_ref):
    j = pl.program_id(1)
    nb = LAT_LEN // ATT_QB
    q = _rope(pq_ref[:, :BR], cq_ref[...], sq_ref[...])
    k_loc = jnp.concatenate([_rope(pp_ref[:, BR:BR + LANE], cp_ref[...], sp_ref[...]),
                             _rope(pq_ref[:, BR:BR + LANE], cq_ref[...], sq_ref[...]),
                             _rope(pn_ref[:, BR:BR + LANE], cn_ref[...], sn_ref[...])], axis=0)
    v_loc = jnp.concatenate([pp_ref[:, BR + LANE:], pq_ref[:, BR + LANE:], pn_ref[:, BR + LANE:]], axis=0)
    k_all = jnp.concatenate([k_loc, kc_ref[0]], axis=0)
    v_all = jnp.concatenate([v_loc, vc_ref[0]], axis=0)
    nk = 3 * ATT_QB + PAST_LEN
    qi = lax.broadcasted_iota(jnp.int32, (ATT_GROUP * ATT_QB, nk), 0) & (ATT_QB - 1)
    kj = lax.broadcasted_iota(jnp.int32, (ATT_GROUP * ATT_QB, nk), 1)
    rel = kj - ATT_QB - qi
    kpos = (j - 1) * ATT_QB + kj
    valid = (((rel <= ATT_WINDOW) & (rel >= -ATT_WINDOW) & (kpos >= 0) & (kpos < nb * ATT_QB))
             | (kj >= 3 * ATT_QB))
    o_ref[...] = _gqa(q, k_all, v_all, sink_ref, valid)


def _lat_attn(p_att, sink, rope_cos, rope_sin, k_ctx, v_ctx):
    nb = LAT_LEN // ATT_QB
    base = T_CTX // ATT_QB
    row = lambda b, j: base + b * nb + j
    pspec = lambda f: pl.BlockSpec((ATT_QB, ATT_COLS), lambda b, j: (row(b, f(j)), 0))
    tspec = lambda f: pl.BlockSpec((ATT_QB, LANE), lambda b, j: (f(j), 0))
    cur = lambda j: j
    prv = lambda j: jnp.maximum(j - 1, 0)
    nxt = lambda j: jnp.minimum(j + 1, nb - 1)
    cspec = pl.BlockSpec((1, PAST_LEN, LANE), lambda b, j: (b, 0, 0))
    return pl.pallas_call(
        _lat_attn_kernel,
        grid=(N_LAT, nb),
        in_specs=[pl.BlockSpec(memory_space=pltpu.SMEM), pspec(cur), pspec(prv), pspec(nxt),
                  tspec(cur), tspec(cur), tspec(prv), tspec(prv), tspec(nxt), tspec(nxt), cspec, cspec],
        out_specs=pl.BlockSpec((ATT_QB, BR), lambda b, j: (b * nb + j, 0)),
        out_shape=jax.ShapeDtypeStruct((N_LAT * LAT_LEN, BR), F32),
        compiler_params=_cparams(("parallel", "parallel")),
        name="lat_attn",
    )(sink, p_att, p_att, p_att, rope_cos, rope_sin, rope_cos, rope_sin, rope_cos, rope_sin, k_ctx, v_ctx)


def _merge_kernel(h_ref, mod_ref, yr_ref, bonus_ref, g_ref, lnw_ref, lnb_ref, ys_ref, xs_ref, z_ref, dvec_ref,
                  snw_ref, ya_ref, pg_ref, wb_ref, wo_ref, o_ref):
    y = yr_ref[0] + yr_ref[1] + bonus_ref[...]
    mu = _seg_sum(y, HD) * (1.0 / HD)
    yc = y - mu
    var = _seg_sum(yc * yc, HD) * (1.0 / HD)
    y_a = (yc * lax.rsqrt(var + RWKV_GN_EPS) * lnw_ref[...] + lnb_ref[...]) * g_ref[...]
    y = (ys_ref[0] + ys_ref[1] + dvec_ref[...] * xs_ref[...]) * _silu(z_ref[...])
    y_b = y * lax.rsqrt(jnp.mean(y * y, axis=-1, keepdims=True) + NORM_EPS) * snw_ref[...]
    merged = None
    for n, br in enumerate((y_a, y_b, ya_ref[...])):
        wide = _dot(br.astype(BF16), wb_ref[n])
        term = _sigmoid(pg_ref[:, n * D_MODEL:(n + 1) * D_MODEL]) * wide
        merged = term if merged is None else merged + term
    g1 = mod_ref[0][:, 2 * D_MODEL:3 * D_MODEL]
    o_ref[...] = h_ref[...] + g1 * _dot(merged.astype(BF16), wo_ref[...])


def _merge(h, mod, yr, bonus, g, ys, xs, p_ssd, ya, p_gate, lp):
    tm = TB
    row = lambda w: pl.BlockSpec((tm, w), lambda i: (i, 0))
    row2 = pl.BlockSpec((2, tm, BR), lambda i: (0, i, 0))
    return pl.pallas_call(
        _merge_kernel,
        grid=(T_ALL // tm,),
        in_specs=[row(D_MODEL), pl.BlockSpec((1, 1, 6 * D_MODEL), lambda i: (_blk_type(i), 0, 0)),
                  row2, row(BR), row(BR), _full((1, BR)), _full((1, BR)),
                  row2, row(BR), row(BR), _full((1, BR)), _full((1, BR)),
                  row(BR), row(GATE_COLS), _full((3, BR, D_MODEL)), _full((D_MODEL, D_MODEL))],
        out_specs=row(D_MODEL),
        out_shape=jax.ShapeDtypeStruct((T_ALL, D_MODEL), F32),
        compiler_params=_cparams(("parallel",)),
        name="merge",
    )(h, mod, yr, bonus, g, lp['rwkv_lnx_w'], lp['rwkv_lnx_b'], ys, xs, p_ssd, lp['ssd_d_vec'], lp['ssd_norm_w'],
      ya, p_gate, lp['w_branch'], lp['w_out'])


def _route_kernel(h_ref, mod_ref, nw_ref, wr_ref, xn_ref, gate_ref):
    m = mod_ref[0]
    xn = _normmod(h_ref[...], nw_ref[...], m[:, 3 * D_MODEL:4 * D_MODEL], m[:, 4 * D_MODEL:5 * D_MODEL])
    xn_ref[...] = xn.astype(BF16)
    logits = _dot(xn, wr_ref[...], HIGHEST)
    lane_i = lax.broadcasted_iota(jnp.int32, logits.shape, 1)
    lane = lane_i.astype(F32)
    lane_grp = (lane_i >> 3).astype(F32)
    neg = -jnp.inf
    big = float(LANE)
    is_g = (lane_i >= MOE_EXPERTS) & (lane_i < MOE_EXPERTS + MOE_GROUPS)
    gl = jnp.where(is_g, logits, neg)
    gmax = jnp.max(gl, axis=-1, keepdims=True)
    gsel = jnp.min(jnp.where(gl == gmax, lane - MOE_EXPERTS, big), axis=-1, keepdims=True)
    g_w = 1.0 / jnp.sum(jnp.where(is_g, jnp.exp(gl - gmax), 0.0), axis=-1, keepdims=True)
    el = jnp.where((lane_i < MOE_EXPERTS) & (lane_grp == gsel), logits, neg)
    m1 = jnp.max(el, axis=-1, keepdims=True)
    i1 = jnp.min(jnp.where(el == m1, lane, big), axis=-1, keepdims=True)
    el2 = jnp.where(lane == i1, neg, el)
    m2 = jnp.max(el2, axis=-1, keepdims=True)
    i2 = jnp.min(jnp.where(el2 == m2, lane, big), axis=-1, keepdims=True)
    e2 = jnp.exp(m2 - m1)
    w1 = 1.0 / (1.0 + e2)
    gate_ref[...] = jnp.where(lane == i1, w1 * g_w, jnp.where(lane == i2, e2 * w1 * g_w, 0.0))


def _route(h, mod, nw, wr):
    tm = TB
    return pl.pallas_call(
        _route_kernel,
        grid=(T_ALL // tm,),
        in_specs=[pl.BlockSpec((tm, D_MODEL), lambda i: (i, 0)),
                  pl.BlockSpec((1, 1, 6 * D_MODEL), lambda i: (_blk_type(i), 0, 0)),
                  _full((1, D_MODEL)), _full((D_MODEL, LANE))],
        out_specs=[pl.BlockSpec((tm, D_MODEL), lambda i: (i, 0)), pl.BlockSpec((tm, LANE), lambda i: (i, 0))],
        out_shape=[jax.ShapeDtypeStruct((T_ALL, D_MODEL), BF16), jax.ShapeDtypeStruct((T_ALL, LANE), F32)],
        compiler_params=_cparams(("parallel",)),
        name="moe_route",
    )(h, mod, nw, wr)


def _experts_kernel(xn_ref, gate_ref, w13_ref, w2_ref, h_ref, mod_ref, o_ref, acc_ref):
    e = pl.program_id(1)

    @pl.when(e == 0)
    def _():
        acc_ref[...] = jnp.zeros_like(acc_ref)

    gate = gate_ref[...]
    lane = lax.broadcasted_iota(jnp.int32, gate.shape, 1)
    gcol = jnp.sum(jnp.where(lane == e, gate, 0.0), axis=-1, keepdims=True)
    h13 = _dot(xn_ref[...], w13_ref[0])
    act = _silu(h13[:, :D_EXPERT]) * h13[:, D_EXPERT:] * gcol
    acc_ref[...] += _dot(act.astype(BF16), w2_ref[0])

    @pl.when(e == MOE_EXPERTS - 1)
    def _():
        g2 = mod_ref[0][:, 5 * D_MODEL:6 * D_MODEL]
        o_ref[...] = h_ref[...] + g2 * acc_ref[...]


def _experts(xn, gate, w13, w2, h, mod):
    tm = 4 * TB
    return pl.pallas_call(
        _experts_kernel,
        grid=(T_ALL // tm, MOE_EXPERTS),
        in_specs=[pl.BlockSpec((tm, D_MODEL), lambda i, e: (i, 0)),
                  pl.BlockSpec((tm, LANE), lambda i, e: (i, 0)),
                  pl.BlockSpec((1, D_MODEL, 2 * D_EXPERT), lambda i, e: (e, 0, 0)),
                  pl.BlockSpec((1, D_EXPERT, D_MODEL), lambda i, e: (e, 0, 0)),
                  pl.BlockSpec((tm, D_MODEL), lambda i, e: (i, 0)),
                  pl.BlockSpec((1, 1, 6 * D_MODEL), lambda i, e: (_blk_type(4 * i), 0, 0))],
        out_specs=pl.BlockSpec((tm, D_MODEL), lambda i, e: (i, 0)),
        out_shape=jax.ShapeDtypeStruct((T_ALL, D_MODEL), F32),
        scratch_shapes=[pltpu.VMEM((tm, D_MODEL), F32)],
        compiler_params=_cparams(("parallel", "arbitrary")),
        name="moe_experts",
    )(xn, gate, w13, w2, h, mod)


def _final_norm_kernel(h_ref, w_ref, o_ref):
    x = h_ref[...]
    o_ref[...] = x * lax.rsqrt(jnp.mean(x * x, axis=-1, keepdims=True) + NORM_EPS) * w_ref[...]


def _final_norm(h, w):
    tm = 2 * TB
    return pl.pallas_call(
        _final_norm_kernel,
        grid=(T_ALL // tm,),
        in_specs=[pl.BlockSpec((tm, D_MODEL), lambda i: (i, 0)), _full((1, D_MODEL))],
        out_specs=pl.BlockSpec((tm, D_MODEL), lambda i: (i, 0)),
        out_shape=jax.ShapeDtypeStruct((T_ALL, D_MODEL), F32),
        compiler_params=_cparams(("parallel",)),
        name="final_norm",
    )(h, w)


def _rope_tables():
    pos = jnp.arange(LAT_LEN)
    row = (pos // GRID_W).astype(F32)
    col = (pos % GRID_W).astype(F32)
    half = HD // 2
    inv = 1.0 / (ROPE_BASE ** (jnp.arange(0, half, 2, dtype=F32) / half))
    ar, ac = row[:, None] * inv[None, :], col[:, None] * inv[None, :]
    cos = jnp.concatenate([jnp.cos(ar), jnp.cos(ar), jnp.cos(ac), jnp.cos(ac)], axis=1)
    sin = jnp.concatenate([-jnp.sin(ar), jnp.sin(ar), -jnp.sin(ac), jnp.sin(ac)], axis=1)
    return jnp.tile(cos, (1, 2)), jnp.tile(sin, (1, 2))


def _rwkv_state_in(state):
    s = jnp.swapaxes(state, -1, -2).reshape(N_LAT, 2, HEADS // 2, 2, HD, HD)
    eye = jnp.eye(2, dtype=F32)
    bd = jnp.einsum('bdpjkv,jm->bdpjkmv', s, eye).reshape(N_LAT, 2, HEADS // 2, LANE, LANE)
    bd = jnp.moveaxis(bd, 1, 0)
    zeros = jnp.zeros((2, N_CTX, HEADS // 2, LANE, LANE), F32)
    return jnp.concatenate([zeros, bd], axis=1)


def _rwkv_state_out(sf):
    s = sf[:, :N_CTX].reshape(2, N_CTX, HEADS // 2, 2, HD, 2, HD)
    diag = jnp.stack([s[:, :, :, 0, :, 0, :], s[:, :, :, 1, :, 1, :]], axis=3)
    out = jnp.swapaxes(diag.reshape(2, N_CTX, HEADS, HD, HD), -1, -2)
    return jnp.moveaxis(out, 0, 1)


def _ssd_state_in(state):
    s = jnp.transpose(state, (1, 0, 4, 2, 3))
    grp = (jnp.arange(HEADS) // 4)[None, :] == jnp.arange(2)[:, None]
    full = jnp.where(grp[None, None, :, None, :, None], s[:, :, None], 0.0)
    full = full.reshape(2, N_LAT, LANE, BR)
    return jnp.concatenate([jnp.zeros((2, N_CTX, LANE, BR), F32), full], axis=1)


def _ssd_state_out(hf):
    s = hf[:, :N_CTX].reshape(2, N_CTX, 2, HD, HEADS, HD)
    per_head = jnp.stack([s[:, :, h // 4, :, h, :] for h in range(HEADS)], axis=2)
    return jnp.transpose(per_head, (1, 0, 2, 4, 3))


def _layer_params(l, w_in, prm):
    lp = {}
    wi = w_in[l]
    o_ssd, o_att, o_gate = RWKV_COLS, RWKV_COLS + 1296, RWKV_COLS + 1296 + ATT_COLS
    lp['w_rwkv'] = wi[:, :o_ssd].astype(BF16)
    dt_cols = wi[:, o_ssd + 1280:o_ssd + 1296]
    pad = jnp.zeros((D_MODEL, LANE - HEADS), F32)
    lp['w_ssd'] = jnp.concatenate([wi[:, o_ssd:o_ssd + 1280], dt_cols[:, :HEADS], pad, dt_cols[:, HEADS:], pad],
                                  axis=1).astype(BF16)
    lp['w_att'] = wi[:, o_att:o_gate].astype(BF16)
    lp['w_gate'] = wi[:, o_gate:].astype(BF16)
    row = lambda name: prm[name][l].reshape(1, -1)
    for name in ('norm1_w', 'norm2_w', 'rwkv_mu', 'rwkv_k_k', 'rwkv_k_a', 'rwkv_r_k', 'rwkv_lnx_w', 'rwkv_lnx_b',
                 'ssd_conv_b', 'ssd_norm_w'):
        lp[name] = row(name)
    lp['rwkv_w0'] = prm['rwkv_w0'][l]
    lp['rwkv_a0'] = prm['rwkv_a0'][l]
    z64 = jnp.zeros((HD, BR), F32)
    w2, a2 = prm['rwkv_w2'][l], prm['rwkv_a2'][l]
    lp['rwkv_w2p'] = jnp.stack([jnp.concatenate([w2[0], z64]), jnp.concatenate([z64, w2[1]])])
    lp['rwkv_a2p'] = jnp.stack([jnp.concatenate([a2[0], z64]), jnp.concatenate([z64, a2[1]])])
    lp['rwkv_g2'] = prm['rwkv_g2'][l]
    lp['ssd_conv_w8'] = jnp.concatenate([prm['ssd_conv_w'][l], jnp.zeros((8 - SSD_CONV, SSD_XBC), F32)])
    dtb = prm['ssd_dt_bias'][l]
    zp = jnp.zeros((LANE - HEADS,), F32)
    lp['ssd_dt_bias_p'] = jnp.concatenate([dtb[0], zp, dtb[1], zp]).reshape(1, 2 * LANE)
    lp['ssd_a_log_p'] = jnp.pad(prm['ssd_a_log'][l], ((0, 0), (0, LANE - HEADS))).reshape(2, 1, LANE)
    lp['ssd_d_vec'] = jnp.repeat(prm['ssd_d'][l], HD).reshape(1, BR)
    lp['att_sink'] = prm['att_sink'][l]
    lp['w_branch'] = prm['w_branch'][l].astype(BF16)
    lp['w_out'] = prm['w_out'][l].astype(BF16)
    wr = jnp.concatenate([prm['moe_w_expert'][l].reshape(D_MODEL, MOE_EXPERTS), prm['moe_w_group'][l]], axis=1)
    lp['w_route'] = jnp.pad(wr, ((0, 0), (0, LANE - MOE_EXPERTS - MOE_GROUPS)))
    lp['w13'] = jnp.concatenate([prm['moe_w1'][l], prm['moe_w3'][l]], axis=-1).astype(BF16)
    lp['w2'] = prm['moe_w2'][l].astype(BF16)
    return lp


def kernel(x_prompt, x_sample, cache_attn_k, cache_attn_v, state_rwkv, state_ssd, c, c_ctx, w_ada, b_ada, norm1_w, norm2_w, w_in, rwkv_mu, rwkv_w0, rwkv_w2, rwkv_a0, rwkv_a2, rwkv_g2, rwkv_k_k, rwkv_k_a, rwkv_r_k, rwkv_lnx_w, rwkv_lnx_b, ssd_conv_w, ssd_conv_b, ssd_dt_bias, ssd_a_log, ssd_d, ssd_norm_w, att_sink, w_branch, w_out, moe_w_group, moe_w_expert, moe_w1, moe_w3, moe_w2, final_norm_w):
    prm = dict(norm1_w=norm1_w, norm2_w=norm2_w, rwkv_mu=rwkv_mu, rwkv_w0=rwkv_w0, rwkv_w2=rwkv_w2,
               rwkv_a0=rwkv_a0, rwkv_a2=rwkv_a2, rwkv_g2=rwkv_g2, rwkv_k_k=rwkv_k_k, rwkv_k_a=rwkv_k_a,
               rwkv_r_k=rwkv_r_k.reshape(DEPTH, BR), rwkv_lnx_w=rwkv_lnx_w, rwkv_lnx_b=rwkv_lnx_b,
               ssd_conv_w=ssd_conv_w, ssd_conv_b=ssd_conv_b, ssd_dt_bias=ssd_dt_bias, ssd_a_log=ssd_a_log,
               ssd_d=ssd_d, ssd_norm_w=ssd_norm_w, att_sink=att_sink, w_branch=w_branch, w_out=w_out,
               moe_w_group=moe_w_group, moe_w_expert=moe_w_expert, moe_w1=moe_w1, moe_w3=moe_w3, moe_w2=moe_w2)
    cond8 = jnp.concatenate([c_ctx[None, :], c, jnp.zeros((8 - 1 - N_LAT, D_MODEL), F32)], axis=0)
    mod_all = _adaln(cond8, w_ada, b_ada)
    rope_cos, rope_sin = _rope_tables()
    h = jnp.concatenate([x_prompt.reshape(T_CTX, D_MODEL), x_sample.reshape(N_LAT * LAT_LEN, D_MODEL)], axis=0)
    ks, vs, srs, sss = [], [], [], []
    for l in range(DEPTH):
        lp = _layer_params(l, w_in, prm)
        mod = mod_all[l].reshape(8, 1, 6 * D_MODEL)
        p_rwkv = _normproj(h, mod, lp['norm1_w'], lp['w_rwkv'], RWKV_COLS, "proj_rwkv")
        p_ssd = _normproj(h, mod, lp['norm1_w'], lp['w_ssd'], SSD_PCOLS, "proj_ssd")
        p_att = _normproj(h, mod, lp['norm1_w'], lp['w_att'], ATT_COLS, "proj_att")
        p_gate = _normproj(h, mod, lp['norm1_w'], lp['w_gate'], GATE_COLS // 2, "proj_gate")
        r, v, kk, g, bonus, lw, kd, b = _rwkv_prep(p_rwkv, lp)
        yr, sf = _rwkv_scan(r, v, kk, lw, kd, b, _rwkv_state_in(state_rwkv[:, l]))
        srs.append(_rwkv_state_out(sf))
        xs, bc, dt = _ssd_prep(p_ssd, lp)
        ys, hf = _ssd_scan(xs, bc, dt, lp['ssd_a_log_p'], _ssd_state_in(state_ssd[:, l]))
        sss.append(_ssd_state_out(hf))
        ya_ctx = _ctx_attn(p_att, lp['att_sink'])
        ya_lat = _lat_attn(p_att, lp['att_sink'], rope_cos, rope_sin,
                           cache_attn_k[:, l].reshape(N_LAT, PAST_LEN, LANE),
                           cache_attn_v[:, l].reshape(N_LAT, PAST_LEN, LANE))
        ya = jnp.concatenate([ya_ctx, ya_lat], axis=0)
        ks.append(p_att[:T_CTX, BR:BR + LANE].reshape(N_CTX, TB, ATT_KVH, HD))
        vs.append(p_att[:T_CTX, BR + LANE:].reshape(N_CTX, TB, ATT_KVH, HD))
        h = _merge(h, mod, yr, bonus, g, ys, xs, p_ssd, ya, p_gate, lp)
        xn, gate = _route(h, mod, lp['norm2_w'], lp['w_route'])
        h = _experts(xn, gate, lp['w13'], lp['w2'], h, mod)
    y = _final_norm(h, final_norm_w.reshape(1, D_MODEL))
    return (y[:T_CTX].reshape(N_CTX, TB, D_MODEL), y[T_CTX:].reshape(N_LAT, LAT_LEN, D_MODEL),
            jnp.stack(ks, axis=1), jnp.stack(vs, axis=1), jnp.stack(srs, axis=1), jnp.stack(sss, axis=1))
```

```python
import functools
import math

import jax
import jax.numpy as jnp
from jax import lax
from jax.experimental import pallas as pl
from jax.experimental.pallas import tpu as pltpu

F32 = jnp.float32
BF16 = jnp.bfloat16
HIGHEST = lax.Precision.HIGHEST

D_MODEL = 1024
DEPTH = 4
N_CTX = 16
N_LAT = 2
LAT_LEN = 4096
PAST_LEN = 512
GRID_W = 64
NORM_EPS = 1e-6
TB = 256
NLB = LAT_LEN // TB
NBLK = N_CTX + N_LAT * NLB
T_CTX = N_CTX * TB
T_ALL = NBLK * TB
N_SEQ = N_CTX + N_LAT
HALO = 8

HEADS = 8
HD = 64
BR = HEADS * HD
RWKV_COLS = 1920
RWKV_CHUNK = 64
RWKV_GN_EPS = 64e-5
SSD_CONV = 5
SSD_XBC = 768
SSD_PCOLS = 1536
ATT_COLS = 768
ATT_KVH = 2
ATT_GROUP = 4
ATT_WINDOW = 128
ATT_QB = 128
ATT_SCALE = HD ** -0.5
ROPE_BASE = 10000.0
GATE_COLS = 3 * D_MODEL
MOE_GROUPS = 4
MOE_PER_GROUP = 8
MOE_EXPERTS = 32
D_EXPERT = 256
LANE = 128

VMEM_LIMIT = 48 * 1024 * 1024


def _cparams(sem):
    return pltpu.CompilerParams(dimension_semantics=sem, vmem_limit_bytes=VMEM_LIMIT)


def _dot(a, b, precision=None):
    return jnp.dot(a, b, precision=precision, preferred_element_type=F32)


def _dot_nt(a, b, precision=None):
    return lax.dot_general(a, b, (((1,), (1,)), ((), ())), precision=precision, preferred_element_type=F32)


def _dot_tn(a, b, precision=None):
    return lax.dot_general(a, b, (((0,), (0,)), ((), ())), precision=precision, preferred_element_type=F32)


def _sigmoid(x):
    return 1.0 / (1.0 + jnp.exp(-x))


def _silu(x):
    return x * _sigmoid(x)


def _softplus(x):
    return jnp.maximum(x, 0.0) + jnp.log(1.0 + jnp.exp(-jnp.abs(x)))


def _blk_type(i):
    return jnp.where(i < N_CTX, 0, 1 + (i - N_CTX) // NLB)


def _blk_seq(i):
    return jnp.where(i < N_CTX, i, N_CTX + (i - N_CTX) // NLB)


def _blk_first(i):
    return jnp.logical_or(i < N_CTX, (i - N_CTX) % NLB == 0)


def _blk_last(i):
    return jnp.logical_or(i < N_CTX, (i - N_CTX) % NLB == NLB - 1)


def _scan_blk(d, i):
    return i + d * (NBLK - 1 - 2 * i)


def _scan_ends(d, blk):
    fwd = d == 0
    a, b = _blk_first(blk), _blk_last(blk)
    first = jnp.logical_or(jnp.logical_and(fwd, a), jnp.logical_and(jnp.logical_not(fwd), b))
    last = jnp.logical_or(jnp.logical_and(fwd, b), jnp.logical_and(jnp.logical_not(fwd), a))
    return first, last


def _adaln_kernel(c_ref, w_ref, b_ref, o_ref):
    o_ref[0] = _dot(_silu(c_ref[...]), w_ref[0], HIGHEST) + b_ref[0]


def _adaln(cond8, w_ada, b_ada):
    nj = 6
    return pl.pallas_call(
        _adaln_kernel,
        grid=(DEPTH, nj),
        in_specs=[pl.BlockSpec((8, D_MODEL), lambda l, j: (0, 0)),
                  pl.BlockSpec((1, D_MODEL, D_MODEL), lambda l, j: (l, 0, j)),
                  pl.BlockSpec((1, 1, D_MODEL), lambda l, j: (l, 0, j))],
        out_specs=pl.BlockSpec((1, 8, D_MODEL), lambda l, j: (l, 0, j)),
        out_shape=jax.ShapeDtypeStruct((DEPTH, 8, 6 * D_MODEL), F32),
        compiler_params=_cparams(("parallel", "parallel")),
        name="adaln",
    )(cond8, w_ada, b_ada.reshape(DEPTH, 1, 6 * D_MODEL))


def _normmod(x, nw, shift, scale):
    y = x * lax.rsqrt(jnp.mean(x * x, axis=-1, keepdims=True) + NORM_EPS) * nw
    return y * (1.0 + scale) + shift


def _normproj_kernel(h_ref, mod_ref, nw_ref, w_ref, o_ref, xn_ref, *, slot):
    @pl.when(pl.program_id(1) == 0)
    def _():
        m = mod_ref[0]
        sh = m[:, slot * D_MODEL:(slot + 1) * D_MODEL]
        sc = m[:, (slot + 1) * D_MODEL:(slot + 2) * D_MODEL]
        xn_ref[...] = _normmod(h_ref[...], nw_ref[...], sh, sc).astype(BF16)
    o_ref[...] = _dot(xn_ref[...], w_ref[...])


def _normproj(h, mod, nw, w, tn, name):
    tm = 2 * TB
    n = w.shape[1]
    return pl.pallas_call(
        functools.partial(_normproj_kernel, slot=0),
        grid=(T_ALL // tm, n // tn),
        in_specs=[pl.BlockSpec((tm, D_MODEL), lambda i, j: (i, 0)),
                  pl.BlockSpec((1, 1, 6 * D_MODEL), lambda i, j: (_blk_type(2 * i), 0, 0)),
                  pl.BlockSpec((1, D_MODEL), lambda i, j: (0, 0)),
                  pl.BlockSpec((D_MODEL, tn), lambda i, j: (0, j))],
        out_specs=pl.BlockSpec((tm, tn), lambda i, j: (i, j)),
        out_shape=jax.ShapeDtypeStruct((T_ALL, n), F32),
        scratch_shapes=[pltpu.VMEM((tm, D_MODEL), BF16)],
        compiler_params=_cparams(("parallel", "arbitrary")),
        name=name,
    )(h, mod, nw, w)


def _with_halo(prev_ref, cur_ref, next_ref, i):
    prev = jnp.where(_blk_first(i), 0.0, prev_ref[...])
    nxt = jnp.where(_blk_last(i), 0.0, next_ref[...])
    return jnp.concatenate([prev, cur_ref[...], nxt], axis=0)


def _shifted(ext, s):
    n = ext.shape[0]
    return pltpu.roll(ext, (-s) % n, axis=0)[HALO:HALO + TB]


def _seg_sum(x, width):
    ii = lax.broadcasted_iota(jnp.int32, (LANE, LANE), 0)
    jj = lax.broadcasted_iota(jnp.int32, (LANE, LANE), 1)
    shift = int(math.log2(width))
    ones = ((ii >> shift) == (jj >> shift)).astype(F32)
    parts = [_dot(x[:, c:c + LANE], ones, HIGHEST) for c in range(0, x.shape[1], LANE)]
    return jnp.concatenate(parts, axis=1)


def _rwkv_prep_kernel(pp_ref, p_ref, pn_ref, mu_ref, w0_ref, w2_ref, a0_ref, a2_ref, g2_ref, kk_w_ref, ka_ref,
                      rk_ref, r_ref, v_ref, kk_ref, g_ref, bonus_ref, lw_ref, kd_ref, b_ref):
    i = pl.program_id(0)
    ext = _with_halo(pp_ref, p_ref, pn_ref, i)
    p = p_ref[...]
    p = p + mu_ref[...] * (0.5 * (_shifted(ext, -1) + _shifted(ext, 1)) - p)
    r, k, v = p[:, 0:BR], p[:, BR:2 * BR], p[:, 2 * BR:3 * BR]
    wd, ad, gd = p[:, 1536:1664], p[:, 1664:1792], p[:, 1792:1920]
    kk = k * kk_w_ref[...]
    kk = kk / jnp.maximum(jnp.sqrt(_seg_sum(kk * kk, HD)), 1e-12)
    r_ref[...] = r
    v_ref[...] = v
    kk_ref[...] = kk
    g_ref[...] = _dot(_sigmoid(gd), g2_ref[...], HIGHEST)
    tw = jnp.tanh(wd)
    kd_sum = jnp.zeros_like(k)
    for d in range(2):
        wl = w0_ref[d:d + 1, :] + _dot(tw, w2_ref[d], HIGHEST)
        lw_ref[d] = -_sigmoid(wl) * math.exp(-0.5)
        a = _sigmoid(a0_ref[d:d + 1, :] + _dot(ad, a2_ref[d], HIGHEST))
        kd = k * (1.0 + (a - 1.0) * ka_ref[...])
        kd_ref[d] = kd
        b_ref[d] = kk * a
        kd_sum = kd_sum + kd
    bonus_ref[...] = _seg_sum(r * kd_sum * rk_ref[...], HD) * v


def _halo_specs(cols):
    per = TB // HALO
    return [pl.BlockSpec((HALO, cols), lambda i: (jnp.maximum(i * per - 1, 0), 0)),
            pl.BlockSpec((TB, cols), lambda i: (i, 0)),
            pl.BlockSpec((HALO, cols), lambda i: (jnp.minimum((i + 1) * per, T_ALL // HALO - 1), 0))]


def _full(shape):
    return pl.BlockSpec(shape, lambda *_: (0,) * len(shape))


def _rwkv_prep(p_rwkv, lp):
    tok = jax.ShapeDtypeStruct((T_ALL, BR), F32)
    tok2 = jax.ShapeDtypeStruct((2, T_ALL, BR), F32)
    spec1 = pl.BlockSpec((TB, BR), lambda i: (i, 0))
    spec2 = pl.BlockSpec((2, TB, BR), lambda i: (0, i, 0))
    return pl.pallas_call(
        _rwkv_prep_kernel,
        grid=(NBLK,),
        in_specs=_halo_specs(RWKV_COLS) + [
            _full((1, RWKV_COLS)), _full((2, BR)), _full((2, LANE, BR)), _full((2, BR)), _full((2, LANE, BR)),
            _full((LANE, BR)), _full((1, BR)), _full((1, BR)), _full((1, BR))],
        out_specs=[spec1, spec1, spec1, spec1, spec1, spec2, spec2, spec2],
        out_shape=[tok, tok, tok, tok, tok, tok2, tok2, tok2],
        compiler_params=_cparams(("parallel",)),
        name="rwkv_prep",
    )(p_rwkv, p_rwkv, p_rwkv, lp['rwkv_mu'], lp['rwkv_w0'], lp['rwkv_w2p'], lp['rwkv_a0'], lp['rwkv_a2p'],
      lp['rwkv_g2'], lp['rwkv_k_k'], lp['rwkv_k_a'], lp['rwkv_r_k'])


def _pair_bd(x):
    h0 = lax.broadcasted_iota(jnp.int32, x.shape, 1) < HD
    return jnp.concatenate([jnp.where(h0, x, 0.0), jnp.where(h0, 0.0, x)], axis=0)


def _split16(x):
    hi = x.astype(BF16)
    return hi, (x - hi.astype(F32)).astype(BF16)


def _dot16(a, b):
    return _dot(a.astype(BF16), b.astype(BF16))


def _rwkv_scan_kernel(r_ref, v_ref, kk_ref, lw_ref, kd_ref, b_ref, s0_ref, y_ref, sf_ref,
                      s_ref, lrhs_ref, ly_ref, t3_ref, v2_ref, ebt3_ref, wc_ref):
    d = pl.program_id(0)
    blk = _scan_blk(d, pl.program_id(1))
    first, last = _scan_ends(d, blk)
    sgn = 1 - 2 * d
    C = RWKV_CHUNK
    nchunk = TB // C
    npair = BR // LANE

    @pl.when(first)
    def _():
        s_ref[...] = s0_ref[0, 0]

    ci = lax.broadcasted_iota(jnp.int32, (C, C), 0)
    cj = lax.broadcasted_iota(jnp.int32, (C, C), 1)
    tri = ((cj - ci) * sgn <= 0).astype(BF16)
    ii = lax.broadcasted_iota(jnp.int32, (LANE, LANE), 0)
    jj = lax.broadcasted_iota(jnp.int32, (LANE, LANE), 1)
    same = (ii >> 6) == (jj >> 6)
    rel = ((jj & (C - 1)) - (ii & (C - 1))) * sgn
    before = jnp.logical_and(same, rel < 0)
    before_incl = jnp.logical_and(same, rel <= 0)

    def rows_of(c):
        cc = c + d * (nchunk - 1 - 2 * c)
        return pl.ds(pl.multiple_of(cc * C, C), C)

    n_of = {}
    for c in range(nchunk):
        rows = rows_of(c)
        lw = lw_ref[0, rows, :]
        l0 = lw.astype(BF16)
        l1 = (lw - l0.astype(F32)).astype(BF16)
        l2 = (lw - l0.astype(F32) - l1.astype(F32)).astype(BF16)
        cum = _dot(tri, l0) + _dot(tri, l1) + _dot(tri, l2)
        tot = jnp.sum(lw, axis=0, keepdims=True)
        e_neg = jnp.exp(-cum)
        e_end = jnp.exp(tot - cum)
        r = r_ref[rows, :]
        v = v_ref[rows, :]
        kk = kk_ref[rows, :]
        kd = kd_ref[0, rows, :]
        b = b_ref[0, rows, :]
        k_t, b_t = kd * e_neg, b * e_neg
        kk_h, r_h = kk * jnp.exp(cum - lw), r * jnp.exp(cum)
        k_e, b_e = kd * e_end, b * e_end
        for p in range(npair):
            sl = slice(p * LANE, (p + 1) * LANE)
            KK, R, KT, BT, V, KE, BE = (_pair_bd(x[:, sl]) for x in (kk_h, r_h, k_t, b_t, v, k_e, b_e))
            kk16, r16 = KK.astype(BF16), R.astype(BF16)
            g = _dot_nt(jnp.concatenate([kk16, r16], axis=0),
                        jnp.concatenate([BT, KT], axis=0).astype(BF16))
            n_of[c, p] = jnp.where(before, g[:LANE, :LANE], 0.0)
            a1 = jnp.where(before, g[:LANE, LANE:], 0.0)
            a4 = jnp.where(before_incl, g[LANE:, :LANE], 0.0)
            a3 = jnp.where(before_incl, g[LANE:, LANE:], 0.0)
            lrhs_ref[c, p] = jnp.concatenate([kk16, a1.astype(BF16)], axis=1)
            ly_ref[c, p] = jnp.concatenate([r16, a3.astype(BF16), a4.astype(BF16)], axis=1)
            vh, vl = _split16(V)
            v2_ref[c, p] = jnp.concatenate([vh, vl], axis=0)
            eh, el = _split16(jnp.concatenate([KE.T, BE.T], axis=1))
            ebt3_ref[c, p] = jnp.concatenate([eh, el, eh], axis=1)
            wc_ref[c, p] = jnp.exp(jnp.broadcast_to(tot[:, sl], (LANE, LANE)).T)

    chains = list(n_of)
    level = lambda f: {k: f(k) for k in chains}
    eye = (ii == jj).astype(F32)
    blk16 = (ii >> 4) == (jj >> 4)
    dg = level(lambda k: jnp.where(blk16, n_of[k], 0.0))
    low = level(lambda k: n_of[k] - dg[k])
    d2 = level(lambda k: _dot16(dg[k], dg[k]))
    d4 = level(lambda k: _dot16(d2[k], d2[k]))
    x1 = level(lambda k: _dot16(eye - dg[k], eye + d2[k]))
    d8 = level(lambda k: _dot16(d4[k], d4[k]))
    x2 = level(lambda k: _dot16(x1[k], eye + d4[k]))
    xd = level(lambda k: _dot16(x2[k], eye + d8[k]))
    m = level(lambda k: _dot16(xd[k], low[k]))
    m2 = level(lambda k: _dot16(m[k], m[k]))
    t1 = level(lambda k: _dot16(eye - m[k], eye + m2[k]))
    tinv = level(lambda k: _dot16(t1[k], xd[k]))
    for k in chains:
        th, tl = _split16(tinv[k])
        t3_ref[k] = jnp.concatenate([th, tl, th], axis=1)

    pairs = range(npair)
    for c in range(nchunk):
        rows = rows_of(c)
        s = [s_ref[p] for p in pairs]
        s16 = [x.astype(BF16) for x in s]
        vh = [v2_ref[c, p, :LANE] for p in pairs]
        vl = [v2_ref[c, p, LANE:] for p in pairs]
        rhs = [_split16(_dot(lrhs_ref[c, p], jnp.concatenate([s16[p], vh[p]], axis=0))) for p in pairs]
        u = [_split16(-_dot(t3_ref[c, p], jnp.concatenate([rhs[p][0], rhs[p][0], rhs[p][1]], axis=0)))
             for p in pairs]
        for p in pairs:
            uh, ul = u[p]
            s_ref[p] = s[p] * wc_ref[c, p] + _dot(
                ebt3_ref[c, p], jnp.concatenate([vh[p], uh, vh[p], uh, vl[p], ul], axis=0))
        for p in pairs:
            y = _dot(ly_ref[c, p], jnp.concatenate([s16[p], vh[p], u[p][0]], axis=0))
            y_ref[0, rows, p * LANE:(p + 1) * LANE] = y[:C] + y[C:]

    @pl.when(last)
    def _():
        sf_ref[0, 0] = s_ref[...]


def _rwkv_scan(r, v, kk, lw, kd, b, s0):
    np_ = BR // LANE
    nchunk = TB // RWKV_CHUNK
    spec1 = pl.BlockSpec((TB, BR), lambda d, i: (_scan_blk(d, i), 0))
    spec2 = pl.BlockSpec((1, TB, BR), lambda d, i: (d, _scan_blk(d, i), 0))
    sspec = pl.BlockSpec((1, 1, np_, LANE, LANE), lambda d, i: (d, _blk_seq(_scan_blk(d, i)), 0, 0, 0))
    per = lambda rows, cols, dt: pltpu.VMEM((nchunk, np_, rows, cols), dt)
    return pl.pallas_call(
        _rwkv_scan_kernel,
        grid=(2, NBLK),
        in_specs=[spec1, spec1, spec1, spec2, spec2, spec2, sspec],
        out_specs=[spec2, sspec],
        out_shape=[jax.ShapeDtypeStruct((2, T_ALL, BR), F32),
                   jax.ShapeDtypeStruct((2, N_SEQ, np_, LANE, LANE), F32)],
        scratch_shapes=[pltpu.VMEM((np_, LANE, LANE), F32),
                        per(LANE, 2 * LANE, BF16), per(LANE, 3 * LANE, BF16), per(LANE, 3 * LANE, BF16),
                        per(2 * LANE, LANE, BF16), per(LANE, 6 * LANE, BF16), per(LANE, LANE, F32)],
        compiler_params=_cparams(("arbitrary", "arbitrary")),
        name="rwkv_scan",
    )(r, v, kk, lw, kd, b, s0)


def _ssd_prep_kernel(pp_ref, p_ref, pn_ref, cw_ref, cb_ref, dtb_ref, x_ref, bc_ref, dt_ref):
    i = pl.program_id(0)
    ext = _with_halo(pp_ref, p_ref, pn_ref, i)[:, BR:BR + SSD_XBC]
    acc = cb_ref[...] + cw_ref[0:1, :] * _shifted(ext, -(SSD_CONV // 2))
    for j in range(1, SSD_CONV):
        acc = acc + cw_ref[j:j + 1, :] * _shifted(ext, j - SSD_CONV // 2)
    xbc = _silu(acc)
    x_ref[...] = xbc[:, :BR]
    bc_ref[...] = xbc[:, BR:]
    dt_ref[...] = _softplus(p_ref[:, BR + SSD_XBC:] + dtb_ref[...])


def _ssd_prep(p_ssd, lp):
    return pl.pallas_call(
        _ssd_prep_kernel,
        grid=(NBLK,),
        in_specs=_halo_specs(SSD_PCOLS) + [_full((8, SSD_XBC)), _full((1, SSD_XBC)), _full((1, 2 * LANE))],
        out_specs=[pl.BlockSpec((TB, BR), lambda i: (i, 0)),
                   pl.BlockSpec((TB, 2 * LANE), lambda i: (i, 0)),
                   pl.BlockSpec((TB, 2 * LANE), lambda i: (i, 0))],
        out_shape=[jax.ShapeDtypeStruct((T_ALL, BR), F32),
                   jax.ShapeDtypeStruct((T_ALL, 2 * LANE), F32),
                   jax.ShapeDtypeStruct((T_ALL, 2 * LANE), F32)],
        compiler_params=_cparams(("parallel",)),
        name="ssd_prep",
    )(p_ssd, p_ssd, p_ssd, lp['ssd_conv_w8'], lp['ssd_conv_b'], lp['ssd_dt_bias_p'])


def _ssd_scan_kernel(x_ref, bc_ref, dt_ref, alog_ref, h0_ref, y_ref, hf_ref, hs_ref):
    d = pl.program_id(0)
    blk = _scan_blk(d, pl.program_id(1))
    first, last = _scan_ends(d, blk)
    sgn = 1 - 2 * d

    @pl.when(first)
    def _():
        hs_ref[...] = h0_ref[0, 0]

    x = x_ref[...]
    bm = bc_ref[:, :LANE]
    cm = bc_ref[:, LANE:]
    dt = dt_ref[...]
    a_neg = -jnp.exp(alog_ref[0])
    dta = dt * a_neg
    qi = lax.broadcasted_iota(jnp.int32, (TB, TB), 0)
    qj = lax.broadcasted_iota(jnp.int32, (TB, TB), 1)
    before_incl = (qj - qi) * sgn <= 0
    a_cum = _dot(before_incl.astype(F32), dta, HIGHEST)
    tot = jnp.sum(dta, axis=0, keepdims=True)
    a_cum_t = a_cum.T
    dt_t = dt.T
    eh = lax.broadcasted_iota(jnp.int32, (LANE, BR), 0)
    ec = lax.broadcasted_iota(jnp.int32, (LANE, BR), 1)
    expand = ((ec >> 6) == eh).astype(F32)
    e_in = _dot(jnp.exp(a_cum), expand, HIGHEST)
    to_end = _dot(jnp.exp(tot - a_cum) * dt, expand, HIGHEST)
    dec = _dot(jnp.broadcast_to(jnp.exp(tot), (8, LANE)), expand, HIGHEST)[0:1]
    hs = hs_ref[...]
    cb16, bb16, xb16 = cm.astype(BF16), bm.astype(BF16), x.astype(BF16)
    y_off = _dot(cb16, hs.astype(BF16)) * e_in
    glane = lax.broadcasted_iota(jnp.int32, (TB, LANE), 1) >> 6
    half = lax.broadcasted_iota(jnp.int32, (TB, LANE), 1) < HD
    y_parts = []
    for g in range(2):
        cbg = _dot_nt(jnp.where(glane == g, cm, 0.0).astype(BF16), bb16)
        for pr in range(2):
            ys = []
            for hh in range(2):
                h = 4 * g + 2 * pr + hh
                seg = a_cum[:, h:h + 1] - a_cum_t[h:h + 1, :]
                decay = jnp.exp(jnp.where(before_incl, seg, -jnp.inf))
                sc = cbg * decay * dt_t[h:h + 1, :]
                p0 = (2 * g + pr) * LANE
                ys.append(_dot(sc.astype(BF16), xb16[:, p0:p0 + LANE]))
            y_parts.append(jnp.where(half, ys[0], ys[1]))
    y_ref[0] = jnp.concatenate(y_parts, axis=1) + y_off
    upd = _dot_tn(bb16, (x * to_end).astype(BF16))
    ui = lax.broadcasted_iota(jnp.int32, (LANE, BR), 0)
    uj = lax.broadcasted_iota(jnp.int32, (LANE, BR), 1)
    hs_ref[...] = hs * dec + jnp.where((ui >> 6) == (uj >> 8), upd, 0.0)

    @pl.when(last)
    def _():
        hf_ref[0, 0] = hs_ref[...]


def _ssd_scan(x, bc, dt, alog, h0):
    sspec = pl.BlockSpec((1, 1, LANE, BR), lambda d, i: (d, _blk_seq(_scan_blk(d, i)), 0, 0))
    return pl.pallas_call(
        _ssd_scan_kernel,
        grid=(2, NBLK),
        in_specs=[pl.BlockSpec((TB, BR), lambda d, i: (_scan_blk(d, i), 0)),
                  pl.BlockSpec((TB, 2 * LANE), lambda d, i: (_scan_blk(d, i), 0)),
                  pl.BlockSpec((TB, LANE), lambda d, i: (_scan_blk(d, i), d)),
                  pl.BlockSpec((1, 1, LANE), lambda d, i: (d, 0, 0)),
                  sspec],
        out_specs=[pl.BlockSpec((1, TB, BR), lambda d, i: (d, _scan_blk(d, i), 0)), sspec],
        out_shape=[jax.ShapeDtypeStruct((2, T_ALL, BR), F32),
                   jax.ShapeDtypeStruct((2, N_SEQ, LANE, BR), F32)],
        scratch_shapes=[pltpu.VMEM((LANE, BR), F32)],
        compiler_params=_cparams(("arbitrary", "arbitrary")),
        name="ssd_scan",
    )(x, bc, dt, alog, h0)


def _sink_softmax(s, sink_col):
    m = jnp.maximum(jnp.max(s, axis=-1, keepdims=True), sink_col)
    e = jnp.exp(s - m)
    return e / (jnp.sum(e, axis=-1, keepdims=True) + jnp.exp(sink_col - m))


def _gqa(q, k_all, v_all, sink_ref, valid):
    nq = q.shape[0]
    outs = []
    for kvh in range(ATT_KVH):
        kh = k_all[:, kvh * HD:(kvh + 1) * HD].astype(BF16)
        vh = v_all[:, kvh * HD:(kvh + 1) * HD].astype(BF16)
        heads = range(kvh * ATT_GROUP, (kvh + 1) * ATT_GROUP)
        qg = jnp.concatenate([q[:, h * HD:(h + 1) * HD] for h in heads], axis=0).astype(BF16)
        s = _dot_nt(qg, kh) * ATT_SCALE
        if valid is not None:
            s = jnp.where(valid, s, -jnp.inf)
        sink_col = jnp.concatenate([jnp.full((nq, 1), sink_ref[h], F32) for h in heads], axis=0)
        o = _dot(_sink_softmax(s, sink_col).astype(BF16), vh)
        outs += [o[g * nq:(g + 1) * nq] for g in range(ATT_GROUP)]
    return jnp.concatenate(outs, axis=1)


def _ctx_attn_kernel(sink_ref, p_ref, o_ref):
    p = p_ref[...]
    o_ref[...] = _gqa(p[:, :BR], p[:, BR:BR + LANE], p[:, BR + LANE:], sink_ref, None)


def _ctx_attn(p_att, sink):
    return pl.pallas_call(
        _ctx_attn_kernel,
        grid=(N_CTX,),
        in_specs=[pl.BlockSpec(memory_space=pltpu.SMEM), pl.BlockSpec((TB, ATT_COLS), lambda i: (i, 0))],
        out_specs=pl.BlockSpec((TB, BR), lambda i: (i, 0)),
        out_shape=jax.ShapeDtypeStruct((T_CTX, BR), F32),
        compiler_params=_cparams(("parallel",)),
        name="ctx_attn",
    )(sink, p_att)


def _rope(x, cos, sin_signed):
    lanes = x.shape[1]
    reps = lanes // LANE
    if reps > 1:
        cos = jnp.concatenate([cos] * reps, axis=1)
        sin_signed = jnp.concatenate([sin_signed] * reps, axis=1)
    lo = (lax.broadcasted_iota(jnp.int32, x.shape, 1) & 31) < 16
    partner = jnp.where(lo, pltpu.roll(x, lanes - 16, axis=1), pltpu.roll(x, 16, axis=1))
    return x * cos + partner * sin_signed


def _lat_attn_kernel(sink_ref, pq_ref, pp_ref, pn_ref, cq_ref, sq_ref, cp_ref, sp_ref, cn_ref, sn_ref,
                     kc_ref, vc_ref, o_ref):
    j = pl.program_id(1)
    nb = LAT_LEN // ATT_QB
    q = _rope(pq_ref[:, :BR], cq_ref[...], sq_ref[...])
    k_loc = jnp.concatenate([_rope(pp_ref[:, BR:BR + LANE], cp_ref[...], sp_ref[...]),
                             _rope(pq_ref[:, BR:BR + LANE], cq_ref[...], sq_ref[...]),
                             _rope(pn_ref[:, BR:BR + LANE], cn_ref[...], sn_ref[...])], axis=0)
    v_loc = jnp.concatenate([pp_ref[:, BR + LANE:], pq_ref[:, BR + LANE:], pn_ref[:, BR + LANE:]], axis=0)
    k_all = jnp.concatenate([k_loc, kc_ref[0]], axis=0)
    v_all = jnp.concatenate([v_loc, vc_ref[0]], axis=0)
    nk = 3 * ATT_QB + PAST_LEN
    qi = lax.broadcasted_iota(jnp.int32, (ATT_GROUP * ATT_QB, nk), 0) & (ATT_QB - 1)
    kj = lax.broadcasted_iota(jnp.int32, (ATT_GROUP * ATT_QB, nk), 1)
    rel = kj - ATT_QB - qi
    kpos = (j - 1) * ATT_QB + kj
    valid = (((rel <= ATT_WINDOW) & (rel >= -ATT_WINDOW) & (kpos >= 0) & (kpos < nb * ATT_QB))
             | (kj >= 3 * ATT_QB))
    o_ref[...] = _gqa(q, k_all, v_all, sink_ref, valid)


def _lat_attn(p_att, sink, rope_cos, rope_sin, k_ctx, v_ctx):
    nb = LAT_LEN // ATT_QB
    base = T_CTX // ATT_QB
    row = lambda b, j: base + b * nb + j
    pspec = lambda f: pl.BlockSpec((ATT_QB, ATT_COLS), lambda b, j: (row(b, f(j)), 0))
    tspec = lambda f: pl.BlockSpec((ATT_QB, LANE), lambda b, j: (f(j), 0))
    cur = lambda j: j
    prv = lambda j: jnp.maximum(j - 1, 0)
    nxt = lambda j: jnp.minimum(j + 1, nb - 1)
    cspec = pl.BlockSpec((1, PAST_LEN, LANE), lambda b, j: (b, 0, 0))
    return pl.pallas_call(
        _lat_attn_kernel,
        grid=(N_LAT, nb),
        in_specs=[pl.BlockSpec(memory_space=pltpu.SMEM), pspec(cur), pspec(prv), pspec(nxt),
                  tspec(cur), tspec(cur), tspec(prv), tspec(prv), tspec(nxt), tspec(nxt), cspec, cspec],
        out_specs=pl.BlockSpec((ATT_QB, BR), lambda b, j: (b * nb + j, 0)),
        out_shape=jax.ShapeDtypeStruct((N_LAT * LAT_LEN, BR), F32),
        compiler_params=_cparams(("parallel", "parallel")),
        name="lat_attn",
    )(sink, p_att, p_att, p_att, rope_cos, rope_sin, rope_cos, rope_sin, rope_cos, rope_sin, k_ctx, v_ctx)


def _merge_kernel(h_ref, mod_ref, yr_ref, bonus_ref, g_ref, lnw_ref, lnb_ref, ys_ref, xs_ref, z_ref, dvec_ref,
                  snw_ref, ya_ref, pg_ref, wb_ref, wo_ref, o_ref):
    y = yr_ref[0] + yr_ref[1] + bonus_ref[...]
    mu = _seg_sum(y, HD) * (1.0 / HD)
    yc = y - mu
    var = _seg_sum(yc * yc, HD) * (1.0 / HD)
    y_a = (yc * lax.rsqrt(var + RWKV_GN_EPS) * lnw_ref[...] + lnb_ref[...]) * g_ref[...]
    y = (ys_ref[0] + ys_ref[1] + dvec_ref[...] * xs_ref[...]) * _silu(z_ref[...])
    y_b = y * lax.rsqrt(jnp.mean(y * y, axis=-1, keepdims=True) + NORM_EPS) * snw_ref[...]
    merged = None
    for n, br in enumerate((y_a, y_b, ya_ref[...])):
        wide = _dot(br.astype(BF16), wb_ref[n])
        term = _sigmoid(pg_ref[:, n * D_MODEL:(n + 1) * D_MODEL]) * wide
        merged = term if merged is None else merged + term
    g1 = mod_ref[0][:, 2 * D_MODEL:3 * D_MODEL]
    o_ref[...] = h_ref[...] + g1 * _dot(merged.astype(BF16), wo_ref[...])


def _merge(h, mod, yr, bonus, g, ys, xs, p_ssd, ya, p_gate, lp):
    tm = TB
    row = lambda w: pl.BlockSpec((tm, w), lambda i: (i, 0))
    row2 = pl.BlockSpec((2, tm, BR), lambda i: (0, i, 0))
    return pl.pallas_call(
        _merge_kernel,
        grid=(T_ALL // tm,),
        in_specs=[row(D_MODEL), pl.BlockSpec((1, 1, 6 * D_MODEL), lambda i: (_blk_type(i), 0, 0)),
                  row2, row(BR), row(BR), _full((1, BR)), _full((1, BR)),
                  row2, row(BR), row(BR), _full((1, BR)), _full((1, BR)),
                  row(BR), row(GATE_COLS), _full((3, BR, D_MODEL)), _full((D_MODEL, D_MODEL))],
        out_specs=row(D_MODEL),
        out_shape=jax.ShapeDtypeStruct((T_ALL, D_MODEL), F32),
        compiler_params=_cparams(("parallel",)),
        name="merge",
    )(h, mod, yr, bonus, g, lp['rwkv_lnx_w'], lp['rwkv_lnx_b'], ys, xs, p_ssd, lp['ssd_d_vec'], lp['ssd_norm_w'],
      ya, p_gate, lp['w_branch'], lp['w_out'])


def _route_kernel(h_ref, mod_ref, nw_ref, wr_ref, xn_ref, gate_ref):
    m = mod_ref[0]
    xn = _normmod(h_ref[...], nw_ref[...], m[:, 3 * D_MODEL:4 * D_MODEL], m[:, 4 * D_MODEL:5 * D_MODEL])
    xn_ref[...] = xn.astype(BF16)
    logits = _dot(xn, wr_ref[...], HIGHEST)
    lane_i = lax.broadcasted_iota(jnp.int32, logits.shape, 1)
    lane = lane_i.astype(F32)
    lane_grp = (lane_i >> 3).astype(F32)
    neg = -jnp.inf
    big = float(LANE)
    is_g = (lane_i >= MOE_EXPERTS) & (lane_i < MOE_EXPERTS + MOE_GROUPS)
    gl = jnp.where(is_g, logits, neg)
    gmax = jnp.max(gl, axis=-1, keepdims=True)
    gsel = jnp.min(jnp.where(gl == gmax, lane - MOE_EXPERTS, big), axis=-1, keepdims=True)
    g_w = 1.0 / jnp.sum(jnp.where(is_g, jnp.exp(gl - gmax), 0.0), axis=-1, keepdims=True)
    el = jnp.where((lane_i < MOE_EXPERTS) & (lane_grp == gsel), logits, neg)
    m1 = jnp.max(el, axis=-1, keepdims=True)
    i1 = jnp.min(jnp.where(el == m1, lane, big), axis=-1, keepdims=True)
    el2 = jnp.where(lane == i1, neg, el)
    m2 = jnp.max(el2, axis=-1, keepdims=True)
    i2 = jnp.min(jnp.where(el2 == m2, lane, big), axis=-1, keepdims=True)
    e2 = jnp.exp(m2 - m1)
    w1 = 1.0 / (1.0 + e2)
    gate_ref[...] = jnp.where(lane == i1, w1 * g_w, jnp.where(lane == i2, e2 * w1 * g_w, 0.0))


def _route(h, mod, nw, wr):
    tm = TB
    return pl.pallas_call(
        _route_kernel,
        grid=(T_ALL // tm,),
        in_specs=[pl.BlockSpec((tm, D_MODEL), lambda i: (i, 0)),
                  pl.BlockSpec((1, 1, 6 * D_MODEL), lambda i: (_blk_type(i), 0, 0)),
                  _full((1, D_MODEL)), _full((D_MODEL, LANE))],
        out_specs=[pl.BlockSpec((tm, D_MODEL), lambda i: (i, 0)), pl.BlockSpec((tm, LANE), lambda i: (i, 0))],
        out_shape=[jax.ShapeDtypeStruct((T_ALL, D_MODEL), BF16), jax.ShapeDtypeStruct((T_ALL, LANE), F32)],
        compiler_params=_cparams(("parallel",)),
        name="moe_route",
    )(h, mod, nw, wr)


def _experts_kernel(xn_ref, gate_ref, w13_ref, w2_ref, h_ref, mod_ref, o_ref, acc_ref):
    e = pl.program_id(1)

    @pl.when(e == 0)
    def _():
        acc_ref[...] = jnp.zeros_like(acc_ref)

    gate = gate_ref[...]
    lane = lax.broadcasted_iota(jnp.int32, gate.shape, 1)
    gcol = jnp.sum(jnp.where(lane == e, gate, 0.0), axis=-1, keepdims=True)
    h13 = _dot(xn_ref[...], w13_ref[0])
    act = _silu(h13[:, :D_EXPERT]) * h13[:, D_EXPERT:] * gcol
    acc_ref[...] += _dot(act.astype(BF16), w2_ref[0])

    @pl.when(e == MOE_EXPERTS - 1)
    def _():
        g2 = mod_ref[0][:, 5 * D_MODEL:6 * D_MODEL]
        o_ref[...] = h_ref[...] + g2 * acc_ref[...]


def _experts(xn, gate, w13, w2, h, mod):
    tm = 4 * TB
    return pl.pallas_call(
        _experts_kernel,
        grid=(T_ALL // tm, MOE_EXPERTS),
        in_specs=[pl.BlockSpec((tm, D_MODEL), lambda i, e: (i, 0)),
                  pl.BlockSpec((tm, LANE), lambda i, e: (i, 0)),
                  pl.BlockSpec((1, D_MODEL, 2 * D_EXPERT), lambda i, e: (e, 0, 0)),
                  pl.BlockSpec((1, D_EXPERT, D_MODEL), lambda i, e: (e, 0, 0)),
                  pl.BlockSpec((tm, D_MODEL), lambda i, e: (i, 0)),
                  pl.BlockSpec((1, 1, 6 * D_MODEL), lambda i, e: (_blk_type(4 * i), 0, 0))],
        out_specs=pl.BlockSpec((tm, D_MODEL), lambda i, e: (i, 0)),
        out_shape=jax.ShapeDtypeStruct((T_ALL, D_MODEL), F32),
        scratch_shapes=[pltpu.VMEM((tm, D_MODEL), F32)],
        compiler_params=_cparams(("parallel", "arbitrary")),
        name="moe_experts",
    )(xn, gate, w13, w2, h, mod)


def _final_norm_kernel(h_ref, w_ref, o_ref):
    x = h_ref[...]
    o_ref[...] = x * lax.rsqrt(jnp.mean(x * x, axis=-1, keepdims=True) + NORM_EPS) * w_ref[...]


def _final_norm(h, w):
    tm = 2 * TB
    return pl.pallas_call(
        _final_norm_kernel,
        grid=(T_ALL // tm,),
        in_specs=[pl.BlockSpec((tm, D_MODEL), lambda i: (i, 0)), _full((1, D_MODEL))],
        out_specs=pl.BlockSpec((tm, D_MODEL), lambda i: (i, 0)),
        out_shape=jax.ShapeDtypeStruct((T_ALL, D_MODEL), F32),
        compiler_params=_cparams(("parallel",)),
        name="final_norm",
    )(h, w)


def _rope_tables():
    pos = jnp.arange(LAT_LEN)
    row = (pos // GRID_W).astype(F32)
    col = (pos % GRID_W).astype(F32)
    half = HD // 2
    inv = 1.0 / (ROPE_BASE ** (jnp.arange(0, half, 2, dtype=F32) / half))
    ar, ac = row[:, None] * inv[None, :], col[:, None] * inv[None, :]
    cos = jnp.concatenate([jnp.cos(ar), jnp.cos(ar), jnp.cos(ac), jnp.cos(ac)], axis=1)
    sin = jnp.concatenate([-jnp.sin(ar), jnp.sin(ar), -jnp.sin(ac), jnp.sin(ac)], axis=1)
    return jnp.tile(cos, (1, 2)), jnp.tile(sin, (1, 2))


def _rwkv_state_in(state):
    s = jnp.swapaxes(state, -1, -2).reshape(N_LAT, 2, HEADS // 2, 2, HD, HD)
    eye = jnp.eye(2, dtype=F32)
    bd = jnp.einsum('bdpjkv,jm->bdpjkmv', s, eye).reshape(N_LAT, 2, HEADS // 2, LANE, LANE)
    bd = jnp.moveaxis(bd, 1, 0)
    zeros = jnp.zeros((2, N_CTX, HEADS // 2, LANE, LANE), F32)
    return jnp.concatenate([zeros, bd], axis=1)


def _rwkv_state_out(sf):
    s = sf[:, :N_CTX].reshape(2, N_CTX, HEADS // 2, 2, HD, 2, HD)
    diag = jnp.stack([s[:, :, :, 0, :, 0, :], s[:, :, :, 1, :, 1, :]], axis=3)
    out = jnp.swapaxes(diag.reshape(2, N_CTX, HEADS, HD, HD), -1, -2)
    return jnp.moveaxis(out, 0, 1)


def _ssd_state_in(state):
    s = jnp.transpose(state, (1, 0, 4, 2, 3))
    grp = (jnp.arange(HEADS) // 4)[None, :] == jnp.arange(2)[:, None]
    full = jnp.where(grp[None, None, :, None, :, None], s[:, :, None], 0.0)
    full = full.reshape(2, N_LAT, LANE, BR)
    return jnp.concatenate([jnp.zeros((2, N_CTX, LANE, BR), F32), full], axis=1)


def _ssd_state_out(hf):
    s = hf[:, :N_CTX].reshape(2, N_CTX, 2, HD, HEADS, HD)
    per_head = jnp.stack([s[:, :, h // 4, :, h, :] for h in range(HEADS)], axis=2)
    return jnp.transpose(per_head, (1, 0, 2, 4, 3))


def _layer_params(l, w_in, prm):
    lp = {}
    wi = w_in[l]
    o_ssd, o_att, o_gate = RWKV_COLS, RWKV_COLS + 1296, RWKV_COLS + 1296 + ATT_COLS
    lp['w_rwkv'] = wi[:, :o_ssd].astype(BF16)
    dt_cols = wi[:, o_ssd + 1280:o_ssd + 1296]
    pad = jnp.zeros((D_MODEL, LANE - HEADS), F32)
    lp['w_ssd'] = jnp.concatenate([wi[:, o_ssd:o_ssd + 1280], dt_cols[:, :HEADS], pad, dt_cols[:, HEADS:], pad],
                                  axis=1).astype(BF16)
    lp['w_att'] = wi[:, o_att:o_gate].astype(BF16)
    lp['w_gate'] = wi[:, o_gate:].astype(BF16)
    row = lambda name: prm[name][l].reshape(1, -1)
    for name in ('norm1_w', 'norm2_w', 'rwkv_mu', 'rwkv_k_k', 'rwkv_k_a', 'rwkv_r_k', 'rwkv_lnx_w', 'rwkv_lnx_b',
                 'ssd_conv_b', 'ssd_norm_w'):
        lp[name] = row(name)
    lp['rwkv_w0'] = prm['rwkv_w0'][l]
    lp['rwkv_a0'] = prm['rwkv_a0'][l]
    z64 = jnp.zeros((HD, BR), F32)
    w2, a2 = prm['rwkv_w2'][l], prm['rwkv_a2'][l]
    lp['rwkv_w2p'] = jnp.stack([jnp.concatenate([w2[0], z64]), jnp.concatenate([z64, w2[1]])])
    lp['rwkv_a2p'] = jnp.stack([jnp.concatenate([a2[0], z64]), jnp.concatenate([z64, a2[1]])])
    lp['rwkv_g2'] = prm['rwkv_g2'][l]
    lp['ssd_conv_w8'] = jnp.concatenate([prm['ssd_conv_w'][l], jnp.zeros((8 - SSD_CONV, SSD_XBC), F32)])
    dtb = prm['ssd_dt_bias'][l]
    zp = jnp.zeros((LANE - HEADS,), F32)
    lp['ssd_dt_bias_p'] = jnp.concatenate([dtb[0], zp, dtb[1], zp]).reshape(1, 2 * LANE)
    lp['ssd_a_log_p'] = jnp.pad(prm['ssd_a_log'][l], ((0, 0), (0, LANE - HEADS))).reshape(2, 1, LANE)
    lp['ssd_d_vec'] = jnp.repeat(prm['ssd_d'][l], HD).reshape(1, BR)
    lp['att_sink'] = prm['att_sink'][l]
    lp['w_branch'] = prm['w_branch'][l].astype(BF16)
    lp['w_out'] = prm['w_out'][l].astype(BF16)
    wr = jnp.concatenate([prm['moe_w_expert'][l].reshape(D_MODEL, MOE_EXPERTS), prm['moe_w_group'][l]], axis=1)
    lp['w_route'] = jnp.pad(wr, ((0, 0), (0, LANE - MOE_EXPERTS - MOE_GROUPS)))
    lp['w13'] = jnp.concatenate([prm['moe_w1'][l], prm['moe_w3'][l]], axis=-1).astype(BF16)
    lp['w2'] = prm['moe_w2'][l].astype(BF16)
    return lp


def kernel(x_prompt, x_sample, cache_attn_k, cache_attn_v, state_rwkv, state_ssd, c, c_ctx, w_ada, b_ada, norm1_w, norm2_w, w_in, rwkv_mu, rwkv_w0, rwkv_w2, rwkv_a0, rwkv_a2, rwkv_g2, rwkv_k_k, rwkv_k_a, rwkv_r_k, rwkv_lnx_w, rwkv_lnx_b, ssd_conv_w, ssd_conv_b, ssd_dt_bias, ssd_a_log, ssd_d, ssd_norm_w, att_sink, w_branch, w_out, moe_w_group, moe_w_expert, moe_w1, moe_w3, moe_w2, final_norm_w):
    prm = dict(norm1_w=norm1_w, norm2_w=norm2_w, rwkv_mu=rwkv_mu, rwkv_w0=rwkv_w0, rwkv_w2=rwkv_w2,
               rwkv_a0=rwkv_a0, rwkv_a2=rwkv_a2, rwkv_g2=rwkv_g2, rwkv_k_k=rwkv_k_k, rwkv_k_a=rwkv_k_a,
               rwkv_r_k=rwkv_r_k.reshape(DEPTH, BR), rwkv_lnx_w=rwkv_lnx_w, rwkv_lnx_b=rwkv_lnx_b,
               ssd_conv_w=ssd_conv_w, ssd_conv_b=ssd_conv_b, ssd_dt_bias=ssd_dt_bias, ssd_a_log=ssd_a_log,
               ssd_d=ssd_d, ssd_norm_w=ssd_norm_w, att_sink=att_sink, w_branch=w_branch, w_out=w_out,
               moe_w_group=moe_w_group, moe_w_expert=moe_w_expert, moe_w1=moe_w1, moe_w3=moe_w3, moe_w2=moe_w2)
    cond8 = jnp.concatenate([c_ctx[None, :], c, jnp.zeros((8 - 1 - N_LAT, D_MODEL), F32)], axis=0)
    mod_all = _adaln(cond8, w_ada, b_ada)
    rope_cos, rope_sin = _rope_tables()
    h = jnp.concatenate([x_prompt.reshape(T_CTX, D_MODEL), x_sample.reshape(N_LAT * LAT_LEN, D_MODEL)], axis=0)
    ks, vs, srs, sss = [], [], [], []
    for l in range(DEPTH):
        lp = _layer_params(l, w_in, prm)
        mod = mod_all[l].reshape(8, 1, 6 * D_MODEL)
        p_rwkv = _normproj(h, mod, lp['norm1_w'], lp['w_rwkv'], RWKV_COLS, "proj_rwkv")
        p_ssd = _normproj(h, mod, lp['norm1_w'], lp['w_ssd'], SSD_PCOLS, "proj_ssd")
        p_att = _normproj(h, mod, lp['norm1_w'], lp['w_att'], ATT_COLS, "proj_att")
        p_gate = _normproj(h, mod, lp['norm1_w'], lp['w_gate'], GATE_COLS // 2, "proj_gate")
        r, v, kk, g, bonus, lw, kd, b = _rwkv_prep(p_rwkv, lp)
        yr, sf = _rwkv_scan(r, v, kk, lw, kd, b, _rwkv_state_in(state_rwkv[:, l]))
        srs.append(_rwkv_state_out(sf))
        xs, bc, dt = _ssd_prep(p_ssd, lp)
        ys, hf = _ssd_scan(xs, bc, dt, lp['ssd_a_log_p'], _ssd_state_in(state_ssd[:, l]))
        sss.append(_ssd_state_out(hf))
        ya_ctx = _ctx_attn(p_att, lp['att_sink'])
        ya_lat = _lat_attn(p_att, lp['att_sink'], rope_cos, rope_sin,
                           cache_attn_k[:, l].reshape(N_LAT, PAST_LEN, LANE),
                           cache_attn_v[:, l].reshape(N_LAT, PAST_LEN, LANE))
        ya = jnp.concatenate([ya_ctx, ya_lat], axis=0)
        ks.append(p_att[:T_CTX, BR:BR + LANE].reshape(N_CTX, TB, ATT_KVH, HD))
        vs.append(p_att[:T_CTX, BR + LANE:].reshape(N_CTX, TB, ATT_KVH, HD))
        h = _merge(h, mod, yr, bonus, g, ys, xs, p_ssd, ya, p_gate, lp)
        xn, gate = _route(h, mod, lp['norm2_w'], lp['w_route'])
        h = _experts(xn, gate, lp['w13'], lp['w2'], h, mod)
    y = _final_norm(h, final_norm_w.reshape(1, D_MODEL))
    return (y[:T_CTX].reshape(N_CTX, TB, D_MODEL), y[T_CTX:].reshape(N_LAT, LAT_LEN, D_MODEL),
            jnp.stack(ks, axis=1), jnp.stack(vs, axis=1), jnp.stack(srs, axis=1), jnp.stack(sss, axis=1))
```

```python
import math

import jax
import jax.numpy as jnp
from jax import lax
from jax.experimental import pallas as pl
from jax.experimental.pallas import tpu as pltpu

F32 = jnp.float32
BF16 = jnp.bfloat16
HIGHEST = lax.Precision.HIGHEST

D_MODEL = 1024
DEPTH = 4
N_CTX = 16
N_LAT = 2
LAT_LEN = 4096
PAST_LEN = 512
GRID_W = 64
NORM_EPS = 1e-6
TB = 256
NLB = LAT_LEN // TB
NBLK = N_CTX + N_LAT * NLB
T_CTX = N_CTX * TB
T_ALL = NBLK * TB
N_SEQ = N_CTX + N_LAT
HALO = 8

HEADS = 8
HD = 64
BR = HEADS * HD
RWKV_COLS = 1920
RWKV_CHUNK = 64
RWKV_GN_EPS = 64e-5
SSD_CONV = 5
SSD_XBC = 768
SSD_PCOLS = 1536
ATT_KVH = 2
ATT_GROUP = 4
ATT_PCOLS = BR + 4 * 2 * HD
ATT_WINDOW = 128
ATT_SCALE = HD ** -0.5
ROPE_BASE = 10000.0
GATE_COLS = 3 * D_MODEL
MOE_GROUPS = 4
MOE_EXPERTS = 32
D_EXPERT = 256
LANE = 128

VMEM_LIMIT = 48 * 1024 * 1024


def _cparams(sem):
    return pltpu.CompilerParams(dimension_semantics=sem, vmem_limit_bytes=VMEM_LIMIT)


def _dot(a, b, precision=None):
    return jnp.dot(a, b, precision=precision, preferred_element_type=F32)


def _dot_nt(a, b):
    return lax.dot_general(a, b, (((1,), (1,)), ((), ())), preferred_element_type=F32)


def _dot_tn(a, b):
    return lax.dot_general(a, b, (((0,), (0,)), ((), ())), preferred_element_type=F32)


def _split16(x):
    hi = x.astype(BF16)
    return hi, (x - hi.astype(F32)).astype(BF16)


def _dot16(a, b):
    return _dot(a.astype(BF16), b.astype(BF16))


def _dot_split(a, b):
    ah, al = _split16(a)
    bh, bl = _split16(b)
    return _dot(jnp.concatenate([ah, al, ah], axis=1), jnp.concatenate([bh, bh, bl], axis=0))


def _dot_exact_rhs(a, b16):
    ah, al = _split16(a)
    return _dot(jnp.concatenate([ah, al], axis=1), jnp.concatenate([b16, b16], axis=0))


def _cumsum_rows(tri16, x):
    x0 = x.astype(BF16)
    r1 = x - x0.astype(F32)
    x1 = r1.astype(BF16)
    x2 = (r1 - x1.astype(F32)).astype(BF16)
    return _dot(tri16, x0) + _dot(tri16, x1) + _dot(tri16, x2)


def _sigmoid(x):
    return 1.0 / (1.0 + jnp.exp(-x))


def _silu(x):
    return x * _sigmoid(x)


def _softplus(x):
    return jnp.maximum(x, 0.0) + jnp.log(1.0 + jnp.exp(-jnp.abs(x)))


def _full(shape):
    return pl.BlockSpec(shape, lambda *_: (0,) * len(shape))


def _blk_type(i):
    return jnp.where(i < N_CTX, 0, 1 + (i - N_CTX) // NLB)


def _blk_seq(i):
    return jnp.where(i < N_CTX, i, N_CTX + (i - N_CTX) // NLB)


def _blk_first(i):
    return jnp.logical_or(i < N_CTX, (i - N_CTX) % NLB == 0)


def _blk_last(i):
    return jnp.logical_or(i < N_CTX, (i - N_CTX) % NLB == NLB - 1)


def _scan_blk(d, i):
    return i + d * (NBLK - 1 - 2 * i)


def _scan_ends(d, blk):
    fwd = d == 0
    a, b = _blk_first(blk), _blk_last(blk)
    first = jnp.logical_or(jnp.logical_and(fwd, a), jnp.logical_and(jnp.logical_not(fwd), b))
    last = jnp.logical_or(jnp.logical_and(fwd, b), jnp.logical_and(jnp.logical_not(fwd), a))
    return first, last


def _lat_of_seq(seq):
    return jnp.clip(seq - N_CTX, 0, N_LAT - 1)


def _adaln_kernel(c_ref, w_ref, b_ref, o_ref):
    o_ref[0] = _dot(_silu(c_ref[...]), w_ref[0], HIGHEST) + b_ref[0]


def _adaln(cond8, w_ada, b_ada):
    return pl.pallas_call(
        _adaln_kernel,
        grid=(DEPTH, 6),
        in_specs=[pl.BlockSpec((8, D_MODEL), lambda l, j: (0, 0)),
                  pl.BlockSpec((1, D_MODEL, D_MODEL), lambda l, j: (l, 0, j)),
                  pl.BlockSpec((1, 1, D_MODEL), lambda l, j: (l, 0, j))],
        out_specs=pl.BlockSpec((1, 8, D_MODEL), lambda l, j: (l, 0, j)),
        out_shape=jax.ShapeDtypeStruct((DEPTH, 8, 6 * D_MODEL), F32),
        compiler_params=_cparams(("parallel", "parallel")),
        name="adaln",
    )(cond8, w_ada, b_ada.reshape(DEPTH, 1, 6 * D_MODEL))


def _normmod(x, nw, shift, scale):
    y = x * lax.rsqrt(jnp.mean(x * x, axis=-1, keepdims=True) + NORM_EPS) * nw
    return y * (1.0 + scale) + shift


def _normproj_kernel(h_ref, mod_ref, nw_ref, w_ref, o_ref, xn_ref):
    @pl.when(pl.program_id(1) == 0)
    def _():
        m = mod_ref[0]
        xn_ref[...] = _normmod(h_ref[...], nw_ref[...], m[:, :D_MODEL], m[:, D_MODEL:2 * D_MODEL]).astype(BF16)
    o_ref[...] = _dot(xn_ref[...], w_ref[0])


def _normproj(h, mod, nw, w, l, tn, name):
    tm = 2 * TB
    n = w.shape[2]
    return pl.pallas_call(
        _normproj_kernel,
        grid=(T_ALL // tm, n // tn),
        in_specs=[pl.BlockSpec((tm, D_MODEL), lambda i, j: (i, 0)),
                  pl.BlockSpec((1, 1, 6 * D_MODEL), lambda i, j: (_blk_type(2 * i), 0, 0)),
                  pl.BlockSpec((1, D_MODEL), lambda i, j: (0, 0)),
                  pl.BlockSpec((1, D_MODEL, tn), lambda i, j: (l, 0, j))],
        out_specs=pl.BlockSpec((tm, tn), lambda i, j: (i, j)),
        out_shape=jax.ShapeDtypeStruct((T_ALL, n), F32),
        scratch_shapes=[pltpu.VMEM((tm, D_MODEL), BF16)],
        compiler_params=_cparams(("parallel", "arbitrary")),
        name=name,
    )(h, mod, nw, w)


def _with_halo(prev_ref, cur_ref, next_ref, i):
    prev = jnp.where(_blk_first(i), 0.0, prev_ref[...])
    nxt = jnp.where(_blk_last(i), 0.0, next_ref[...])
    return jnp.concatenate([prev, cur_ref[...], nxt], axis=0)


def _shifted(ext, s):
    n = ext.shape[0]
    return pltpu.roll(ext, (-s) % n, axis=0)[HALO:HALO + TB]


def _seg_sum(x, width):
    ii = lax.broadcasted_iota(jnp.int32, (LANE, LANE), 0)
    jj = lax.broadcasted_iota(jnp.int32, (LANE, LANE), 1)
    shift = int(math.log2(width))
    ones = ((ii >> shift) == (jj >> shift)).astype(BF16)
    parts = [_dot_exact_rhs(x[:, c:c + LANE], ones) for c in range(0, x.shape[1], LANE)]
    return jnp.concatenate(parts, axis=1)


def _rwkv_prep_kernel(pp_ref, p_ref, pn_ref, mu_ref, w0_ref, w2_ref, a0_ref, a2_ref, g2_ref, kk_w_ref, ka_ref,
                      rk_ref, r_ref, v_ref, kk_ref, g_ref, bonus_ref, lw_ref, kd_ref, b_ref):
    i = pl.program_id(0)
    ext = _with_halo(pp_ref, p_ref, pn_ref, i)
    p = p_ref[...]
    p = p + mu_ref[...] * (0.5 * (_shifted(ext, -1) + _shifted(ext, 1)) - p)
    r, k, v = p[:, 0:BR], p[:, BR:2 * BR], p[:, 2 * BR:3 * BR]
    wd, ad, gd = p[:, 1536:1664], p[:, 1664:1792], p[:, 1792:1920]
    kk = k * kk_w_ref[...]
    kk = kk / jnp.maximum(jnp.sqrt(_seg_sum(kk * kk, HD)), 1e-12)
    r_ref[...] = r
    v_ref[...] = v
    kk_ref[...] = kk
    tw = jnp.tanh(wd)
    g_ref[...] = _dot_split(_sigmoid(gd), g2_ref[...])
    wl = [_dot_split(tw, w2_ref[d]) for d in range(2)]
    al = [_dot_split(ad, a2_ref[d]) for d in range(2)]
    kd_sum = jnp.zeros_like(k)
    for d in range(2):
        lw_ref[d] = -_sigmoid(w0_ref[d:d + 1, :] + wl[d]) * math.exp(-0.5)
        a = _sigmoid(a0_ref[d:d + 1, :] + al[d])
        kd = k * (1.0 + (a - 1.0) * ka_ref[...])
        kd_ref[d] = kd
        b_ref[d] = kk * a
        kd_sum = kd_sum + kd
    bonus_ref[...] = _seg_sum(r * kd_sum * rk_ref[...], HD) * v


def _halo_specs(cols):
    per = TB // HALO
    return [pl.BlockSpec((HALO, cols), lambda i: (jnp.maximum(i * per - 1, 0), 0)),
            pl.BlockSpec((TB, cols), lambda i: (i, 0)),
            pl.BlockSpec((HALO, cols), lambda i: (jnp.minimum((i + 1) * per, T_ALL // HALO - 1), 0))]


def _rwkv_prep(p_rwkv, lp):
    tok = jax.ShapeDtypeStruct((T_ALL, BR), F32)
    tok2 = jax.ShapeDtypeStruct((2, T_ALL, BR), F32)
    spec1 = pl.BlockSpec((TB, BR), lambda i: (i, 0))
    spec2 = pl.BlockSpec((2, TB, BR), lambda i: (0, i, 0))
    return pl.pallas_call(
        _rwkv_prep_kernel,
        grid=(NBLK,),
        in_specs=_halo_specs(RWKV_COLS) + [
            _full((1, RWKV_COLS)), _full((2, BR)), _full((2, LANE, BR)), _full((2, BR)), _full((2, LANE, BR)),
            _full((LANE, BR)), _full((1, BR)), _full((1, BR)), _full((1, BR))],
        out_specs=[spec1, spec1, spec1, spec1, spec1, spec2, spec2, spec2],
        out_shape=[tok, tok, tok, tok, tok, tok2, tok2, tok2],
        compiler_params=_cparams(("parallel",)),
        name="rwkv_prep",
    )(p_rwkv, p_rwkv, p_rwkv, lp['rwkv_mu'], lp['rwkv_w0'], lp['rwkv_w2p'], lp['rwkv_a0'], lp['rwkv_a2p'],
      lp['rwkv_g2'], lp['rwkv_k_k'], lp['rwkv_k_a'], lp['rwkv_r_k'])


def _pair_bd(x):
    h0 = lax.broadcasted_iota(jnp.int32, x.shape, 1) < HD
    return jnp.concatenate([jnp.where(h0, x, 0.0), jnp.where(h0, 0.0, x)], axis=0)


def _rwkv_scan_kernel(r_ref, v_ref, kk_ref, lw_ref, kd_ref, b_ref, s0_ref, y_ref, sf_ref,
                      s_ref, lrhs_ref, ly_ref, t3_ref, v2_ref, ebt3_ref, wc_ref):
    d = pl.program_id(0)
    blk = _scan_blk(d, pl.program_id(1))
    first, last = _scan_ends(d, blk)
    sgn = 1 - 2 * d
    C = RWKV_CHUNK
    nchunk = TB // C
    npair = BR // LANE
    ii = lax.broadcasted_iota(jnp.int32, (LANE, LANE), 0)
    jj = lax.broadcasted_iota(jnp.int32, (LANE, LANE), 1)
    same = (ii >> 6) == (jj >> 6)

    @pl.when(jnp.logical_and(first, blk < N_CTX))
    def _():
        s_ref[...] = jnp.zeros_like(s_ref)

    @pl.when(jnp.logical_and(first, blk >= N_CTX))
    def _():
        for p in range(npair):
            a = jnp.concatenate([s0_ref[0, 0, 2 * p], s0_ref[0, 0, 2 * p + 1]], axis=1)
            s_ref[p] = jnp.where(same, jnp.concatenate([a, a], axis=0).T, 0.0)

    ci = lax.broadcasted_iota(jnp.int32, (C, C), 0)
    cj = lax.broadcasted_iota(jnp.int32, (C, C), 1)
    tri = ((cj - ci) * sgn <= 0).astype(BF16)
    rel = ((jj & (C - 1)) - (ii & (C - 1))) * sgn
    before = jnp.logical_and(same, rel < 0)
    before_incl = jnp.logical_and(same, rel <= 0)

    def rows_of(c):
        cc = c + d * (nchunk - 1 - 2 * c)
        return pl.ds(pl.multiple_of(cc * C, C), C)

    n_of = {}
    for c in range(nchunk):
        rows = rows_of(c)
        lw = lw_ref[0, rows, :]
        cum = _cumsum_rows(tri, lw)
        tot = jnp.sum(lw, axis=0, keepdims=True)
        e_neg = jnp.exp(-cum)
        e_end = jnp.exp(tot - cum)
        r = r_ref[rows, :]
        v = v_ref[rows, :]
        kk = kk_ref[rows, :]
        kd = kd_ref[0, rows, :]
        b = b_ref[0, rows, :]
        k_t, b_t = kd * e_neg, b * e_neg
        kk_h, r_h = kk * jnp.exp(cum - lw), r * jnp.exp(cum)
        k_e, b_e = kd * e_end, b * e_end
        for p in range(npair):
            sl = slice(p * LANE, (p + 1) * LANE)
            KK, R, KT, BT, V, KE, BE = (_pair_bd(x[:, sl]) for x in (kk_h, r_h, k_t, b_t, v, k_e, b_e))
            kk16, r16 = KK.astype(BF16), R.astype(BF16)
            g = _dot_nt(jnp.concatenate([kk16, r16], axis=0),
                        jnp.concatenate([BT, KT], axis=0).astype(BF16))
            n_of[c, p] = jnp.where(before, g[:LANE, :LANE], 0.0)
            a1 = jnp.where(before, g[:LANE, LANE:], 0.0)
            a4 = jnp.where(before_incl, g[LANE:, :LANE], 0.0)
            a3 = jnp.where(before_incl, g[LANE:, LANE:], 0.0)
            lrhs_ref[c, p] = jnp.concatenate([kk16, a1.astype(BF16)], axis=1)
            ly_ref[c, p] = jnp.concatenate([r16, a3.astype(BF16), a4.astype(BF16)], axis=1)
            vh, vl = _split16(V)
            v2_ref[c, p] = jnp.concatenate([vh, vl], axis=0)
            eh, el = _split16(jnp.concatenate([KE.T, BE.T], axis=1))
            ebt3_ref[c, p] = jnp.concatenate([eh, el, eh], axis=1)
            wc_ref[c, p] = jnp.exp(jnp.broadcast_to(tot[:, sl], (LANE, LANE)).T)

    chains = list(n_of)
    level = lambda f: {k: f(k) for k in chains}
    eye = (ii == jj).astype(F32)
    blk16 = (ii >> 4) == (jj >> 4)
    dg = level(lambda k: jnp.where(blk16, n_of[k], 0.0))
    low = level(lambda k: n_of[k] - dg[k])
    d2 = level(lambda k: _dot16(dg[k], dg[k]))
    d4 = level(lambda k: _dot16(d2[k], d2[k]))
    x1 = level(lambda k: _dot16(eye - dg[k], eye + d2[k]))
    d8 = level(lambda k: _dot16(d4[k], d4[k]))
    x2 = level(lambda k: _dot16(x1[k], eye + d4[k]))
    xd = level(lambda k: _dot16(x2[k], eye + d8[k]))
    m = level(lambda k: _dot16(xd[k], low[k]))
    m2 = level(lambda k: _dot16(m[k], m[k]))
    t1 = level(lambda k: _dot16(eye - m[k], eye + m2[k]))
    tinv = level(lambda k: _dot16(t1[k], xd[k]))
    for k in chains:
        th, tl = _split16(tinv[k])
        t3_ref[k] = jnp.concatenate([th, tl, th], axis=1)

    pairs = range(npair)
    for c in range(nchunk):
        rows = rows_of(c)
        s = [s_ref[p] for p in pairs]
        s16 = [x.astype(BF16) for x in s]
        vh = [v2_ref[c, p, :LANE] for p in pairs]
        vl = [v2_ref[c, p, LANE:] for p in pairs]
        rhs = [_split16(_dot(lrhs_ref[c, p], jnp.concatenate([s16[p], vh[p]], axis=0))) for p in pairs]
        u = [_split16(-_dot(t3_ref[c, p], jnp.concatenate([rhs[p][0], rhs[p][0], rhs[p][1]], axis=0)))
             for p in pairs]
        for p in pairs:
            uh, ul = u[p]
            s_ref[p] = s[p] * wc_ref[c, p] + _dot(
                ebt3_ref[c, p], jnp.concatenate([vh[p], uh, vh[p], uh, vl[p], ul], axis=0))
        for p in pairs:
            y = _dot(ly_ref[c, p], jnp.concatenate([s16[p], vh[p], u[p][0]], axis=0))
            y_ref[0, rows, p * LANE:(p + 1) * LANE] = y[:C] + y[C:]

    @pl.when(last)
    def _():
        for p in range(npair):
            sp = s_ref[p]
            folded = sp[:HD] + sp[HD:]
            z = jnp.concatenate([folded, folded], axis=0).T
            sf_ref[0, 0, 2 * p] = z[:HD, :HD]
            sf_ref[0, 0, 2 * p + 1] = z[HD:, :HD]


def _rwkv_scan(r, v, kk, lw, kd, b, s0):
    np_ = BR // LANE
    nchunk = TB // RWKV_CHUNK
    spec1 = pl.BlockSpec((TB, BR), lambda d, i: (_scan_blk(d, i), 0))
    spec2 = pl.BlockSpec((1, TB, BR), lambda d, i: (d, _scan_blk(d, i), 0))
    seq = lambda d, i: _blk_seq(_scan_blk(d, i))
    s0spec = pl.BlockSpec((1, 1, HEADS, HD, HD), lambda d, i: (_lat_of_seq(seq(d, i)), d, 0, 0, 0))
    sfspec = pl.BlockSpec((1, 1, HEADS, HD, HD), lambda d, i: (seq(d, i), d, 0, 0, 0))
    per = lambda rows, cols, dt: pltpu.VMEM((nchunk, np_, rows, cols), dt)
    return pl.pallas_call(
        _rwkv_scan_kernel,
        grid=(2, NBLK),
        in_specs=[spec1, spec1, spec1, spec2, spec2, spec2, s0spec],
        out_specs=[spec2, sfspec],
        out_shape=[jax.ShapeDtypeStruct((2, T_ALL, BR), F32),
                   jax.ShapeDtypeStruct((N_SEQ, 2, HEADS, HD, HD), F32)],
        scratch_shapes=[pltpu.VMEM((np_, LANE, LANE), F32),
                        per(LANE, 2 * LANE, BF16), per(LANE, 3 * LANE, BF16), per(LANE, 3 * LANE, BF16),
                        per(2 * LANE, LANE, BF16), per(LANE, 6 * LANE, BF16), per(LANE, LANE, F32)],
        compiler_params=_cparams(("arbitrary", "arbitrary")),
        name="rwkv_scan",
    )(r, v, kk, lw, kd, b, s0)


def _ssd_prep_kernel(pp_ref, p_ref, pn_ref, cw_ref, cb_ref, dtb_ref, x_ref, bc_ref, dt_ref):
    i = pl.program_id(0)
    ext = _with_halo(pp_ref, p_ref, pn_ref, i)[:, BR:BR + SSD_XBC]
    acc = cb_ref[...] + cw_ref[0:1, :] * _shifted(ext, -(SSD_CONV // 2))
    for j in range(1, SSD_CONV):
        acc = acc + cw_ref[j:j + 1, :] * _shifted(ext, j - SSD_CONV // 2)
    xbc = _silu(acc)
    x_ref[...] = xbc[:, :BR]
    bc_ref[...] = xbc[:, BR:]
    dt_ref[...] = _softplus(p_ref[:, BR + SSD_XBC:] + dtb_ref[...])


def _ssd_prep(p_ssd, lp):
    return pl.pallas_call(
        _ssd_prep_kernel,
        grid=(NBLK,),
        in_specs=_halo_specs(SSD_PCOLS) + [_full((8, SSD_XBC)), _full((1, SSD_XBC)), _full((1, 2 * LANE))],
        out_specs=[pl.BlockSpec((TB, BR), lambda i: (i, 0)),
                   pl.BlockSpec((TB, 2 * LANE), lambda i: (i, 0)),
                   pl.BlockSpec((TB, 2 * LANE), lambda i: (i, 0))],
        out_shape=[jax.ShapeDtypeStruct((T_ALL, BR), F32),
                   jax.ShapeDtypeStruct((T_ALL, 2 * LANE), F32),
                   jax.ShapeDtypeStruct((T_ALL, 2 * LANE), F32)],
        compiler_params=_cparams(("parallel",)),
        name="ssd_prep",
    )(p_ssd, p_ssd, p_ssd, lp['ssd_conv_w8'], lp['ssd_conv_b'], lp['ssd_dt_bias_p'])


def _ssd_scan_kernel(x_ref, bc_ref, dt_ref, alog_ref, h0_ref, y_ref, hf_ref, hs_ref):
    d = pl.program_id(0)
    blk = _scan_blk(d, pl.program_id(1))
    first, last = _scan_ends(d, blk)
    sgn = 1 - 2 * d
    npair = BR // LANE
    half64 = lax.broadcasted_iota(jnp.int32, (HD, LANE), 1) < HD

    @pl.when(jnp.logical_and(first, blk < N_CTX))
    def _():
        hs_ref[...] = jnp.zeros_like(hs_ref)

    @pl.when(jnp.logical_and(first, blk >= N_CTX))
    def _():
        zero = jnp.zeros((HD, LANE), F32)
        blocks = []
        for q in range(npair):
            a = jnp.concatenate([h0_ref[0, 0, 2 * q], h0_ref[0, 0, 2 * q + 1]], axis=1)
            t = jnp.concatenate([a, a], axis=0).T
            blocks.append(jnp.where(half64, t[:HD], t[HD:]))
        rows = [jnp.concatenate([blocks[q] if q // 2 == g else zero for q in range(npair)], axis=1)
                for g in range(2)]
        hs_ref[...] = jnp.concatenate(rows, axis=0)

    x = x_ref[...]
    bm = bc_ref[:, :LANE]
    cm = bc_ref[:, LANE:]
    dt = dt_ref[...]
    a_neg = -jnp.exp(alog_ref[0])
    dta = dt * a_neg
    qi = lax.broadcasted_iota(jnp.int32, (TB, TB), 0)
    qj = lax.broadcasted_iota(jnp.int32, (TB, TB), 1)
    before_incl = (qj - qi) * sgn <= 0
    a_cum = _cumsum_rows(before_incl.astype(BF16), dta)
    tot = jnp.sum(dta, axis=0, keepdims=True)
    a_cum_t = a_cum.T
    dt_t = dt.T
    eh = lax.broadcasted_iota(jnp.int32, (LANE, BR), 0)
    ec = lax.broadcasted_iota(jnp.int32, (LANE, BR), 1)
    expand = ((ec >> 6) == eh).astype(BF16)
    e_in = _dot_exact_rhs(jnp.exp(a_cum), expand)
    to_end = _dot_exact_rhs(jnp.exp(tot - a_cum) * dt, expand)
    dec = _dot_exact_rhs(jnp.broadcast_to(jnp.exp(tot), (8, LANE)), expand)[0:1]
    hs = hs_ref[...]
    cb16, bb16, xb16 = cm.astype(BF16), bm.astype(BF16), x.astype(BF16)
    glane = lax.broadcasted_iota(jnp.int32, (TB, LANE), 1) >> 6
    half = lax.broadcasted_iota(jnp.int32, (TB, LANE), 1) < HD
    cbg = [_dot_nt(jnp.where(glane == g, cm, 0.0).astype(BF16), bb16) for g in range(2)]
    y_off = _dot(cb16, hs.astype(BF16)) * e_in
    upd = _dot_tn(bb16, (x * to_end).astype(BF16))
    scores = []
    for h in range(HEADS):
        seg = a_cum[:, h:h + 1] - a_cum_t[h:h + 1, :]
        decay = jnp.exp(jnp.where(before_incl, seg, -jnp.inf))
        scores.append((cbg[h // 4] * decay * dt_t[h:h + 1, :]).astype(BF16))
    y_heads = [_dot(scores[h], xb16[:, (h // 2) * LANE:(h // 2 + 1) * LANE]) for h in range(HEADS)]
    y_parts = [jnp.where(half, y_heads[2 * q], y_heads[2 * q + 1]) for q in range(npair)]
    y_ref[0] = jnp.concatenate(y_parts, axis=1) + y_off
    ui = lax.broadcasted_iota(jnp.int32, (LANE, BR), 0)
    uj = lax.broadcasted_iota(jnp.int32, (LANE, BR), 1)
    hs_ref[...] = hs * dec + jnp.where((ui >> 6) == (uj >> 8), upd, 0.0)

    @pl.when(last)
    def _():
        hn = hs_ref[...]
        for q in range(npair):
            g = q // 2
            w = hn[g * HD:(g + 1) * HD, q * LANE:(q + 1) * LANE]
            z = jnp.concatenate([w, w], axis=0).T
            hf_ref[0, 0, 2 * q] = z[:HD, :HD]
            hf_ref[0, 0, 2 * q + 1] = z[HD:, :HD]


def _ssd_scan(x, bc, dt, alog, h0):
    seq = lambda d, i: _blk_seq(_scan_blk(d, i))
    return pl.pallas_call(
        _ssd_scan_kernel,
        grid=(2, NBLK),
        in_specs=[pl.BlockSpec((TB, BR), lambda d, i: (_scan_blk(d, i), 0)),
                  pl.BlockSpec((TB, 2 * LANE), lambda d, i: (_scan_blk(d, i), 0)),
                  pl.BlockSpec((TB, LANE), lambda d, i: (_scan_blk(d, i), d)),
                  pl.BlockSpec((1, 1, LANE), lambda d, i: (d, 0, 0)),
                  pl.BlockSpec((1, 1, HEADS, HD, HD), lambda d, i: (_lat_of_seq(seq(d, i)), d, 0, 0, 0))],
        out_specs=[pl.BlockSpec((1, TB, BR), lambda d, i: (d, _scan_blk(d, i), 0)),
                   pl.BlockSpec((1, 1, HEADS, HD, HD), lambda d, i: (seq(d, i), d, 0, 0, 0))],
        out_shape=[jax.ShapeDtypeStruct((2, T_ALL, BR), F32),
                   jax.ShapeDtypeStruct((N_SEQ, 2, HEADS, HD, HD), F32)],
        scratch_shapes=[pltpu.VMEM((LANE, BR), F32)],
        compiler_params=_cparams(("arbitrary", "arbitrary")),
        name="ssd_scan",
    )(x, bc, dt, alog, h0)


def _gqa(q, kd, vd, sink_ref, bias):
    nq = q.shape[0]
    log2e = math.log2(math.e)
    half = lax.broadcasted_iota(jnp.int32, (nq, LANE), 1) < HD
    q = q * (ATT_SCALE * log2e)
    scores = []
    for g in range(ATT_KVH):
        rows = []
        for j in (2 * g, 2 * g + 1):
            q2 = q[:, j * LANE:(j + 1) * LANE]
            rows += [jnp.where(half, q2, 0.0), jnp.where(half, 0.0, q2)]
        qg = jnp.concatenate(rows, axis=0).astype(BF16)
        scores.append(_dot_nt(qg, kd[g].astype(BF16)))
    outs = []
    for g in range(ATT_KVH):
        s = scores[g]
        if bias is not None:
            kb = bias.shape[1]
            s = jnp.concatenate([s[:, :kb] + jnp.concatenate([bias] * ATT_GROUP, axis=0), s[:, kb:]], axis=1)
        sink = jnp.concatenate(
            [jnp.full((nq, 1), sink_ref[ATT_GROUP * g + h] * log2e, F32) for h in range(ATT_GROUP)], axis=0)
        mx = jnp.maximum(jnp.max(s, axis=-1, keepdims=True), sink)
        e = jnp.exp2(s - mx)
        inv = 1.0 / (jnp.sum(e, axis=-1, keepdims=True) + jnp.exp2(sink - mx))
        o = _dot(e.astype(BF16), vd[g].astype(BF16)) * inv
        outs += [jnp.where(half, o[0:nq], o[nq:2 * nq]), jnp.where(half, o[2 * nq:3 * nq], o[3 * nq:])]
    return jnp.concatenate(outs, axis=1)


def _ctx_attn_kernel(sink_ref, p_ref, o_ref):
    kd = [p_ref[:, BR + g * LANE:BR + (g + 1) * LANE] for g in range(ATT_KVH)]
    vd = [p_ref[:, BR + (2 + g) * LANE:BR + (3 + g) * LANE] for g in range(ATT_KVH)]
    o_ref[...] = _gqa(p_ref[:, :BR], kd, vd, sink_ref, None)


def _ctx_attn(p_att, sink):
    return pl.pallas_call(
        _ctx_attn_kernel,
        grid=(N_CTX,),
        in_specs=[pl.BlockSpec(memory_space=pltpu.SMEM), pl.BlockSpec((TB, ATT_PCOLS), lambda i: (i, 0))],
        out_specs=pl.BlockSpec((TB, BR), lambda i: (i, 0)),
        out_shape=jax.ShapeDtypeStruct((T_CTX, BR), F32),
        compiler_params=_cparams(("parallel",)),
        name="ctx_attn",
    )(sink, p_att)


def _rope(x, cos, sin_signed):
    lanes = x.shape[1]
    reps = lanes // LANE
    if reps > 1:
        cos = jnp.concatenate([cos] * reps, axis=1)
        sin_signed = jnp.concatenate([sin_signed] * reps, axis=1)
    lo = (lax.broadcasted_iota(jnp.int32, x.shape, 1) & 31) < 16
    partner = jnp.where(lo, pltpu.roll(x, lanes - 16, axis=1), pltpu.roll(x, 16, axis=1))
    return x * cos + partner * sin_signed


def _lat_attn_kernel(sink_ref, pq_ref, pp_ref, pn_ref, cq_ref, sq_ref, cp_ref, sp_ref, cn_ref, sn_ref,
                     kc_ref, vc_ref, o_ref):
    j = pl.program_id(1)
    hw = ATT_WINDOW
    q = _rope(pq_ref[:, :BR], cq_ref[...], sq_ref[...])
    kd, vd = [], []
    for g in range(ATT_KVH):
        kc, vc = BR + g * LANE, BR + (2 + g) * LANE
        kh, vh = g * LANE, (2 + g) * LANE
        kd.append(jnp.concatenate([_rope(pp_ref[:, kh:kh + LANE], cp_ref[...], sp_ref[...]),
                                   _rope(pq_ref[:, kc:kc + LANE], cq_ref[...], sq_ref[...]),
                                   _rope(pn_ref[:, kh:kh + LANE], cn_ref[...], sn_ref[...]),
                                   kc_ref[0, g]], axis=0))
        vd.append(jnp.concatenate([pp_ref[:, vh:vh + LANE], pq_ref[:, vc:vc + LANE], pn_ref[:, vh:vh + LANE],
                                   vc_ref[0, g]], axis=0))
    nloc = TB + 2 * hw
    qi = lax.broadcasted_iota(jnp.int32, (TB, nloc), 0)
    kj = lax.broadcasted_iota(jnp.int32, (TB, nloc), 1)
    rel = kj - hw - qi
    kpos = j * TB - hw + kj
    valid = (rel <= ATT_WINDOW) & (rel >= -ATT_WINDOW) & (kpos >= 0) & (kpos < LAT_LEN)
    o_ref[...] = _gqa(q, kd, vd, sink_ref, jnp.where(valid, 0.0, -jnp.inf))


def _lat_attn(p_att, sink, rope_cos, rope_sin, k_ctx, v_ctx):
    nb = LAT_LEN // TB
    hw = ATT_WINDOW
    per = TB // hw
    qrow = lambda b, j: N_CTX + b * nb + j
    hrow = lambda b, j: per * (N_CTX + b * nb)
    prv = lambda j: jnp.maximum(per * j - 1, 0)
    nxt = lambda j: jnp.minimum(per * (j + 1), per * nb - 1)
    hspec = lambda f: pl.BlockSpec((hw, 4 * LANE), lambda b, j: (hrow(b, j) + f(j), 1))
    tspec = lambda rows, f: pl.BlockSpec((rows, LANE), lambda b, j: (f(j), 0))
    cspec = pl.BlockSpec((1, ATT_KVH, PAST_LEN, LANE), lambda b, j: (b, 0, 0, 0))
    return pl.pallas_call(
        _lat_attn_kernel,
        grid=(N_LAT, nb),
        in_specs=[pl.BlockSpec(memory_space=pltpu.SMEM),
                  pl.BlockSpec((TB, ATT_PCOLS), lambda b, j: (qrow(b, j), 0)), hspec(prv), hspec(nxt),
                  tspec(TB, lambda j: j), tspec(TB, lambda j: j), tspec(hw, prv), tspec(hw, prv),
                  tspec(hw, nxt), tspec(hw, nxt), cspec, cspec],
        out_specs=pl.BlockSpec((TB, BR), lambda b, j: (b * nb + j, 0)),
        out_shape=jax.ShapeDtypeStruct((N_LAT * LAT_LEN, BR), F32),
        compiler_params=_cparams(("parallel", "parallel")),
        name="lat_attn",
    )(sink, p_att, p_att, p_att, rope_cos, rope_sin, rope_cos, rope_sin, rope_cos, rope_sin, k_ctx, v_ctx)


def _merge_kernel(h_ref, mod_ref, yr_ref, bonus_ref, g_ref, lnw_ref, lnb_ref, ys_ref, xs_ref, z_ref, dvec_ref,
                  snw_ref, yac_ref, yal_ref, pg_ref, wb_ref, wo_ref, o_ref):
    y = yr_ref[0] + yr_ref[1] + bonus_ref[...]
    mu = _seg_sum(y, HD) * (1.0 / HD)
    yc = y - mu
    var = _seg_sum(yc * yc, HD) * (1.0 / HD)
    y_a = (yc * lax.rsqrt(var + RWKV_GN_EPS) * lnw_ref[...] + lnb_ref[...]) * g_ref[...]
    y = (ys_ref[0] + ys_ref[1] + dvec_ref[...] * xs_ref[...]) * _silu(z_ref[...])
    y_b = y * lax.rsqrt(jnp.mean(y * y, axis=-1, keepdims=True) + NORM_EPS) * snw_ref[...]
    y_c = jnp.where(pl.program_id(0) < N_CTX, yac_ref[...], yal_ref[...])
    wide = [_dot(br.astype(BF16), wb_ref[0, n]) for n, br in enumerate((y_a, y_b, y_c))]
    merged = None
    for n in range(3):
        term = _sigmoid(pg_ref[:, n * D_MODEL:(n + 1) * D_MODEL]) * wide[n]
        merged = term if merged is None else merged + term
    g1 = mod_ref[0][:, 2 * D_MODEL:3 * D_MODEL]
    o_ref[...] = h_ref[...] + g1 * _dot(merged.astype(BF16), wo_ref[0])


def _merge(h, mod, yr, bonus, g, ys, xs, p_ssd, ya_ctx, ya_lat, p_gate, lp, w_branch, w_out, l):
    tm = TB
    row = lambda w: pl.BlockSpec((tm, w), lambda i: (i, 0))
    row2 = pl.BlockSpec((2, tm, BR), lambda i: (0, i, 0))
    return pl.pallas_call(
        _merge_kernel,
        grid=(T_ALL // tm,),
        in_specs=[row(D_MODEL), pl.BlockSpec((1, 1, 6 * D_MODEL), lambda i: (_blk_type(i), 0, 0)),
                  row2, row(BR), row(BR), _full((1, BR)), _full((1, BR)),
                  row2, row(BR), row(BR), _full((1, BR)), _full((1, BR)),
                  pl.BlockSpec((tm, BR), lambda i: (jnp.minimum(i, N_CTX - 1), 0)),
                  pl.BlockSpec((tm, BR), lambda i: (jnp.maximum(i - N_CTX, 0), 0)),
                  row(GATE_COLS),
                  pl.BlockSpec((1, 3, BR, D_MODEL), lambda i: (l, 0, 0, 0)),
                  pl.BlockSpec((1, D_MODEL, D_MODEL), lambda i: (l, 0, 0))],
        out_specs=row(D_MODEL),
        out_shape=jax.ShapeDtypeStruct((T_ALL, D_MODEL), F32),
        compiler_params=_cparams(("parallel",)),
        name="merge",
    )(h, mod, yr, bonus, g, lp['rwkv_lnx_w'], lp['rwkv_lnx_b'], ys, xs, p_ssd, lp['ssd_d_vec'], lp['ssd_norm_w'],
      ya_ctx, ya_lat, p_gate, w_branch, w_out)


def _route_kernel(h_ref, mod_ref, nw_ref, wr_ref, xn_ref, gate_ref):
    m = mod_ref[0]
    xn = _normmod(h_ref[...], nw_ref[...], m[:, 3 * D_MODEL:4 * D_MODEL], m[:, 4 * D_MODEL:5 * D_MODEL])
    xn_ref[...] = xn.astype(BF16)
    logits = _dot(xn, wr_ref[...], HIGHEST)
    lane_i = lax.broadcasted_iota(jnp.int32, logits.shape, 1)
    lane = lane_i.astype(F32)
    lane_grp = (lane_i >> 3).astype(F32)
    neg = -jnp.inf
    big = float(LANE)
    is_g = (lane_i >= MOE_EXPERTS) & (lane_i < MOE_EXPERTS + MOE_GROUPS)
    gl = jnp.where(is_g, logits, neg)
    gmax = jnp.max(gl, axis=-1, keepdims=True)
    gsel = jnp.min(jnp.where(gl == gmax, lane - MOE_EXPERTS, big), axis=-1, keepdims=True)
    g_w = 1.0 / jnp.sum(jnp.where(is_g, jnp.exp(gl - gmax), 0.0), axis=-1, keepdims=True)
    el = jnp.where((lane_i < MOE_EXPERTS) & (lane_grp == gsel), logits, neg)
    m1 = jnp.max(el, axis=-1, keepdims=True)
    i1 = jnp.min(jnp.where(el == m1, lane, big), axis=-1, keepdims=True)
    el2 = jnp.where(lane == i1, neg, el)
    m2 = jnp.max(el2, axis=-1, keepdims=True)
    i2 = jnp.min(jnp.where(el2 == m2, lane, big), axis=-1, keepdims=True)
    e2 = jnp.exp(m2 - m1)
    w1 = 1.0 / (1.0 + e2)
    gate_ref[...] = jnp.where(lane == i1, w1 * g_w, jnp.where(lane == i2, e2 * w1 * g_w, 0.0))


def _route(h, mod, nw, wr):
    tm = TB
    return pl.pallas_call(
        _route_kernel,
        grid=(T_ALL // tm,),
        in_specs=[pl.BlockSpec((tm, D_MODEL), lambda i: (i, 0)),
                  pl.BlockSpec((1, 1, 6 * D_MODEL), lambda i: (_blk_type(i), 0, 0)),
                  _full((1, D_MODEL)), _full((D_MODEL, LANE))],
        out_specs=[pl.BlockSpec((tm, D_MODEL), lambda i: (i, 0)), pl.BlockSpec((tm, LANE), lambda i: (i, 0))],
        out_shape=[jax.ShapeDtypeStruct((T_ALL, D_MODEL), BF16), jax.ShapeDtypeStruct((T_ALL, LANE), F32)],
        compiler_params=_cparams(("parallel",)),
        name="moe_route",
    )(h, mod, nw, wr)


def _experts_kernel(xn_ref, gate_ref, w1_ref, w3_ref, w2_ref, h_ref, mod_ref, o_ref, acc_ref):
    e = pl.program_id(1)

    @pl.when(e == 0)
    def _():
        acc_ref[...] = jnp.zeros_like(acc_ref)

    gate = gate_ref[...]
    lane = lax.broadcasted_iota(jnp.int32, gate.shape, 1)
    gcol = jnp.sum(jnp.where(lane == e, gate, 0.0), axis=-1, keepdims=True)
    xn = xn_ref[...]
    act = _silu(_dot(xn, w1_ref[0, 0])) * _dot(xn, w3_ref[0, 0]) * gcol
    acc_ref[...] += _dot(act.astype(BF16), w2_ref[0, 0])

    @pl.when(e == MOE_EXPERTS - 1)
    def _():
        g2 = mod_ref[0][:, 5 * D_MODEL:6 * D_MODEL]
        o_ref[...] = h_ref[...] + g2 * acc_ref[...]


def _experts(xn, gate, w1, w3, w2, h, mod, l):
    tm = 4 * TB
    return pl.pallas_call(
        _experts_kernel,
        grid=(T_ALL // tm, MOE_EXPERTS),
        in_specs=[pl.BlockSpec((tm, D_MODEL), lambda i, e: (i, 0)),
                  pl.BlockSpec((tm, LANE), lambda i, e: (i, 0)),
                  pl.BlockSpec((1, 1, D_MODEL, D_EXPERT), lambda i, e: (l, e, 0, 0)),
                  pl.BlockSpec((1, 1, D_MODEL, D_EXPERT), lambda i, e: (l, e, 0, 0)),
                  pl.BlockSpec((1, 1, D_EXPERT, D_MODEL), lambda i, e: (l, e, 0, 0)),
                  pl.BlockSpec((tm, D_MODEL), lambda i, e: (i, 0)),
                  pl.BlockSpec((1, 1, 6 * D_MODEL), lambda i, e: (_blk_type(4 * i), 0, 0))],
        out_specs=pl.BlockSpec((tm, D_MODEL), lambda i, e: (i, 0)),
        out_shape=jax.ShapeDtypeStruct((T_ALL, D_MODEL), F32),
        scratch_shapes=[pltpu.VMEM((tm, D_MODEL), F32)],
        compiler_params=_cparams(("parallel", "arbitrary")),
        name="moe_experts",
    )(xn, gate, w1, w3, w2, h, mod)


def _final_norm_kernel(h_ref, w_ref, o_ref):
    x = h_ref[...]
    o_ref[...] = x * lax.rsqrt(jnp.mean(x * x, axis=-1, keepdims=True) + NORM_EPS) * w_ref[...]


def _final_norm(h, w, row0, rows, name):
    tm = 2 * TB
    return pl.pallas_call(
        _final_norm_kernel,
        grid=(rows // tm,),
        in_specs=[pl.BlockSpec((tm, D_MODEL), lambda i: (row0 // tm + i, 0)), _full((1, D_MODEL))],
        out_specs=pl.BlockSpec((tm, D_MODEL), lambda i: (i, 0)),
        out_shape=jax.ShapeDtypeStruct((rows, D_MODEL), F32),
        compiler_params=_cparams(("parallel",)),
        name=name,
    )(h, w)


def _rope_tables():
    pos = jnp.arange(LAT_LEN)
    row = (pos // GRID_W).astype(F32)
    col = (pos % GRID_W).astype(F32)
    half = HD // 2
    inv = 1.0 / (ROPE_BASE ** (jnp.arange(0, half, 2, dtype=F32) / half))
    ar, ac = row[:, None] * inv[None, :], col[:, None] * inv[None, :]
    cos = jnp.concatenate([jnp.cos(ar), jnp.cos(ar), jnp.cos(ac), jnp.cos(ac)], axis=1)
    sin = jnp.concatenate([-jnp.sin(ar), jnp.sin(ar), -jnp.sin(ac), jnp.sin(ac)], axis=1)
    return jnp.tile(cos, (1, 2)), jnp.tile(sin, (1, 2))


def _dup_heads(x):
    x = jnp.swapaxes(x, -2, -3)
    return jnp.concatenate([x, x], axis=-1)


def _proj_weights(w_in):
    o_ssd, o_att = RWKV_COLS, RWKV_COLS + 1296
    o_gate = o_att + BR + 2 * ATT_KVH * HD
    pad = jnp.zeros((DEPTH, D_MODEL, LANE - HEADS), F32)
    dt0 = o_ssd + BR + SSD_XBC
    w_ssd = jnp.concatenate([w_in[:, :, o_ssd:dt0], w_in[:, :, dt0:dt0 + HEADS], pad,
                             w_in[:, :, dt0 + HEADS:dt0 + 2 * HEADS], pad], axis=2)
    kv = [w_in[:, :, o_att + BR + n * HD:o_att + BR + (n + 1) * HD] for n in range(2 * ATT_KVH)]
    w_att = jnp.concatenate([w_in[:, :, o_att:o_att + BR]] + [x for c in kv for x in (c, c)], axis=2)
    return (w_in[:, :, :o_ssd].astype(BF16), w_ssd.astype(BF16), w_att.astype(BF16),
            w_in[:, :, o_gate:].astype(BF16))


def _layer_params(l, prm):
    lp = {}
    row = lambda name: prm[name][l].reshape(1, -1)
    for name in ('norm1_w', 'norm2_w', 'rwkv_mu', 'rwkv_k_k', 'rwkv_k_a', 'rwkv_r_k', 'rwkv_lnx_w', 'rwkv_lnx_b',
                 'ssd_conv_b', 'ssd_norm_w'):
        lp[name] = row(name)
    lp['rwkv_w0'] = prm['rwkv_w0'][l]
    lp['rwkv_a0'] = prm['rwkv_a0'][l]
    z64 = jnp.zeros((HD, BR), F32)
    w2, a2 = prm['rwkv_w2'][l], prm['rwkv_a2'][l]
    lp['rwkv_w2p'] = jnp.stack([jnp.concatenate([w2[0], z64]), jnp.concatenate([z64, w2[1]])])
    lp['rwkv_a2p'] = jnp.stack([jnp.concatenate([a2[0], z64]), jnp.concatenate([z64, a2[1]])])
    lp['rwkv_g2'] = prm['rwkv_g2'][l]
    lp['ssd_conv_w8'] = jnp.concatenate([prm['ssd_conv_w'][l], jnp.zeros((8 - SSD_CONV, SSD_XBC), F32)])
    dtb = prm['ssd_dt_bias'][l]
    zp = jnp.zeros((LANE - HEADS,), F32)
    lp['ssd_dt_bias_p'] = jnp.concatenate([dtb[0], zp, dtb[1], zp]).reshape(1, 2 * LANE)
    lp['ssd_a_log_p'] = jnp.pad(prm['ssd_a_log'][l], ((0, 0), (0, LANE - HEADS))).reshape(2, 1, LANE)
    lp['ssd_d_vec'] = jnp.repeat(prm['ssd_d'][l], HD).reshape(1, BR)
    lp['att_sink'] = prm['att_sink'][l]
    wr = jnp.concatenate([prm['moe_w_expert'][l].reshape(D_MODEL, MOE_EXPERTS), prm['moe_w_group'][l]], axis=1)
    lp['w_route'] = jnp.pad(wr, ((0, 0), (0, LANE - MOE_EXPERTS - MOE_GROUPS)))
    return lp


def kernel(x_prompt, x_sample, cache_attn_k, cache_attn_v, state_rwkv, state_ssd, c, c_ctx, w_ada, b_ada, norm1_w, norm2_w, w_in, rwkv_mu, rwkv_w0, rwkv_w2, rwkv_a0, rwkv_a2, rwkv_g2, rwkv_k_k, rwkv_k_a, rwkv_r_k, rwkv_lnx_w, rwkv_lnx_b, ssd_conv_w, ssd_conv_b, ssd_dt_bias, ssd_a_log, ssd_d, ssd_norm_w, att_sink, w_branch, w_out, moe_w_group, moe_w_expert, moe_w1, moe_w3, moe_w2, final_norm_w):
    prm = dict(norm1_w=norm1_w, norm2_w=norm2_w, rwkv_mu=rwkv_mu, rwkv_w0=rwkv_w0, rwkv_w2=rwkv_w2,
               rwkv_a0=rwkv_a0, rwkv_a2=rwkv_a2, rwkv_g2=rwkv_g2, rwkv_k_k=rwkv_k_k, rwkv_k_a=rwkv_k_a,
               rwkv_r_k=rwkv_r_k.reshape(DEPTH, BR), rwkv_lnx_w=rwkv_lnx_w, rwkv_lnx_b=rwkv_lnx_b,
               ssd_conv_w=ssd_conv_w, ssd_conv_b=ssd_conv_b, ssd_dt_bias=ssd_dt_bias, ssd_a_log=ssd_a_log,
               ssd_d=ssd_d, ssd_norm_w=ssd_norm_w, att_sink=att_sink,
               moe_w_group=moe_w_group, moe_w_expert=moe_w_expert)
    w_rwkv, w_ssd, w_att, w_gate = _proj_weights(w_in)
    wb16, wo16 = w_branch.astype(BF16), w_out.astype(BF16)
    w1_16, w3_16, w2_16 = moe_w1.astype(BF16), moe_w3.astype(BF16), moe_w2.astype(BF16)
    kc_dup, vc_dup = _dup_heads(cache_attn_k), _dup_heads(cache_attn_v)
    cond8 = jnp.concatenate([c_ctx[None, :], c, jnp.zeros((8 - 1 - N_LAT, D_MODEL), F32)], axis=0)
    mod_all = _adaln(cond8, w_ada, b_ada)
    rope_cos, rope_sin = _rope_tables()
    h = jnp.concatenate([x_prompt.reshape(T_CTX, D_MODEL), x_sample.reshape(N_LAT * LAT_LEN, D_MODEL)], axis=0)
    ks, vs, srs, sss = [], [], [], []
    for l in range(DEPTH):
        lp = _layer_params(l, prm)
        mod = mod_all[l].reshape(8, 1, 6 * D_MODEL)
        p_rwkv = _normproj(h, mod, lp['norm1_w'], w_rwkv, l, RWKV_COLS, "proj_rwkv")
        p_ssd = _normproj(h, mod, lp['norm1_w'], w_ssd, l, SSD_PCOLS, "proj_ssd")
        p_att = _normproj(h, mod, lp['norm1_w'], w_att, l, ATT_PCOLS, "proj_att")
        p_gate = _normproj(h, mod, lp['norm1_w'], w_gate, l, GATE_COLS // 2, "proj_gate")
        r, v, kk, g, bonus, lw, kd, b = _rwkv_prep(p_rwkv, lp)
        yr, sf = _rwkv_scan(r, v, kk, lw, kd, b, state_rwkv[:, l])
        srs.append(sf)
        xs, bc, dt = _ssd_prep(p_ssd, lp)
        ys, hf = _ssd_scan(xs, bc, dt, lp['ssd_a_log_p'], state_ssd[:, l])
        sss.append(hf)
        ya_ctx = _ctx_attn(p_att, lp['att_sink'])
        ya_lat = _lat_attn(p_att, lp['att_sink'], rope_cos, rope_sin, kc_dup[:, l], vc_dup[:, l])
        kv = [p_att[:T_CTX, BR + n * LANE:BR + n * LANE + HD].reshape(N_CTX, TB, HD) for n in range(2 * ATT_KVH)]
        ks.append(jnp.stack(kv[:ATT_KVH], axis=2))
        vs.append(jnp.stack(kv[ATT_KVH:], axis=2))
        h = _merge(h, mod, yr, bonus, g, ys, xs, p_ssd, ya_ctx, ya_lat, p_gate, lp, wb16, wo16, l)
        xn, gate = _route(h, mod, lp['norm2_w'], lp['w_route'])
        h = _experts(xn, gate, w1_16, w3_16, w2_16, h, mod, l)
    fw = final_norm_w.reshape(1, D_MODEL)
    y_ctx = _final_norm(h, fw, 0, T_CTX, "final_norm_ctx")
    y_lat = _final_norm(h, fw, T_CTX, N_LAT * LAT_LEN, "final_norm_lat")
    return (y_ctx.reshape(N_CTX, TB, D_MODEL), y_lat.reshape(N_LAT, LAT_LEN, D_MODEL),
            jnp.stack(ks, axis=1), jnp.stack(vs, axis=1),
            jnp.stack(srs, axis=1)[:N_CTX], jnp.stack(sss, axis=1)[:N_CTX])
```

```python
import math

import jax
import jax.numpy as jnp
from jax import lax
from jax.experimental import pallas as pl
from jax.experimental.pallas import tpu as pltpu

F32 = jnp.float32
BF16 = jnp.bfloat16
HIGHEST = lax.Precision.HIGHEST

D_MODEL = 1024
DEPTH = 4
N_CTX = 16
N_LAT = 2
LAT_LEN = 4096
PAST_LEN = 512
GRID_W = 64
NORM_EPS = 1e-6
TB = 256
NLB = LAT_LEN // TB
NBLK = N_CTX + N_LAT * NLB
T_CTX = N_CTX * TB
T_ALL = NBLK * TB
N_SEQ = N_CTX + N_LAT
HALO = 8

HEADS = 8
HD = 64
BR = HEADS * HD
RWKV_COLS = 1920
RWKV_CHUNK = 64
RWKV_GN_EPS = 64e-5
SSD_CONV = 5
SSD_XBC = 768
SSD_PCOLS = 1536
ATT_KVH = 2
ATT_GROUP = 4
ATT_PCOLS = BR + 4 * 2 * HD
ATT_WINDOW = 128
ATT_SCALE = HD ** -0.5
ROPE_BASE = 10000.0
GATE_COLS = 3 * D_MODEL
MOE_GROUPS = 4
MOE_EXPERTS = 32
D_EXPERT = 256
LANE = 128

VMEM_LIMIT = 48 * 1024 * 1024


def _cparams(sem):
    return pltpu.CompilerParams(dimension_semantics=sem, vmem_limit_bytes=VMEM_LIMIT)


def _dot(a, b, precision=None):
    return jnp.dot(a, b, precision=precision, preferred_element_type=F32)


def _dot_nt(a, b):
    return lax.dot_general(a, b, (((1,), (1,)), ((), ())), preferred_element_type=F32)


def _dot_tn(a, b):
    return lax.dot_general(a, b, (((0,), (0,)), ((), ())), preferred_element_type=F32)


def _split16(x):
    hi = x.astype(BF16)
    return hi, (x - hi.astype(F32)).astype(BF16)


def _dot16(a, b):
    return _dot(a.astype(BF16), b.astype(BF16))


def _dot_split(a, b):
    ah, al = _split16(a)
    bh, bl = _split16(b)
    return _dot(jnp.concatenate([ah, al, ah], axis=1), jnp.concatenate([bh, bh, bl], axis=0))


def _dot_exact_rhs(a, b16):
    ah, al = _split16(a)
    return _dot(jnp.concatenate([ah, al], axis=1), jnp.concatenate([b16, b16], axis=0))


def _cumsum_rows(tri16, x):
    x0 = x.astype(BF16)
    r1 = x - x0.astype(F32)
    x1 = r1.astype(BF16)
    x2 = (r1 - x1.astype(F32)).astype(BF16)
    return _dot(tri16, x0) + _dot(tri16, x1) + _dot(tri16, x2)


def _sigmoid(x):
    return 1.0 / (1.0 + jnp.exp(-x))


def _silu(x):
    return x * _sigmoid(x)


def _softplus(x):
    return jnp.maximum(x, 0.0) + jnp.log(1.0 + jnp.exp(-jnp.abs(x)))


def _full(shape):
    return pl.BlockSpec(shape, lambda *_: (0,) * len(shape))


def _blk_type(i):
    return jnp.where(i < N_CTX, 0, 1 + (i - N_CTX) // NLB)


def _blk_seq(i):
    return jnp.where(i < N_CTX, i, N_CTX + (i - N_CTX) // NLB)


def _blk_first(i):
    return jnp.logical_or(i < N_CTX, (i - N_CTX) % NLB == 0)


def _blk_last(i):
    return jnp.logical_or(i < N_CTX, (i - N_CTX) % NLB == NLB - 1)


def _scan_blk(d, i):
    return i + d * (NBLK - 1 - 2 * i)


def _scan_ends(d, blk):
    fwd = d == 0
    a, b = _blk_first(blk), _blk_last(blk)
    first = jnp.logical_or(jnp.logical_and(fwd, a), jnp.logical_and(jnp.logical_not(fwd), b))
    last = jnp.logical_or(jnp.logical_and(fwd, b), jnp.logical_and(jnp.logical_not(fwd), a))
    return first, last


def _lat_of_seq(seq):
    return jnp.clip(seq - N_CTX, 0, N_LAT - 1)


def _adaln_kernel(c_ref, w_ref, b_ref, o_ref):
    o_ref[0] = _dot(_silu(c_ref[...]), w_ref[0], HIGHEST) + b_ref[0]


def _adaln(cond8, w_ada, b_ada):
    return pl.pallas_call(
        _adaln_kernel,
        grid=(DEPTH, 6),
        in_specs=[pl.BlockSpec((8, D_MODEL), lambda l, j: (0, 0)),
                  pl.BlockSpec((1, D_MODEL, D_MODEL), lambda l, j: (l, 0, j)),
                  pl.BlockSpec((1, 1, D_MODEL), lambda l, j: (l, 0, j))],
        out_specs=pl.BlockSpec((1, 8, D_MODEL), lambda l, j: (l, 0, j)),
        out_shape=jax.ShapeDtypeStruct((DEPTH, 8, 6 * D_MODEL), F32),
        compiler_params=_cparams(("parallel", "parallel")),
        name="adaln",
    )(cond8, w_ada, b_ada.reshape(DEPTH, 1, 6 * D_MODEL))


def _normmod(x, nw, shift, scale):
    y = x * lax.rsqrt(jnp.mean(x * x, axis=-1, keepdims=True) + NORM_EPS) * nw
    return y * (1.0 + scale) + shift


def _normproj_kernel(h_ref, mod_ref, nw_ref, w_ref, o_ref, xn_ref):
    @pl.when(pl.program_id(1) == 0)
    def _():
        m = mod_ref[0]
        xn_ref[...] = _normmod(h_ref[...], nw_ref[...], m[:, :D_MODEL], m[:, D_MODEL:2 * D_MODEL]).astype(BF16)
    o_ref[...] = _dot(xn_ref[...], w_ref[0])


def _normproj(h, mod, nw, w, l, tn, name):
    tm = 2 * TB
    n = w.shape[2]
    return pl.pallas_call(
        _normproj_kernel,
        grid=(T_ALL // tm, n // tn),
        in_specs=[pl.BlockSpec((tm, D_MODEL), lambda i, j: (i, 0)),
                  pl.BlockSpec((1, 1, 6 * D_MODEL), lambda i, j: (_blk_type(2 * i), 0, 0)),
                  pl.BlockSpec((1, D_MODEL), lambda i, j: (0, 0)),
                  pl.BlockSpec((1, D_MODEL, tn), lambda i, j: (l, 0, j))],
        out_specs=pl.BlockSpec((tm, tn), lambda i, j: (i, j)),
        out_shape=jax.ShapeDtypeStruct((T_ALL, n), F32),
        scratch_shapes=[pltpu.VMEM((tm, D_MODEL), BF16)],
        compiler_params=_cparams(("parallel", "arbitrary")),
        name=name,
    )(h, mod, nw, w)


def _with_halo(prev_ref, cur_ref, next_ref, i):
    prev = jnp.where(_blk_first(i), 0.0, prev_ref[...])
    nxt = jnp.where(_blk_last(i), 0.0, next_ref[...])
    return jnp.concatenate([prev, cur_ref[...], nxt], axis=0)


def _shifted(ext, s):
    n = ext.shape[0]
    return pltpu.roll(ext, (-s) % n, axis=0)[HALO:HALO + TB]


def _seg_sum(x, width):
    ii = lax.broadcasted_iota(jnp.int32, (LANE, LANE), 0)
    jj = lax.broadcasted_iota(jnp.int32, (LANE, LANE), 1)
    shift = int(math.log2(width))
    ones = ((ii >> shift) == (jj >> shift)).astype(BF16)
    parts = [_dot_exact_rhs(x[:, c:c + LANE], ones) for c in range(0, x.shape[1], LANE)]
    return jnp.concatenate(parts, axis=1)


def _rwkv_prep_kernel(pp_ref, p_ref, pn_ref, mu_ref, w0_ref, w2_ref, a0_ref, a2_ref, g2_ref, kk_w_ref, ka_ref,
                      rk_ref, r_ref, v_ref, kk_ref, g_ref, bonus_ref, lw_ref, kd_ref, b_ref):
    i = pl.program_id(0)
    ext = _with_halo(pp_ref, p_ref, pn_ref, i)
    p = p_ref[...]
    p = p + mu_ref[...] * (0.5 * (_shifted(ext, -1) + _shifted(ext, 1)) - p)
    r, k, v = p[:, 0:BR], p[:, BR:2 * BR], p[:, 2 * BR:3 * BR]
    wd, ad, gd = p[:, 1536:1664], p[:, 1664:1792], p[:, 1792:1920]
    kk = k * kk_w_ref[...]
    kk = kk / jnp.maximum(jnp.sqrt(_seg_sum(kk * kk, HD)), 1e-12)
    r_ref[...] = r
    v_ref[...] = v
    kk_ref[...] = kk
    tw = jnp.tanh(wd)
    g_ref[...] = _dot_split(_sigmoid(gd), g2_ref[...])
    wl = [_dot_split(tw, w2_ref[d]) for d in range(2)]
    al = [_dot_split(ad, a2_ref[d]) for d in range(2)]
    kd_sum = jnp.zeros_like(k)
    for d in range(2):
        lw_ref[d] = -_sigmoid(w0_ref[d:d + 1, :] + wl[d]) * math.exp(-0.5)
        a = _sigmoid(a0_ref[d:d + 1, :] + al[d])
        kd = k * (1.0 + (a - 1.0) * ka_ref[...])
        kd_ref[d] = kd
        b_ref[d] = kk * a
        kd_sum = kd_sum + kd
    bonus_ref[...] = _seg_sum(r * kd_sum * rk_ref[...], HD) * v


def _halo_specs(cols):
    per = TB // HALO
    return [pl.BlockSpec((HALO, cols), lambda i: (jnp.maximum(i * per - 1, 0), 0)),
            pl.BlockSpec((TB, cols), lambda i: (i, 0)),
            pl.BlockSpec((HALO, cols), lambda i: (jnp.minimum((i + 1) * per, T_ALL // HALO - 1), 0))]


def _rwkv_prep(p_rwkv, lp):
    tok = jax.ShapeDtypeStruct((T_ALL, BR), F32)
    tok2 = jax.ShapeDtypeStruct((2, T_ALL, BR), F32)
    spec1 = pl.BlockSpec((TB, BR), lambda i: (i, 0))
    spec2 = pl.BlockSpec((2, TB, BR), lambda i: (0, i, 0))
    return pl.pallas_call(
        _rwkv_prep_kernel,
        grid=(NBLK,),
        in_specs=_halo_specs(RWKV_COLS) + [
            _full((1, RWKV_COLS)), _full((2, BR)), _full((2, LANE, BR)), _full((2, BR)), _full((2, LANE, BR)),
            _full((LANE, BR)), _full((1, BR)), _full((1, BR)), _full((1, BR))],
        out_specs=[spec1, spec1, spec1, spec1, spec1, spec2, spec2, spec2],
        out_shape=[tok, tok, tok, tok, tok, tok2, tok2, tok2],
        compiler_params=_cparams(("parallel",)),
        name="rwkv_prep",
    )(p_rwkv, p_rwkv, p_rwkv, lp['rwkv_mu'], lp['rwkv_w0'], lp['rwkv_w2p'], lp['rwkv_a0'], lp['rwkv_a2p'],
      lp['rwkv_g2'], lp['rwkv_k_k'], lp['rwkv_k_a'], lp['rwkv_r_k'])


def _pair_bd(x):
    h0 = lax.broadcasted_iota(jnp.int32, x.shape, 1) < HD
    return jnp.concatenate([jnp.where(h0, x, 0.0), jnp.where(h0, 0.0, x)], axis=0)


def _rwkv_scan_kernel(r_ref, v_ref, kk_ref, lw_ref, kd_ref, b_ref, s0_ref, y_ref, sf_ref,
                      s_ref, lrhs_ref, ly_ref, tinv_ref, v16_ref, ebt_ref, wc_ref):
    d = pl.program_id(0)
    blk = _scan_blk(d, pl.program_id(1))
    first, last = _scan_ends(d, blk)
    sgn = 1 - 2 * d
    C = RWKV_CHUNK
    nchunk = TB // C
    npair = BR // LANE
    ii = lax.broadcasted_iota(jnp.int32, (LANE, LANE), 0)
    jj = lax.broadcasted_iota(jnp.int32, (LANE, LANE), 1)
    same = (ii >> 6) == (jj >> 6)

    @pl.when(jnp.logical_and(first, blk < N_CTX))
    def _():
        s_ref[...] = jnp.zeros_like(s_ref)

    @pl.when(jnp.logical_and(first, blk >= N_CTX))
    def _():
        for p in range(npair):
            a = jnp.concatenate([s0_ref[0, 0, 2 * p], s0_ref[0, 0, 2 * p + 1]], axis=1)
            s_ref[p] = jnp.where(same, jnp.concatenate([a, a], axis=0).T, 0.0)

    ci = lax.broadcasted_iota(jnp.int32, (C, C), 0)
    cj = lax.broadcasted_iota(jnp.int32, (C, C), 1)
    tri = ((cj - ci) * sgn <= 0).astype(BF16)
    rel = ((jj & (C - 1)) - (ii & (C - 1))) * sgn
    before = jnp.logical_and(same, rel < 0)
    before_incl = jnp.logical_and(same, rel <= 0)

    def rows_of(c):
        cc = c + d * (nchunk - 1 - 2 * c)
        return pl.ds(pl.multiple_of(cc * C, C), C)

    n_of = {}
    for c in range(nchunk):
        rows = rows_of(c)
        lw = lw_ref[0, rows, :]
        cum = _cumsum_rows(tri, lw)
        tot = jnp.sum(lw, axis=0, keepdims=True)
        e_neg = jnp.exp(-cum)
        e_end = jnp.exp(tot - cum)
        r = r_ref[rows, :]
        v = v_ref[rows, :]
        kk = kk_ref[rows, :]
        kd = kd_ref[0, rows, :]
        b = b_ref[0, rows, :]
        k_t, b_t = kd * e_neg, b * e_neg
        kk_h, r_h = kk * jnp.exp(cum - lw), r * jnp.exp(cum)
        k_e, b_e = kd * e_end, b * e_end
        for p in range(npair):
            sl = slice(p * LANE, (p + 1) * LANE)
            KK, R, KT, BT, V, KE, BE = (_pair_bd(x[:, sl]) for x in (kk_h, r_h, k_t, b_t, v, k_e, b_e))
            kk16, r16 = KK.astype(BF16), R.astype(BF16)
            g = _dot_nt(jnp.concatenate([kk16, r16], axis=0),
                        jnp.concatenate([BT, KT], axis=0).astype(BF16))
            n_of[c, p] = jnp.where(before, g[:LANE, :LANE], 0.0)
            a1 = jnp.where(before, g[:LANE, LANE:], 0.0)
            a4 = jnp.where(before_incl, g[LANE:, :LANE], 0.0)
            a3 = jnp.where(before_incl, g[LANE:, LANE:], 0.0)
            lrhs_ref[c, p] = jnp.concatenate([kk16, a1.astype(BF16)], axis=1)
            ly_ref[c, p] = jnp.concatenate([r16, a3.astype(BF16), a4.astype(BF16)], axis=1)
            v16_ref[c, p] = V.astype(BF16)
            ebt_ref[c, p] = jnp.concatenate([KE.T, BE.T], axis=1).astype(BF16)
            wc_ref[c, p] = jnp.exp(jnp.broadcast_to(tot[:, sl], (LANE, LANE)).T)

    chains = list(n_of)
    level = lambda f: {k: f(k) for k in chains}
    eye = (ii == jj).astype(F32)
    blk16 = (ii >> 4) == (jj >> 4)
    dg = level(lambda k: jnp.where(blk16, n_of[k], 0.0))
    low = level(lambda k: n_of[k] - dg[k])
    d2 = level(lambda k: _dot16(dg[k], dg[k]))
    d4 = level(lambda k: _dot16(d2[k], d2[k]))
    x1 = level(lambda k: _dot16(eye - dg[k], eye + d2[k]))
    d8 = level(lambda k: _dot16(d4[k], d4[k]))
    x2 = level(lambda k: _dot16(x1[k], eye + d4[k]))
    xd = level(lambda k: _dot16(x2[k], eye + d8[k]))
    m = level(lambda k: _dot16(xd[k], low[k]))
    m2 = level(lambda k: _dot16(m[k], m[k]))
    t1 = level(lambda k: _dot16(eye - m[k], eye + m2[k]))
    tinv = level(lambda k: _dot16(t1[k], xd[k]))
    for k in chains:
        tinv_ref[k] = tinv[k].astype(BF16)

    pairs = range(npair)
    for c in range(nchunk):
        rows = rows_of(c)
        s = [s_ref[p] for p in pairs]
        s16 = [x.astype(BF16) for x in s]
        v16 = [v16_ref[c, p] for p in pairs]
        rhs = [_dot(lrhs_ref[c, p], jnp.concatenate([s16[p], v16[p]], axis=0)).astype(BF16) for p in pairs]
        u16 = [(-_dot(tinv_ref[c, p], rhs[p])).astype(BF16) for p in pairs]
        for p in pairs:
            s_ref[p] = s[p] * wc_ref[c, p] + _dot(ebt_ref[c, p], jnp.concatenate([v16[p], u16[p]], axis=0))
        for p in pairs:
            y = _dot(ly_ref[c, p], jnp.concatenate([s16[p], v16[p], u16[p]], axis=0))
            y_ref[0, rows, p * LANE:(p + 1) * LANE] = y[:C] + y[C:]

    @pl.when(last)
    def _():
        for p in range(npair):
            sp = s_ref[p]
            folded = sp[:HD] + sp[HD:]
            z = jnp.concatenate([folded, folded], axis=0).T
            sf_ref[0, 0, 2 * p] = z[:HD, :HD]
            sf_ref[0, 0, 2 * p + 1] = z[HD:, :HD]


def _rwkv_scan(r, v, kk, lw, kd, b, s0):
    np_ = BR // LANE
    nchunk = TB // RWKV_CHUNK
    spec1 = pl.BlockSpec((TB, BR), lambda d, i: (_scan_blk(d, i), 0))
    spec2 = pl.BlockSpec((1, TB, BR), lambda d, i: (d, _scan_blk(d, i), 0))
    seq = lambda d, i: _blk_seq(_scan_blk(d, i))
    s0spec = pl.BlockSpec((1, 1, HEADS, HD, HD), lambda d, i: (_lat_of_seq(seq(d, i)), d, 0, 0, 0))
    sfspec = pl.BlockSpec((1, 1, HEADS, HD, HD), lambda d, i: (seq(d, i), d, 0, 0, 0))
    per = lambda rows, cols, dt: pltpu.VMEM((nchunk, np_, rows, cols), dt)
    return pl.pallas_call(
        _rwkv_scan_kernel,
        grid=(2, NBLK),
        in_specs=[spec1, spec1, spec1, spec2, spec2, spec2, s0spec],
        out_specs=[spec2, sfspec],
        out_shape=[jax.ShapeDtypeStruct((2, T_ALL, BR), F32),
                   jax.ShapeDtypeStruct((N_SEQ, 2, HEADS, HD, HD), F32)],
        scratch_shapes=[pltpu.VMEM((np_, LANE, LANE), F32),
                        per(LANE, 2 * LANE, BF16), per(LANE, 3 * LANE, BF16), per(LANE, LANE, BF16),
                        per(LANE, LANE, BF16), per(LANE, 2 * LANE, BF16), per(LANE, LANE, F32)],
        compiler_params=_cparams(("arbitrary", "arbitrary")),
        name="rwkv_scan",
    )(r, v, kk, lw, kd, b, s0)


def _ssd_prep_kernel(pp_ref, p_ref, pn_ref, cw_ref, cb_ref, dtb_ref, x_ref, bc_ref, dt_ref):
    i = pl.program_id(0)
    ext = _with_halo(pp_ref, p_ref, pn_ref, i)[:, BR:BR + SSD_XBC]
    acc = cb_ref[...] + cw_ref[0:1, :] * _shifted(ext, -(SSD_CONV // 2))
    for j in range(1, SSD_CONV):
        acc = acc + cw_ref[j:j + 1, :] * _shifted(ext, j - SSD_CONV // 2)
    xbc = _silu(acc)
    x_ref[...] = xbc[:, :BR]
    bc_ref[...] = xbc[:, BR:]
    dt_ref[...] = _softplus(p_ref[:, BR + SSD_XBC:] + dtb_ref[...])


def _ssd_prep(p_ssd, lp):
    return pl.pallas_call(
        _ssd_prep_kernel,
        grid=(NBLK,),
        in_specs=_halo_specs(SSD_PCOLS) + [_full((8, SSD_XBC)), _full((1, SSD_XBC)), _full((1, 2 * LANE))],
        out_specs=[pl.BlockSpec((TB, BR), lambda i: (i, 0)),
                   pl.BlockSpec((TB, 2 * LANE), lambda i: (i, 0)),
                   pl.BlockSpec((TB, 2 * LANE), lambda i: (i, 0))],
        out_shape=[jax.ShapeDtypeStruct((T_ALL, BR), F32),
                   jax.ShapeDtypeStruct((T_ALL, 2 * LANE), F32),
                   jax.ShapeDtypeStruct((T_ALL, 2 * LANE), F32)],
        compiler_params=_cparams(("parallel",)),
        name="ssd_prep",
    )(p_ssd, p_ssd, p_ssd, lp['ssd_conv_w8'], lp['ssd_conv_b'], lp['ssd_dt_bias_p'])


def _ssd_scan_kernel(x_ref, bc_ref, dt_ref, alog_ref, h0_ref, y_ref, hf_ref, hs_ref):
    d = pl.program_id(0)
    blk = _scan_blk(d, pl.program_id(1))
    first, last = _scan_ends(d, blk)
    sgn = 1 - 2 * d
    npair = BR // LANE
    half64 = lax.broadcasted_iota(jnp.int32, (HD, LANE), 1) < HD

    @pl.when(jnp.logical_and(first, blk < N_CTX))
    def _():
        hs_ref[...] = jnp.zeros_like(hs_ref)

    @pl.when(jnp.logical_and(first, blk >= N_CTX))
    def _():
        zero = jnp.zeros((HD, LANE), F32)
        blocks = []
        for q in range(npair):
            a = jnp.concatenate([h0_ref[0, 0, 2 * q], h0_ref[0, 0, 2 * q + 1]], axis=1)
            t = jnp.concatenate([a, a], axis=0).T
            blocks.append(jnp.where(half64, t[:HD], t[HD:]))
        rows = [jnp.concatenate([blocks[q] if q // 2 == g else zero for q in range(npair)], axis=1)
                for g in range(2)]
        hs_ref[...] = jnp.concatenate(rows, axis=0)

    x = x_ref[...]
    bm = bc_ref[:, :LANE]
    cm = bc_ref[:, LANE:]
    dt = dt_ref[...]
    a_neg = -jnp.exp(alog_ref[0])
    dta = dt * a_neg
    qi = lax.broadcasted_iota(jnp.int32, (TB, TB), 0)
    qj = lax.broadcasted_iota(jnp.int32, (TB, TB), 1)
    before_incl = (qj - qi) * sgn <= 0
    a_cum = _cumsum_rows(before_incl.astype(BF16), dta)
    tot = jnp.sum(dta, axis=0, keepdims=True)
    a_cum_t = a_cum.T
    dt_t = dt.T
    eh = lax.broadcasted_iota(jnp.int32, (LANE, BR), 0)
    ec = lax.broadcasted_iota(jnp.int32, (LANE, BR), 1)
    expand = ((ec >> 6) == eh).astype(BF16)
    e_in = _dot_exact_rhs(jnp.exp(a_cum), expand)
    to_end = _dot_exact_rhs(jnp.exp(tot - a_cum) * dt, expand)
    dec = _dot_exact_rhs(jnp.broadcast_to(jnp.exp(tot), (8, LANE)), expand)[0:1]
    hs = hs_ref[...]
    cb16, bb16, xb16 = cm.astype(BF16), bm.astype(BF16), x.astype(BF16)
    glane = lax.broadcasted_iota(jnp.int32, (TB, LANE), 1) >> 6
    half = lax.broadcasted_iota(jnp.int32, (TB, LANE), 1) < HD
    cbg = [_dot_nt(jnp.where(glane == g, cm, 0.0).astype(BF16), bb16) for g in range(2)]
    y_off = _dot(cb16, hs.astype(BF16)) * e_in
    upd = _dot_tn(bb16, (x * to_end).astype(BF16))
    scores = []
    for h in range(HEADS):
        seg = a_cum[:, h:h + 1] - a_cum_t[h:h + 1, :]
        decay = jnp.exp(jnp.where(before_incl, seg, -jnp.inf))
        scores.append((cbg[h // 4] * decay * dt_t[h:h + 1, :]).astype(BF16))
    y_heads = [_dot(scores[h], xb16[:, (h // 2) * LANE:(h // 2 + 1) * LANE]) for h in range(HEADS)]
    y_parts = [jnp.where(half, y_heads[2 * q], y_heads[2 * q + 1]) for q in range(npair)]
    y_ref[0] = jnp.concatenate(y_parts, axis=1) + y_off
    ui = lax.broadcasted_iota(jnp.int32, (LANE, BR), 0)
    uj = lax.broadcasted_iota(jnp.int32, (LANE, BR), 1)
    hs_ref[...] = hs * dec + jnp.where((ui >> 6) == (uj >> 8), upd, 0.0)

    @pl.when(last)
    def _():
        hn = hs_ref[...]
        for q in range(npair):
            g = q // 2
            w = hn[g * HD:(g + 1) * HD, q * LANE:(q + 1) * LANE]
            z = jnp.concatenate([w, w], axis=0).T
            hf_ref[0, 0, 2 * q] = z[:HD, :HD]
            hf_ref[0, 0, 2 * q + 1] = z[HD:, :HD]


def _ssd_scan(x, bc, dt, alog, h0):
    seq = lambda d, i: _blk_seq(_scan_blk(d, i))
    return pl.pallas_call(
        _ssd_scan_kernel,
        grid=(2, NBLK),
        in_specs=[pl.BlockSpec((TB, BR), lambda d, i: (_scan_blk(d, i), 0)),
                  pl.BlockSpec((TB, 2 * LANE), lambda d, i: (_scan_blk(d, i), 0)),
                  pl.BlockSpec((TB, LANE), lambda d, i: (_scan_blk(d, i), d)),
                  pl.BlockSpec((1, 1, LANE), lambda d, i: (d, 0, 0)),
                  pl.BlockSpec((1, 1, HEADS, HD, HD), lambda d, i: (_lat_of_seq(seq(d, i)), d, 0, 0, 0))],
        out_specs=[pl.BlockSpec((1, TB, BR), lambda d, i: (d, _scan_blk(d, i), 0)),
                   pl.BlockSpec((1, 1, HEADS, HD, HD), lambda d, i: (seq(d, i), d, 0, 0, 0))],
        out_shape=[jax.ShapeDtypeStruct((2, T_ALL, BR), F32),
                   jax.ShapeDtypeStruct((N_SEQ, 2, HEADS, HD, HD), F32)],
        scratch_shapes=[pltpu.VMEM((LANE, BR), F32)],
        compiler_params=_cparams(("arbitrary", "arbitrary")),
        name="ssd_scan",
    )(x, bc, dt, alog, h0)


def _gqa(q, kd, vd, sink_ref, bias):
    nq = q.shape[0]
    log2e = math.log2(math.e)
    half = lax.broadcasted_iota(jnp.int32, (nq, LANE), 1) < HD
    q = q * (ATT_SCALE * log2e)
    scores = []
    for g in range(ATT_KVH):
        rows = []
        for j in (2 * g, 2 * g + 1):
            q2 = q[:, j * LANE:(j + 1) * LANE]
            rows += [jnp.where(half, q2, 0.0), jnp.where(half, 0.0, q2)]
        qg = jnp.concatenate(rows, axis=0).astype(BF16)
        scores.append(_dot_nt(qg, kd[g].astype(BF16)))
    outs = []
    for g in range(ATT_KVH):
        s = scores[g]
        if bias is not None:
            kb = bias.shape[1]
            s = jnp.concatenate([s[:, :kb] + jnp.concatenate([bias] * ATT_GROUP, axis=0), s[:, kb:]], axis=1)
        sink = jnp.concatenate(
            [jnp.full((nq, 1), sink_ref[ATT_GROUP * g + h] * log2e, F32) for h in range(ATT_GROUP)], axis=0)
        mx = jnp.maximum(jnp.max(s, axis=-1, keepdims=True), sink)
        e = jnp.exp2(s - mx)
        inv = 1.0 / (jnp.sum(e, axis=-1, keepdims=True) + jnp.exp2(sink - mx))
        o = _dot(e.astype(BF16), vd[g].astype(BF16)) * inv
        outs += [jnp.where(half, o[0:nq], o[nq:2 * nq]), jnp.where(half, o[2 * nq:3 * nq], o[3 * nq:])]
    return jnp.concatenate(outs, axis=1)


def _ctx_attn_kernel(sink_ref, p_ref, o_ref):
    kd = [p_ref[:, BR + g * LANE:BR + (g + 1) * LANE] for g in range(ATT_KVH)]
    vd = [p_ref[:, BR + (2 + g) * LANE:BR + (3 + g) * LANE] for g in range(ATT_KVH)]
    o_ref[...] = _gqa(p_ref[:, :BR], kd, vd, sink_ref, None)


def _ctx_attn(p_att, sink):
    return pl.pallas_call(
        _ctx_attn_kernel,
        grid=(N_CTX,),
        in_specs=[pl.BlockSpec(memory_space=pltpu.SMEM), pl.BlockSpec((TB, ATT_PCOLS), lambda i: (i, 0))],
        out_specs=pl.BlockSpec((TB, BR), lambda i: (i, 0)),
        out_shape=jax.ShapeDtypeStruct((T_CTX, BR), F32),
        compiler_params=_cparams(("parallel",)),
        name="ctx_attn",
    )(sink, p_att)


def _rope(x, cos, sin_signed):
    lanes = x.shape[1]
    reps = lanes // LANE
    if reps > 1:
        cos = jnp.concatenate([cos] * reps, axis=1)
        sin_signed = jnp.concatenate([sin_signed] * reps, axis=1)
    lo = (lax.broadcasted_iota(jnp.int32, x.shape, 1) & 31) < 16
    partner = jnp.where(lo, pltpu.roll(x, lanes - 16, axis=1), pltpu.roll(x, 16, axis=1))
    return x * cos + partner * sin_signed


def _lat_attn_kernel(sink_ref, pq_ref, pp_ref, pn_ref, cq_ref, sq_ref, cp_ref, sp_ref, cn_ref, sn_ref,
                     kc_ref, vc_ref, o_ref):
    j = pl.program_id(1)
    hw = ATT_WINDOW
    q = _rope(pq_ref[:, :BR], cq_ref[...], sq_ref[...])
    kd, vd = [], []
    for g in range(ATT_KVH):
        kc, vc = BR + g * LANE, BR + (2 + g) * LANE
        kh, vh = g * LANE, (2 + g) * LANE
        kd.append(jnp.concatenate([_rope(pp_ref[:, kh:kh + LANE], cp_ref[...], sp_ref[...]),
                                   _rope(pq_ref[:, kc:kc + LANE], cq_ref[...], sq_ref[...]),
                                   _rope(pn_ref[:, kh:kh + LANE], cn_ref[...], sn_ref[...]),
                                   kc_ref[0, g]], axis=0))
        vd.append(jnp.concatenate([pp_ref[:, vh:vh + LANE], pq_ref[:, vc:vc + LANE], pn_ref[:, vh:vh + LANE],
                                   vc_ref[0, g]], axis=0))
    nloc = TB + 2 * hw
    qi = lax.broadcasted_iota(jnp.int32, (TB, nloc), 0)
    kj = lax.broadcasted_iota(jnp.int32, (TB, nloc), 1)
    rel = kj - hw - qi
    kpos = j * TB - hw + kj
    valid = (rel <= ATT_WINDOW) & (rel >= -ATT_WINDOW) & (kpos >= 0) & (kpos < LAT_LEN)
    o_ref[...] = _gqa(q, kd, vd, sink_ref, jnp.where(valid, 0.0, -jnp.inf))


def _lat_attn(p_att, sink, rope_cos, rope_sin, k_ctx, v_ctx):
    nb = LAT_LEN // TB
    hw = ATT_WINDOW
    per = TB // hw
    qrow = lambda b, j: N_CTX + b * nb + j
    hrow = lambda b, j: per * (N_CTX + b * nb)
    prv = lambda j: jnp.maximum(per * j - 1, 0)
    nxt = lambda j: jnp.minimum(per * (j + 1), per * nb - 1)
    hspec = lambda f: pl.BlockSpec((hw, 4 * LANE), lambda b, j: (hrow(b, j) + f(j), 1))
    tspec = lambda rows, f: pl.BlockSpec((rows, LANE), lambda b, j: (f(j), 0))
    cspec = pl.BlockSpec((1, ATT_KVH, PAST_LEN, LANE), lambda b, j: (b, 0, 0, 0))
    return pl.pallas_call(
        _lat_attn_kernel,
        grid=(N_LAT, nb),
        in_specs=[pl.BlockSpec(memory_space=pltpu.SMEM),
                  pl.BlockSpec((TB, ATT_PCOLS), lambda b, j: (qrow(b, j), 0)), hspec(prv), hspec(nxt),
                  tspec(TB, lambda j: j), tspec(TB, lambda j: j), tspec(hw, prv), tspec(hw, prv),
                  tspec(hw, nxt), tspec(hw, nxt), cspec, cspec],
        out_specs=pl.BlockSpec((TB, BR), lambda b, j: (b * nb + j, 0)),
        out_shape=jax.ShapeDtypeStruct((N_LAT * LAT_LEN, BR), F32),
        compiler_params=_cparams(("parallel", "parallel")),
        name="lat_attn",
    )(sink, p_att, p_att, p_att, rope_cos, rope_sin, rope_cos, rope_sin, rope_cos, rope_sin, k_ctx, v_ctx)


def _merge_kernel(h_ref, mod_ref, nw_ref, yr_ref, bonus_ref, g_ref, lnw_ref, lnb_ref, ys_ref, xs_ref, z_ref,
                  dvec_ref, snw_ref, yac_ref, yal_ref, wg_ref, wb_ref, wo_ref, o_ref):
    m = mod_ref[0]
    h = h_ref[...]
    pg = _dot(_normmod(h, nw_ref[...], m[:, :D_MODEL], m[:, D_MODEL:2 * D_MODEL]).astype(BF16), wg_ref[0])
    y = yr_ref[0] + yr_ref[1] + bonus_ref[...]
    mu = _seg_sum(y, HD) * (1.0 / HD)
    yc = y - mu
    var = _seg_sum(yc * yc, HD) * (1.0 / HD)
    y_a = (yc * lax.rsqrt(var + RWKV_GN_EPS) * lnw_ref[...] + lnb_ref[...]) * g_ref[...]
    y = (ys_ref[0] + ys_ref[1] + dvec_ref[...] * xs_ref[...]) * _silu(z_ref[...])
    y_b = y * lax.rsqrt(jnp.mean(y * y, axis=-1, keepdims=True) + NORM_EPS) * snw_ref[...]
    y_c = jnp.where(pl.program_id(0) < N_CTX, yac_ref[...], yal_ref[...])
    wide = [_dot(br.astype(BF16), wb_ref[0, n]) for n, br in enumerate((y_a, y_b, y_c))]
    merged = None
    for n in range(3):
        term = _sigmoid(pg[:, n * D_MODEL:(n + 1) * D_MODEL]) * wide[n]
        merged = term if merged is None else merged + term
    g1 = m[:, 2 * D_MODEL:3 * D_MODEL]
    o_ref[...] = h + g1 * _dot(merged.astype(BF16), wo_ref[0])


def _merge(h, mod, yr, bonus, g, ys, xs, p_ssd, ya_ctx, ya_lat, lp, w_gate, w_branch, w_out, l):
    tm = TB
    row = lambda w: pl.BlockSpec((tm, w), lambda i: (i, 0))
    row2 = pl.BlockSpec((2, tm, BR), lambda i: (0, i, 0))
    return pl.pallas_call(
        _merge_kernel,
        grid=(T_ALL // tm,),
        in_specs=[row(D_MODEL), pl.BlockSpec((1, 1, 6 * D_MODEL), lambda i: (_blk_type(i), 0, 0)),
                  _full((1, D_MODEL)),
                  row2, row(BR), row(BR), _full((1, BR)), _full((1, BR)),
                  row2, row(BR), row(BR), _full((1, BR)), _full((1, BR)),
                  pl.BlockSpec((tm, BR), lambda i: (jnp.minimum(i, N_CTX - 1), 0)),
                  pl.BlockSpec((tm, BR), lambda i: (jnp.maximum(i - N_CTX, 0), 0)),
                  pl.BlockSpec((1, D_MODEL, GATE_COLS), lambda i: (l, 0, 0)),
                  pl.BlockSpec((1, 3, BR, D_MODEL), lambda i: (l, 0, 0, 0)),
                  pl.BlockSpec((1, D_MODEL, D_MODEL), lambda i: (l, 0, 0))],
        out_specs=row(D_MODEL),
        out_shape=jax.ShapeDtypeStruct((T_ALL, D_MODEL), F32),
        compiler_params=_cparams(("parallel",)),
        name="merge",
    )(h, mod, lp['norm1_w'], yr, bonus, g, lp['rwkv_lnx_w'], lp['rwkv_lnx_b'], ys, xs, p_ssd, lp['ssd_d_vec'],
      lp['ssd_norm_w'], ya_ctx, ya_lat, w_gate, w_branch, w_out)


def _route_kernel(h_ref, mod_ref, nw_ref, wr_ref, xn_ref, gate_ref):
    m = mod_ref[0]
    xn = _normmod(h_ref[...], nw_ref[...], m[:, 3 * D_MODEL:4 * D_MODEL], m[:, 4 * D_MODEL:5 * D_MODEL])
    xn_ref[...] = xn.astype(BF16)
    logits = _dot_split(xn, wr_ref[...])
    lane_i = lax.broadcasted_iota(jnp.int32, logits.shape, 1)
    lane = lane_i.astype(F32)
    lane_grp = (lane_i >> 3).astype(F32)
    neg = -jnp.inf
    big = float(LANE)
    is_g = (lane_i >= MOE_EXPERTS) & (lane_i < MOE_EXPERTS + MOE_GROUPS)
    gl = jnp.where(is_g, logits, neg)
    gmax = jnp.max(gl, axis=-1, keepdims=True)
    gsel = jnp.min(jnp.where(gl == gmax, lane - MOE_EXPERTS, big), axis=-1, keepdims=True)
    g_w = 1.0 / jnp.sum(jnp.where(is_g, jnp.exp(gl - gmax), 0.0), axis=-1, keepdims=True)
    el = jnp.where((lane_i < MOE_EXPERTS) & (lane_grp == gsel), logits, neg)
    m1 = jnp.max(el, axis=-1, keepdims=True)
    i1 = jnp.min(jnp.where(el == m1, lane, big), axis=-1, keepdims=True)
    el2 = jnp.where(lane == i1, neg, el)
    m2 = jnp.max(el2, axis=-1, keepdims=True)
    i2 = jnp.min(jnp.where(el2 == m2, lane, big), axis=-1, keepdims=True)
    e2 = jnp.exp(m2 - m1)
    w1 = 1.0 / (1.0 + e2)
    gate_ref[...] = jnp.where(lane == i1, w1 * g_w, jnp.where(lane == i2, e2 * w1 * g_w, 0.0))


def _route(h, mod, nw, wr):
    tm = TB
    return pl.pallas_call(
        _route_kernel,
        grid=(T_ALL // tm,),
        in_specs=[pl.BlockSpec((tm, D_MODEL), lambda i: (i, 0)),
                  pl.BlockSpec((1, 1, 6 * D_MODEL), lambda i: (_blk_type(i), 0, 0)),
                  _full((1, D_MODEL)), _full((D_MODEL, LANE))],
        out_specs=[pl.BlockSpec((tm, D_MODEL), lambda i: (i, 0)), pl.BlockSpec((tm, LANE), lambda i: (i, 0))],
        out_shape=[jax.ShapeDtypeStruct((T_ALL, D_MODEL), BF16), jax.ShapeDtypeStruct((T_ALL, LANE), F32)],
        compiler_params=_cparams(("parallel",)),
        name="moe_route",
    )(h, mod, nw, wr)


MOE_EB = 4


def _experts_kernel(xn_ref, gate_ref, w1_ref, w3_ref, w2_ref, h_ref, mod_ref, o_ref, acc_ref):
    eb = pl.program_id(1)

    @pl.when(eb == 0)
    def _():
        acc_ref[...] = jnp.zeros_like(acc_ref)

    gate = gate_ref[...]
    lane = lax.broadcasted_iota(jnp.int32, gate.shape, 1)
    xn = xn_ref[...]
    h1 = [_dot(xn, w1_ref[0, k]) for k in range(MOE_EB)]
    h3 = [_dot(xn, w3_ref[0, k]) for k in range(MOE_EB)]
    acts = []
    for k in range(MOE_EB):
        gcol = jnp.sum(jnp.where(lane == eb * MOE_EB + k, gate, 0.0), axis=-1, keepdims=True)
        acts.append((_silu(h1[k]) * h3[k] * gcol).astype(BF16))
    acc_ref[...] += _dot(jnp.concatenate(acts, axis=1), w2_ref[0].reshape(MOE_EB * D_EXPERT, D_MODEL))

    @pl.when(eb == MOE_EXPERTS // MOE_EB - 1)
    def _():
        g2 = mod_ref[0][:, 5 * D_MODEL:6 * D_MODEL]
        o_ref[...] = h_ref[...] + g2 * acc_ref[...]


def _experts(xn, gate, w1, w3, w2, h, mod, l):
    tm = 4 * TB
    return pl.pallas_call(
        _experts_kernel,
        grid=(T_ALL // tm, MOE_EXPERTS // MOE_EB),
        in_specs=[pl.BlockSpec((tm, D_MODEL), lambda i, e: (i, 0)),
                  pl.BlockSpec((tm, LANE), lambda i, e: (i, 0)),
                  pl.BlockSpec((1, MOE_EB, D_MODEL, D_EXPERT), lambda i, e: (l, e, 0, 0)),
                  pl.BlockSpec((1, MOE_EB, D_MODEL, D_EXPERT), lambda i, e: (l, e, 0, 0)),
                  pl.BlockSpec((1, MOE_EB, D_EXPERT, D_MODEL), lambda i, e: (l, e, 0, 0)),
                  pl.BlockSpec((tm, D_MODEL), lambda i, e: (i, 0)),
                  pl.BlockSpec((1, 1, 6 * D_MODEL), lambda i, e: (_blk_type(4 * i), 0, 0))],
        out_specs=pl.BlockSpec((tm, D_MODEL), lambda i, e: (i, 0)),
        out_shape=jax.ShapeDtypeStruct((T_ALL, D_MODEL), F32),
        scratch_shapes=[pltpu.VMEM((tm, D_MODEL), F32)],
        compiler_params=_cparams(("parallel", "arbitrary")),
        name="moe_experts",
    )(xn, gate, w1, w3, w2, h, mod)


def _final_norm_kernel(h_ref, w_ref, o_ref):
    x = h_ref[...]
    o_ref[...] = x * lax.rsqrt(jnp.mean(x * x, axis=-1, keepdims=True) + NORM_EPS) * w_ref[...]


def _final_norm(h, w, row0, rows, name):
    tm = 2 * TB
    return pl.pallas_call(
        _final_norm_kernel,
        grid=(rows // tm,),
        in_specs=[pl.BlockSpec((tm, D_MODEL), lambda i: (row0 // tm + i, 0)), _full((1, D_MODEL))],
        out_specs=pl.BlockSpec((tm, D_MODEL), lambda i: (i, 0)),
        out_shape=jax.ShapeDtypeStruct((rows, D_MODEL), F32),
        compiler_params=_cparams(("parallel",)),
        name=name,
    )(h, w)


def _rope_tables():
    pos = jnp.arange(LAT_LEN)
    row = (pos // GRID_W).astype(F32)
    col = (pos % GRID_W).astype(F32)
    half = HD // 2
    inv = 1.0 / (ROPE_BASE ** (jnp.arange(0, half, 2, dtype=F32) / half))
    ar, ac = row[:, None] * inv[None, :], col[:, None] * inv[None, :]
    cos = jnp.concatenate([jnp.cos(ar), jnp.cos(ar), jnp.cos(ac), jnp.cos(ac)], axis=1)
    sin = jnp.concatenate([-jnp.sin(ar), jnp.sin(ar), -jnp.sin(ac), jnp.sin(ac)], axis=1)
    return jnp.tile(cos, (1, 2)), jnp.tile(sin, (1, 2))


def _dup_heads(x):
    x = jnp.swapaxes(x, -2, -3)
    return jnp.concatenate([x, x], axis=-1)


def _proj_weights(w_in):
    o_ssd, o_att = RWKV_COLS, RWKV_COLS + 1296
    o_gate = o_att + BR + 2 * ATT_KVH * HD
    pad = jnp.zeros((DEPTH, D_MODEL, LANE - HEADS), F32)
    dt0 = o_ssd + BR + SSD_XBC
    w_ssd = jnp.concatenate([w_in[:, :, o_ssd:dt0], w_in[:, :, dt0:dt0 + HEADS], pad,
                             w_in[:, :, dt0 + HEADS:dt0 + 2 * HEADS], pad], axis=2)
    kv = [w_in[:, :, o_att + BR + n * HD:o_att + BR + (n + 1) * HD] for n in range(2 * ATT_KVH)]
    w_att = jnp.concatenate([w_in[:, :, o_att:o_att + BR]] + [x for c in kv for x in (c, c)], axis=2)
    return (w_in[:, :, :o_ssd].astype(BF16), w_ssd.astype(BF16), w_att.astype(BF16),
            w_in[:, :, o_gate:].astype(BF16))


def _layer_params(l, prm):
    lp = {}
    row = lambda name: prm[name][l].reshape(1, -1)
    for name in ('norm1_w', 'norm2_w', 'rwkv_mu', 'rwkv_k_k', 'rwkv_k_a', 'rwkv_r_k', 'rwkv_lnx_w', 'rwkv_lnx_b',
                 'ssd_conv_b', 'ssd_norm_w'):
        lp[name] = row(name)
    lp['rwkv_w0'] = prm['rwkv_w0'][l]
    lp['rwkv_a0'] = prm['rwkv_a0'][l]
    z64 = jnp.zeros((HD, BR), F32)
    w2, a2 = prm['rwkv_w2'][l], prm['rwkv_a2'][l]
    lp['rwkv_w2p'] = jnp.stack([jnp.concatenate([w2[0], z64]), jnp.concatenate([z64, w2[1]])])
    lp['rwkv_a2p'] = jnp.stack([jnp.concatenate([a2[0], z64]), jnp.concatenate([z64, a2[1]])])
    lp['rwkv_g2'] = prm['rwkv_g2'][l]
    lp['ssd_conv_w8'] = jnp.concatenate([prm['ssd_conv_w'][l], jnp.zeros((8 - SSD_CONV, SSD_XBC), F32)])
    dtb = prm['ssd_dt_bias'][l]
    zp = jnp.zeros((LANE - HEADS,), F32)
    lp['ssd_dt_bias_p'] = jnp.concatenate([dtb[0], zp, dtb[1], zp]).reshape(1, 2 * LANE)
    lp['ssd_a_log_p'] = jnp.pad(prm['ssd_a_log'][l], ((0, 0), (0, LANE - HEADS))).reshape(2, 1, LANE)
    lp['ssd_d_vec'] = jnp.repeat(prm['ssd_d'][l], HD).reshape(1, BR)
    lp['att_sink'] = prm['att_sink'][l]
    wr = jnp.concatenate([prm['moe_w_expert'][l].reshape(D_MODEL, MOE_EXPERTS), prm['moe_w_group'][l]], axis=1)
    lp['w_route'] = jnp.pad(wr, ((0, 0), (0, LANE - MOE_EXPERTS - MOE_GROUPS)))
    return lp


def kernel(x_prompt, x_sample, cache_attn_k, cache_attn_v, state_rwkv, state_ssd, c, c_ctx, w_ada, b_ada, norm1_w, norm2_w, w_in, rwkv_mu, rwkv_w0, rwkv_w2, rwkv_a0, rwkv_a2, rwkv_g2, rwkv_k_k, rwkv_k_a, rwkv_r_k, rwkv_lnx_w, rwkv_lnx_b, ssd_conv_w, ssd_conv_b, ssd_dt_bias, ssd_a_log, ssd_d, ssd_norm_w, att_sink, w_branch, w_out, moe_w_group, moe_w_expert, moe_w1, moe_w3, moe_w2, final_norm_w):
    prm = dict(norm1_w=norm1_w, norm2_w=norm2_w, rwkv_mu=rwkv_mu, rwkv_w0=rwkv_w0, rwkv_w2=rwkv_w2,
               rwkv_a0=rwkv_a0, rwkv_a2=rwkv_a2, rwkv_g2=rwkv_g2, rwkv_k_k=rwkv_k_k, rwkv_k_a=rwkv_k_a,
               rwkv_r_k=rwkv_r_k.reshape(DEPTH, BR), rwkv_lnx_w=rwkv_lnx_w, rwkv_lnx_b=rwkv_lnx_b,
               ssd_conv_w=ssd_conv_w, ssd_conv_b=ssd_conv_b, ssd_dt_bias=ssd_dt_bias, ssd_a_log=ssd_a_log,
               ssd_d=ssd_d, ssd_norm_w=ssd_norm_w, att_sink=att_sink,
               moe_w_group=moe_w_group, moe_w_expert=moe_w_expert)
    w_rwkv, w_ssd, w_att, w_gate = _proj_weights(w_in)
    wb16, wo16 = w_branch.astype(BF16), w_out.astype(BF16)
    w1_16, w3_16, w2_16 = moe_w1.astype(BF16), moe_w3.astype(BF16), moe_w2.astype(BF16)
    kc_dup, vc_dup = _dup_heads(cache_attn_k), _dup_heads(cache_attn_v)
    cond8 = jnp.concatenate([c_ctx[None, :], c, jnp.zeros((8 - 1 - N_LAT, D_MODEL), F32)], axis=0)
    mod_all = _adaln(cond8, w_ada, b_ada)
    rope_cos, rope_sin = _rope_tables()
    h = jnp.concatenate([x_prompt.reshape(T_CTX, D_MODEL), x_sample.reshape(N_LAT * LAT_LEN, D_MODEL)], axis=0)
    ks, vs, srs, sss = [], [], [], []
    for l in range(DEPTH):
        lp = _layer_params(l, prm)
        mod = mod_all[l].reshape(8, 1, 6 * D_MODEL)
        p_rwkv = _normproj(h, mod, lp['norm1_w'], w_rwkv, l, RWKV_COLS, "proj_rwkv")
        p_ssd = _normproj(h, mod, lp['norm1_w'], w_ssd, l, SSD_PCOLS, "proj_ssd")
        p_att = _normproj(h, mod, lp['norm1_w'], w_att, l, ATT_PCOLS, "proj_att")
        r, v, kk, g, bonus, lw, kd, b = _rwkv_prep(p_rwkv, lp)
        yr, sf = _rwkv_scan(r, v, kk, lw, kd, b, state_rwkv[:, l])
        srs.append(sf)
        xs, bc, dt = _ssd_prep(p_ssd, lp)
        ys, hf = _ssd_scan(xs, bc, dt, lp['ssd_a_log_p'], state_ssd[:, l])
        sss.append(hf)
        ya_ctx = _ctx_attn(p_att, lp['att_sink'])
        ya_lat = _lat_attn(p_att, lp['att_sink'], rope_cos, rope_sin, kc_dup[:, l], vc_dup[:, l])
        kv = [p_att[:T_CTX, BR + n * LANE:BR + n * LANE + HD].reshape(N_CTX, TB, HD) for n in range(2 * ATT_KVH)]
        ks.append(jnp.stack(kv[:ATT_KVH], axis=2))
        vs.append(jnp.stack(kv[ATT_KVH:], axis=2))
        h = _merge(h, mod, yr, bonus, g, ys, xs, p_ssd, ya_ctx, ya_lat, lp, w_gate, wb16, wo16, l)
        xn, gate = _route(h, mod, lp['norm2_w'], lp['w_route'])
        h = _experts(xn, gate, w1_16, w3_16, w2_16, h, mod, l)
    fw = final_norm_w.reshape(1, D_MODEL)
    y_ctx = _final_norm(h, fw, 0, T_CTX, "final_norm_ctx")
    y_lat = _final_norm(h, fw, T_CTX, N_LAT * LAT_LEN, "final_norm_lat")
    return (y_ctx.reshape(N_CTX, TB, D_MODEL), y_lat.reshape(N_LAT, LAT_LEN, D_MODEL),
            jnp.stack(ks, axis=1), jnp.stack(vs, axis=1),
            jnp.stack(srs, axis=1)[:N_CTX], jnp.stack(sss, axis=1)[:N_CTX])
```

```python
import math

import jax
import jax.numpy as jnp
from jax import lax
from jax.experimental import pallas as pl
from jax.experimental.pallas import tpu as pltpu

F32 = jnp.float32
BF16 = jnp.bfloat16
HIGHEST = lax.Precision.HIGHEST

D_MODEL = 1024
DEPTH = 4
N_CTX = 16
N_LAT = 2
LAT_LEN = 4096
PAST_LEN = 512
GRID_W = 64
NORM_EPS = 1e-6
TB = 256
NLB = LAT_LEN // TB
NBLK = N_CTX + N_LAT * NLB
T_CTX = N_CTX * TB
T_ALL = NBLK * TB
N_SEQ = N_CTX + N_LAT
HALO = 8

HEADS = 8
HD = 64
BR = HEADS * HD
RWKV_COLS = 1920
RWKV_CHUNK = 64
RWKV_GN_EPS = 64e-5
SSD_CONV = 5
SSD_XBC = 768
SSD_PCOLS = 1536
ATT_KVH = 2
ATT_GROUP = 4
ATT_PCOLS = BR + 4 * 2 * HD
ATT_WINDOW = 128
ATT_SCALE = HD ** -0.5
ROPE_BASE = 10000.0
GATE_COLS = 3 * D_MODEL
MOE_GROUPS = 4
MOE_EXPERTS = 32
D_EXPERT = 256
LANE = 128

VMEM_LIMIT = 48 * 1024 * 1024


def _cparams(sem):
    return pltpu.CompilerParams(dimension_semantics=sem, vmem_limit_bytes=VMEM_LIMIT)


def _dot(a, b, precision=None):
    return jnp.dot(a, b, precision=precision, preferred_element_type=F32)


def _dot_nt(a, b):
    return lax.dot_general(a, b, (((1,), (1,)), ((), ())), preferred_element_type=F32)


def _dot_tn(a, b):
    return lax.dot_general(a, b, (((0,), (0,)), ((), ())), preferred_element_type=F32)


def _split16(x):
    hi = x.astype(BF16)
    return hi, (x - hi.astype(F32)).astype(BF16)


def _dot16(a, b):
    return _dot(a.astype(BF16), b.astype(BF16))


def _dot_split(a, b):
    ah, al = _split16(a)
    bh, bl = _split16(b)
    return _dot(jnp.concatenate([ah, al, ah], axis=1), jnp.concatenate([bh, bh, bl], axis=0))


def _dot_exact_rhs(a, b16):
    ah, al = _split16(a)
    return _dot(jnp.concatenate([ah, al], axis=1), jnp.concatenate([b16, b16], axis=0))


def _cumsum_rows(tri16, x):
    x0 = x.astype(BF16)
    r1 = x - x0.astype(F32)
    x1 = r1.astype(BF16)
    x2 = (r1 - x1.astype(F32)).astype(BF16)
    return _dot(tri16, x0) + _dot(tri16, x1) + _dot(tri16, x2)


def _sigmoid(x):
    return 1.0 / (1.0 + jnp.exp(-x))


def _silu(x):
    return x * _sigmoid(x)


def _softplus(x):
    return jnp.maximum(x, 0.0) + jnp.log(1.0 + jnp.exp(-jnp.abs(x)))


def _full(shape):
    return pl.BlockSpec(shape, lambda *_: (0,) * len(shape))


def _blk_type(i):
    return jnp.where(i < N_CTX, 0, 1 + (i - N_CTX) // NLB)


def _blk_seq(i):
    return jnp.where(i < N_CTX, i, N_CTX + (i - N_CTX) // NLB)


def _blk_first(i):
    return jnp.logical_or(i < N_CTX, (i - N_CTX) % NLB == 0)


def _blk_last(i):
    return jnp.logical_or(i < N_CTX, (i - N_CTX) % NLB == NLB - 1)


def _scan_blk(d, i):
    return i + d * (NBLK - 1 - 2 * i)


def _scan_ends(d, blk):
    fwd = d == 0
    a, b = _blk_first(blk), _blk_last(blk)
    first = jnp.logical_or(jnp.logical_and(fwd, a), jnp.logical_and(jnp.logical_not(fwd), b))
    last = jnp.logical_or(jnp.logical_and(fwd, b), jnp.logical_and(jnp.logical_not(fwd), a))
    return first, last


def _lat_of_seq(seq):
    return jnp.clip(seq - N_CTX, 0, N_LAT - 1)


def _adaln_kernel(c_ref, w_ref, b_ref, o_ref):
    o_ref[0] = _dot(_silu(c_ref[...]), w_ref[0], HIGHEST) + b_ref[0]


def _adaln(cond8, w_ada, b_ada):
    return pl.pallas_call(
        _adaln_kernel,
        grid=(DEPTH, 6),
        in_specs=[pl.BlockSpec((8, D_MODEL), lambda l, j: (0, 0)),
                  pl.BlockSpec((1, D_MODEL, D_MODEL), lambda l, j: (l, 0, j)),
                  pl.BlockSpec((1, 1, D_MODEL), lambda l, j: (l, 0, j))],
        out_specs=pl.BlockSpec((1, 8, D_MODEL), lambda l, j: (l, 0, j)),
        out_shape=jax.ShapeDtypeStruct((DEPTH, 8, 6 * D_MODEL), F32),
        compiler_params=_cparams(("parallel", "parallel")),
        name="adaln",
    )(cond8, w_ada, b_ada.reshape(DEPTH, 1, 6 * D_MODEL))


def _normmod(x, nw, shift, scale):
    y = x * lax.rsqrt(jnp.mean(x * x, axis=-1, keepdims=True) + NORM_EPS) * nw
    return y * (1.0 + scale) + shift


def _normproj_kernel(h_ref, mod_ref, nw_ref, w_ref, o_ref, xn_ref):
    @pl.when(pl.program_id(1) == 0)
    def _():
        m = mod_ref[0]
        xn_ref[...] = _normmod(h_ref[...], nw_ref[...], m[:, :D_MODEL], m[:, D_MODEL:2 * D_MODEL]).astype(BF16)
    o_ref[...] = _dot(xn_ref[...], w_ref[0])


def _normproj(h, mod, nw, w, l, tn, name):
    tm = 2 * TB
    n = w.shape[2]
    return pl.pallas_call(
        _normproj_kernel,
        grid=(T_ALL // tm, n // tn),
        in_specs=[pl.BlockSpec((tm, D_MODEL), lambda i, j: (i, 0)),
                  pl.BlockSpec((1, 1, 6 * D_MODEL), lambda i, j: (_blk_type(2 * i), 0, 0)),
                  pl.BlockSpec((1, D_MODEL), lambda i, j: (0, 0)),
                  pl.BlockSpec((1, D_MODEL, tn), lambda i, j: (l, 0, j))],
        out_specs=pl.BlockSpec((tm, tn), lambda i, j: (i, j)),
        out_shape=jax.ShapeDtypeStruct((T_ALL, n), F32),
        scratch_shapes=[pltpu.VMEM((tm, D_MODEL), BF16)],
        compiler_params=_cparams(("parallel", "arbitrary")),
        name=name,
    )(h, mod, nw, w)


def _with_halo(prev_ref, cur_ref, next_ref, i):
    prev = jnp.where(_blk_first(i), 0.0, prev_ref[...])
    nxt = jnp.where(_blk_last(i), 0.0, next_ref[...])
    return jnp.concatenate([prev, cur_ref[...], nxt], axis=0)


def _shifted(ext, s):
    n = ext.shape[0]
    return pltpu.roll(ext, (-s) % n, axis=0)[HALO:HALO + TB]


def _seg_sum(x, width):
    ii = lax.broadcasted_iota(jnp.int32, (LANE, LANE), 0)
    jj = lax.broadcasted_iota(jnp.int32, (LANE, LANE), 1)
    shift = int(math.log2(width))
    ones = ((ii >> shift) == (jj >> shift)).astype(BF16)
    parts = [_dot_exact_rhs(x[:, c:c + LANE], ones) for c in range(0, x.shape[1], LANE)]
    return jnp.concatenate(parts, axis=1)


def _rwkv_prep_kernel(pp_ref, p_ref, pn_ref, mu_ref, w0_ref, w2_ref, a0_ref, a2_ref, g2_ref, kk_w_ref, ka_ref,
                      rk_ref, r_ref, v_ref, kk_ref, g_ref, bonus_ref, lw_ref, kd_ref, b_ref):
    i = pl.program_id(0)
    ext = _with_halo(pp_ref, p_ref, pn_ref, i)
    p = p_ref[...]
    p = p + mu_ref[...] * (0.5 * (_shifted(ext, -1) + _shifted(ext, 1)) - p)
    r, k, v = p[:, 0:BR], p[:, BR:2 * BR], p[:, 2 * BR:3 * BR]
    wd, ad, gd = p[:, 1536:1664], p[:, 1664:1792], p[:, 1792:1920]
    kk = k * kk_w_ref[...]
    kk = kk / jnp.maximum(jnp.sqrt(_seg_sum(kk * kk, HD)), 1e-12)
    r_ref[...] = r
    v_ref[...] = v
    kk_ref[...] = kk
    tw = jnp.tanh(wd)
    g_ref[...] = _dot_split(_sigmoid(gd), g2_ref[...])
    wl = [_dot_split(tw, w2_ref[d]) for d in range(2)]
    al = [_dot_split(ad, a2_ref[d]) for d in range(2)]
    kd_sum = jnp.zeros_like(k)
    for d in range(2):
        lw_ref[d] = -_sigmoid(w0_ref[d:d + 1, :] + wl[d]) * math.exp(-0.5)
        a = _sigmoid(a0_ref[d:d + 1, :] + al[d])
        kd = k * (1.0 + (a - 1.0) * ka_ref[...])
        kd_ref[d] = kd
        b_ref[d] = kk * a
        kd_sum = kd_sum + kd
    bonus_ref[...] = _seg_sum(r * kd_sum * rk_ref[...], HD) * v


def _halo_specs(cols):
    per = TB // HALO
    return [pl.BlockSpec((HALO, cols), lambda i: (jnp.maximum(i * per - 1, 0), 0)),
            pl.BlockSpec((TB, cols), lambda i: (i, 0)),
            pl.BlockSpec((HALO, cols), lambda i: (jnp.minimum((i + 1) * per, T_ALL // HALO - 1), 0))]


def _rwkv_prep(p_rwkv, lp):
    tok = jax.ShapeDtypeStruct((T_ALL, BR), F32)
    tok2 = jax.ShapeDtypeStruct((2, T_ALL, BR), F32)
    spec1 = pl.BlockSpec((TB, BR), lambda i: (i, 0))
    spec2 = pl.BlockSpec((2, TB, BR), lambda i: (0, i, 0))
    return pl.pallas_call(
        _rwkv_prep_kernel,
        grid=(NBLK,),
        in_specs=_halo_specs(RWKV_COLS) + [
            _full((1, RWKV_COLS)), _full((2, BR)), _full((2, LANE, BR)), _full((2, BR)), _full((2, LANE, BR)),
            _full((LANE, BR)), _full((1, BR)), _full((1, BR)), _full((1, BR))],
        out_specs=[spec1, spec1, spec1, spec1, spec1, spec2, spec2, spec2],
        out_shape=[tok, tok, tok, tok, tok, tok2, tok2, tok2],
        compiler_params=_cparams(("parallel",)),
        name="rwkv_prep",
    )(p_rwkv, p_rwkv, p_rwkv, lp['rwkv_mu'], lp['rwkv_w0'], lp['rwkv_w2p'], lp['rwkv_a0'], lp['rwkv_a2p'],
      lp['rwkv_g2'], lp['rwkv_k_k'], lp['rwkv_k_a'], lp['rwkv_r_k'])


def _pair_bd(x):
    h0 = lax.broadcasted_iota(jnp.int32, x.shape, 1) < HD
    return jnp.concatenate([jnp.where(h0, x, 0.0), jnp.where(h0, 0.0, x)], axis=0)


def _rwkv_scan_kernel(r_ref, v_ref, kk_ref, lw_ref, kd_ref, b_ref, s0_ref, y_ref, sf_ref,
                      s_ref, lrhs_ref, ly_ref, tinv_ref, v16_ref, ebt_ref, wc_ref):
    d = pl.program_id(0)
    blk = _scan_blk(d, pl.program_id(1))
    first, last = _scan_ends(d, blk)
    sgn = 1 - 2 * d
    C = RWKV_CHUNK
    nchunk = TB // C
    npair = BR // LANE
    ii = lax.broadcasted_iota(jnp.int32, (LANE, LANE), 0)
    jj = lax.broadcasted_iota(jnp.int32, (LANE, LANE), 1)
    same = (ii >> 6) == (jj >> 6)

    @pl.when(jnp.logical_and(first, blk < N_CTX))
    def _():
        s_ref[...] = jnp.zeros_like(s_ref)

    @pl.when(jnp.logical_and(first, blk >= N_CTX))
    def _():
        for p in range(npair):
            a = jnp.concatenate([s0_ref[0, 0, 2 * p], s0_ref[0, 0, 2 * p + 1]], axis=1)
            s_ref[p] = jnp.where(same, jnp.concatenate([a, a], axis=0).T, 0.0)

    ci = lax.broadcasted_iota(jnp.int32, (C, C), 0)
    cj = lax.broadcasted_iota(jnp.int32, (C, C), 1)
    tri = ((cj - ci) * sgn <= 0).astype(BF16)
    rel = ((jj & (C - 1)) - (ii & (C - 1))) * sgn
    before = jnp.logical_and(same, rel < 0)
    before_incl = jnp.logical_and(same, rel <= 0)

    def rows_of(c):
        cc = c + d * (nchunk - 1 - 2 * c)
        return pl.ds(pl.multiple_of(cc * C, C), C)

    n_of = {}
    for c in range(nchunk):
        rows = rows_of(c)
        lw = lw_ref[0, rows, :]
        cum = _cumsum_rows(tri, lw)
        tot = jnp.sum(lw, axis=0, keepdims=True)
        e_neg = jnp.exp(-cum)
        e_end = jnp.exp(tot - cum)
        r = r_ref[rows, :]
        v = v_ref[rows, :]
        kk = kk_ref[rows, :]
        kd = kd_ref[0, rows, :]
        b = b_ref[0, rows, :]
        k_t, b_t = kd * e_neg, b * e_neg
        kk_h, r_h = kk * jnp.exp(cum - lw), r * jnp.exp(cum)
        k_e, b_e = kd * e_end, b * e_end
        for p in range(npair):
            sl = slice(p * LANE, (p + 1) * LANE)
            KK, R, KT, BT, V, KE, BE = (_pair_bd(x[:, sl]) for x in (kk_h, r_h, k_t, b_t, v, k_e, b_e))
            kk16, r16 = KK.astype(BF16), R.astype(BF16)
            g = _dot_nt(jnp.concatenate([kk16, r16], axis=0),
                        jnp.concatenate([BT, KT], axis=0).astype(BF16))
            n_of[c, p] = jnp.where(before, g[:LANE, :LANE], 0.0)
            a1 = jnp.where(before, g[:LANE, LANE:], 0.0)
            a4 = jnp.where(before_incl, g[LANE:, :LANE], 0.0)
            a3 = jnp.where(before_incl, g[LANE:, LANE:], 0.0)
            lrhs_ref[c, p] = jnp.concatenate([kk16, a1.astype(BF16)], axis=1)
            ly_ref[c, p] = jnp.concatenate([r16, a3.astype(BF16), a4.astype(BF16)], axis=1)
            v16_ref[c, p] = V.astype(BF16)
            ebt_ref[c, p] = jnp.concatenate([KE.T, BE.T], axis=1).astype(BF16)
            wc_ref[c, p] = jnp.exp(jnp.broadcast_to(tot[:, sl], (LANE, LANE)).T)

    chains = list(n_of)
    level = lambda f: {k: f(k) for k in chains}
    eye = (ii == jj).astype(F32)
    blk16 = (ii >> 4) == (jj >> 4)
    dg = level(lambda k: jnp.where(blk16, n_of[k], 0.0))
    low = level(lambda k: n_of[k] - dg[k])
    d2 = level(lambda k: _dot16(dg[k], dg[k]))
    d4 = level(lambda k: _dot16(d2[k], d2[k]))
    x1 = level(lambda k: _dot16(eye - dg[k], eye + d2[k]))
    d8 = level(lambda k: _dot16(d4[k], d4[k]))
    x2 = level(lambda k: _dot16(x1[k], eye + d4[k]))
    xd = level(lambda k: _dot16(x2[k], eye + d8[k]))
    m = level(lambda k: _dot16(xd[k], low[k]))
    m2 = level(lambda k: _dot16(m[k], m[k]))
    t1 = level(lambda k: _dot16(eye - m[k], eye + m2[k]))
    tinv = level(lambda k: _dot16(t1[k], xd[k]))
    for k in chains:
        tinv_ref[k] = tinv[k].astype(BF16)

    pairs = range(npair)
    for c in range(nchunk):
        rows = rows_of(c)
        s = [s_ref[p] for p in pairs]
        s16 = [x.astype(BF16) for x in s]
        v16 = [v16_ref[c, p] for p in pairs]
        rhs = [_dot(lrhs_ref[c, p], jnp.concatenate([s16[p], v16[p]], axis=0)).astype(BF16) for p in pairs]
        u16 = [(-_dot(tinv_ref[c, p], rhs[p])).astype(BF16) for p in pairs]
        for p in pairs:
            s_ref[p] = s[p] * wc_ref[c, p] + _dot(ebt_ref[c, p], jnp.concatenate([v16[p], u16[p]], axis=0))
        for p in pairs:
            y = _dot(ly_ref[c, p], jnp.concatenate([s16[p], v16[p], u16[p]], axis=0))
            y_ref[0, rows, p * LANE:(p + 1) * LANE] = y[:C] + y[C:]

    @pl.when(last)
    def _():
        for p in range(npair):
            sp = s_ref[p]
            folded = sp[:HD] + sp[HD:]
            z = jnp.concatenate([folded, folded], axis=0).T
            sf_ref[0, 0, 2 * p] = z[:HD, :HD]
            sf_ref[0, 0, 2 * p + 1] = z[HD:, :HD]


def _rwkv_scan(r, v, kk, lw, kd, b, s0):
    np_ = BR // LANE
    nchunk = TB // RWKV_CHUNK
    spec1 = pl.BlockSpec((TB, BR), lambda d, i: (_scan_blk(d, i), 0))
    spec2 = pl.BlockSpec((1, TB, BR), lambda d, i: (d, _scan_blk(d, i), 0))
    seq = lambda d, i: _blk_seq(_scan_blk(d, i))
    s0spec = pl.BlockSpec((1, 1, HEADS, HD, HD), lambda d, i: (_lat_of_seq(seq(d, i)), d, 0, 0, 0))
    sfspec = pl.BlockSpec((1, 1, HEADS, HD, HD), lambda d, i: (seq(d, i), d, 0, 0, 0))
    per = lambda rows, cols, dt: pltpu.VMEM((nchunk, np_, rows, cols), dt)
    return pl.pallas_call(
        _rwkv_scan_kernel,
        grid=(2, NBLK),
        in_specs=[spec1, spec1, spec1, spec2, spec2, spec2, s0spec],
        out_specs=[spec2, sfspec],
        out_shape=[jax.ShapeDtypeStruct((2, T_ALL, BR), F32),
                   jax.ShapeDtypeStruct((N_SEQ, 2, HEADS, HD, HD), F32)],
        scratch_shapes=[pltpu.VMEM((np_, LANE, LANE), F32),
                        per(LANE, 2 * LANE, BF16), per(LANE, 3 * LANE, BF16), per(LANE, LANE, BF16),
                        per(LANE, LANE, BF16), per(LANE, 2 * LANE, BF16), per(LANE, LANE, F32)],
        compiler_params=_cparams(("arbitrary", "arbitrary")),
        name="rwkv_scan",
    )(r, v, kk, lw, kd, b, s0)


def _ssd_prep_kernel(pp_ref, p_ref, pn_ref, cw_ref, cb_ref, dtb_ref, x_ref, bc_ref, dt_ref):
    i = pl.program_id(0)
    ext = _with_halo(pp_ref, p_ref, pn_ref, i)[:, BR:BR + SSD_XBC]
    acc = cb_ref[...] + cw_ref[0:1, :] * _shifted(ext, -(SSD_CONV // 2))
    for j in range(1, SSD_CONV):
        acc = acc + cw_ref[j:j + 1, :] * _shifted(ext, j - SSD_CONV // 2)
    xbc = _silu(acc)
    x_ref[...] = xbc[:, :BR]
    bc_ref[...] = xbc[:, BR:]
    dt_ref[...] = _softplus(p_ref[:, BR + SSD_XBC:] + dtb_ref[...])


def _ssd_prep(p_ssd, lp):
    return pl.pallas_call(
        _ssd_prep_kernel,
        grid=(NBLK,),
        in_specs=_halo_specs(SSD_PCOLS) + [_full((8, SSD_XBC)), _full((1, SSD_XBC)), _full((1, 2 * LANE))],
        out_specs=[pl.BlockSpec((TB, BR), lambda i: (i, 0)),
                   pl.BlockSpec((TB, 2 * LANE), lambda i: (i, 0)),
                   pl.BlockSpec((TB, 2 * LANE), lambda i: (i, 0))],
        out_shape=[jax.ShapeDtypeStruct((T_ALL, BR), F32),
                   jax.ShapeDtypeStruct((T_ALL, 2 * LANE), F32),
                   jax.ShapeDtypeStruct((T_ALL, 2 * LANE), F32)],
        compiler_params=_cparams(("parallel",)),
        name="ssd_prep",
    )(p_ssd, p_ssd, p_ssd, lp['ssd_conv_w8'], lp['ssd_conv_b'], lp['ssd_dt_bias_p'])


def _ssd_scan_kernel(x_ref, bc_ref, dt_ref, alog_ref, h0_ref, y_ref, hf_ref, hs_ref):
    d = pl.program_id(0)
    blk = _scan_blk(d, pl.program_id(1))
    first, last = _scan_ends(d, blk)
    sgn = 1 - 2 * d
    npair = BR // LANE
    half64 = lax.broadcasted_iota(jnp.int32, (HD, LANE), 1) < HD

    @pl.when(jnp.logical_and(first, blk < N_CTX))
    def _():
        hs_ref[...] = jnp.zeros_like(hs_ref)

    @pl.when(jnp.logical_and(first, blk >= N_CTX))
    def _():
        zero = jnp.zeros((HD, LANE), F32)
        blocks = []
        for q in range(npair):
            a = jnp.concatenate([h0_ref[0, 0, 2 * q], h0_ref[0, 0, 2 * q + 1]], axis=1)
            t = jnp.concatenate([a, a], axis=0).T
            blocks.append(jnp.where(half64, t[:HD], t[HD:]))
        rows = [jnp.concatenate([blocks[q] if q // 2 == g else zero for q in range(npair)], axis=1)
                for g in range(2)]
        hs_ref[...] = jnp.concatenate(rows, axis=0)

    x = x_ref[...]
    bm = bc_ref[:, :LANE]
    cm = bc_ref[:, LANE:]
    dt = dt_ref[...]
    a_neg = -jnp.exp(alog_ref[0])
    dta = dt * a_neg
    qi = lax.broadcasted_iota(jnp.int32, (TB, TB), 0)
    qj = lax.broadcasted_iota(jnp.int32, (TB, TB), 1)
    before_incl = (qj - qi) * sgn <= 0
    a_cum = _cumsum_rows(before_incl.astype(BF16), dta)
    tot = jnp.sum(dta, axis=0, keepdims=True)
    a_cum_t = a_cum.T
    dt_t = dt.T
    eh = lax.broadcasted_iota(jnp.int32, (LANE, BR), 0)
    ec = lax.broadcasted_iota(jnp.int32, (LANE, BR), 1)
    expand = ((ec >> 6) == eh).astype(BF16)
    e_in = _dot_exact_rhs(jnp.exp(a_cum), expand)
    to_end = _dot_exact_rhs(jnp.exp(tot - a_cum) * dt, expand)
    dec = _dot_exact_rhs(jnp.broadcast_to(jnp.exp(tot), (8, LANE)), expand)[0:1]
    hs = hs_ref[...]
    cb16, bb16, xb16 = cm.astype(BF16), bm.astype(BF16), x.astype(BF16)
    glane = lax.broadcasted_iota(jnp.int32, (TB, LANE), 1) >> 6
    half = lax.broadcasted_iota(jnp.int32, (TB, LANE), 1) < HD
    cbg = [_dot_nt(jnp.where(glane == g, cm, 0.0).astype(BF16), bb16) for g in range(2)]
    y_off = _dot(cb16, hs.astype(BF16)) * e_in
    upd = _dot_tn(bb16, (x * to_end).astype(BF16))
    scores = []
    for h in range(HEADS):
        seg = a_cum[:, h:h + 1] - a_cum_t[h:h + 1, :]
        decay = jnp.exp(jnp.where(before_incl, seg, -jnp.inf))
        scores.append((cbg[h // 4] * decay * dt_t[h:h + 1, :]).astype(BF16))
    y_heads = [_dot(scores[h], xb16[:, (h // 2) * LANE:(h // 2 + 1) * LANE]) for h in range(HEADS)]
    y_parts = [jnp.where(half, y_heads[2 * q], y_heads[2 * q + 1]) for q in range(npair)]
    y_ref[0] = jnp.concatenate(y_parts, axis=1) + y_off
    ui = lax.broadcasted_iota(jnp.int32, (LANE, BR), 0)
    uj = lax.broadcasted_iota(jnp.int32, (LANE, BR), 1)
    hs_ref[...] = hs * dec + jnp.where((ui >> 6) == (uj >> 8), upd, 0.0)

    @pl.when(last)
    def _():
        hn = hs_ref[...]
        for q in range(npair):
            g = q // 2
            w = hn[g * HD:(g + 1) * HD, q * LANE:(q + 1) * LANE]
            z = jnp.concatenate([w, w], axis=0).T
            hf_ref[0, 0, 2 * q] = z[:HD, :HD]
            hf_ref[0, 0, 2 * q + 1] = z[HD:, :HD]


def _ssd_scan(x, bc, dt, alog, h0):
    seq = lambda d, i: _blk_seq(_scan_blk(d, i))
    return pl.pallas_call(
        _ssd_scan_kernel,
        grid=(2, NBLK),
        in_specs=[pl.BlockSpec((TB, BR), lambda d, i: (_scan_blk(d, i), 0)),
                  pl.BlockSpec((TB, 2 * LANE), lambda d, i: (_scan_blk(d, i), 0)),
                  pl.BlockSpec((TB, LANE), lambda d, i: (_scan_blk(d, i), d)),
                  pl.BlockSpec((1, 1, LANE), lambda d, i: (d, 0, 0)),
                  pl.BlockSpec((1, 1, HEADS, HD, HD), lambda d, i: (_lat_of_seq(seq(d, i)), d, 0, 0, 0))],
        out_specs=[pl.BlockSpec((1, TB, BR), lambda d, i: (d, _scan_blk(d, i), 0)),
                   pl.BlockSpec((1, 1, HEADS, HD, HD), lambda d, i: (seq(d, i), d, 0, 0, 0))],
        out_shape=[jax.ShapeDtypeStruct((2, T_ALL, BR), F32),
                   jax.ShapeDtypeStruct((N_SEQ, 2, HEADS, HD, HD), F32)],
        scratch_shapes=[pltpu.VMEM((LANE, BR), F32)],
        compiler_params=_cparams(("arbitrary", "arbitrary")),
        name="ssd_scan",
    )(x, bc, dt, alog, h0)


def _gqa(q, kd, vd, sink_ref, bias):
    nq = q.shape[0]
    log2e = math.log2(math.e)
    half = lax.broadcasted_iota(jnp.int32, (nq, LANE), 1) < HD
    q = q * (ATT_SCALE * log2e)
    scores = []
    for g in range(ATT_KVH):
        rows = []
        for j in (2 * g, 2 * g + 1):
            q2 = q[:, j * LANE:(j + 1) * LANE]
            rows += [jnp.where(half, q2, 0.0), jnp.where(half, 0.0, q2)]
        qg = jnp.concatenate(rows, axis=0).astype(BF16)
        scores.append(_dot_nt(qg, kd[g].astype(BF16)))
    outs = []
    for g in range(ATT_KVH):
        s = scores[g]
        if bias is not None:
            kb = bias.shape[1]
            s = jnp.concatenate([s[:, :kb] + jnp.concatenate([bias] * ATT_GROUP, axis=0), s[:, kb:]], axis=1)
        sink = jnp.concatenate(
            [jnp.full((nq, 1), sink_ref[ATT_GROUP * g + h] * log2e, F32) for h in range(ATT_GROUP)], axis=0)
        mx = jnp.maximum(jnp.max(s, axis=-1, keepdims=True), sink)
        e = jnp.exp2(s - mx)
        inv = 1.0 / (jnp.sum(e, axis=-1, keepdims=True) + jnp.exp2(sink - mx))
        o = _dot(e.astype(BF16), vd[g].astype(BF16)) * inv
        outs += [jnp.where(half, o[0:nq], o[nq:2 * nq]), jnp.where(half, o[2 * nq:3 * nq], o[3 * nq:])]
    return jnp.concatenate(outs, axis=1)


def _ctx_attn_kernel(sink_ref, p_ref, o_ref):
    kd = [p_ref[:, BR + g * LANE:BR + (g + 1) * LANE] for g in range(ATT_KVH)]
    vd = [p_ref[:, BR + (2 + g) * LANE:BR + (3 + g) * LANE] for g in range(ATT_KVH)]
    o_ref[...] = _gqa(p_ref[:, :BR], kd, vd, sink_ref, None)


def _ctx_attn(p_att, sink):
    return pl.pallas_call(
        _ctx_attn_kernel,
        grid=(N_CTX,),
        in_specs=[pl.BlockSpec(memory_space=pltpu.SMEM), pl.BlockSpec((TB, ATT_PCOLS), lambda i: (i, 0))],
        out_specs=pl.BlockSpec((TB, BR), lambda i: (i, 0)),
        out_shape=jax.ShapeDtypeStruct((T_CTX, BR), F32),
        compiler_params=_cparams(("parallel",)),
        name="ctx_attn",
    )(sink, p_att)


def _rope(x, cos, sin_signed):
    lanes = x.shape[1]
    reps = lanes // LANE
    if reps > 1:
        cos = jnp.concatenate([cos] * reps, axis=1)
        sin_signed = jnp.concatenate([sin_signed] * reps, axis=1)
    lo = (lax.broadcasted_iota(jnp.int32, x.shape, 1) & 31) < 16
    partner = jnp.where(lo, pltpu.roll(x, lanes - 16, axis=1), pltpu.roll(x, 16, axis=1))
    return x * cos + partner * sin_signed


def _lat_attn_kernel(sink_ref, pq_ref, pp_ref, pn_ref, cq_ref, sq_ref, cp_ref, sp_ref, cn_ref, sn_ref,
                     kc_ref, vc_ref, o_ref):
    j = pl.program_id(1)
    hw = ATT_WINDOW
    q = _rope(pq_ref[:, :BR], cq_ref[...], sq_ref[...])
    kd, vd = [], []
    for g in range(ATT_KVH):
        kc, vc = BR + g * LANE, BR + (2 + g) * LANE
        kh, vh = g * LANE, (2 + g) * LANE
        kd.append(jnp.concatenate([_rope(pp_ref[:, kh:kh + LANE], cp_ref[...], sp_ref[...]),
                                   _rope(pq_ref[:, kc:kc + LANE], cq_ref[...], sq_ref[...]),
                                   _rope(pn_ref[:, kh:kh + LANE], cn_ref[...], sn_ref[...]),
                                   kc_ref[0, g]], axis=0))
        vd.append(jnp.concatenate([pp_ref[:, vh:vh + LANE], pq_ref[:, vc:vc + LANE], pn_ref[:, vh:vh + LANE],
                                   vc_ref[0, g]], axis=0))
    nloc = TB + 2 * hw
    qi = lax.broadcasted_iota(jnp.int32, (TB, nloc), 0)
    kj = lax.broadcasted_iota(jnp.int32, (TB, nloc), 1)
    rel = kj - hw - qi
    kpos = j * TB - hw + kj
    valid = (rel <= ATT_WINDOW) & (rel >= -ATT_WINDOW) & (kpos >= 0) & (kpos < LAT_LEN)
    o_ref[...] = _gqa(q, kd, vd, sink_ref, jnp.where(valid, 0.0, -jnp.inf))


def _lat_attn(p_att, sink, rope_cos, rope_sin, k_ctx, v_ctx):
    nb = LAT_LEN // TB
    hw = ATT_WINDOW
    per = TB // hw
    qrow = lambda b, j: N_CTX + b * nb + j
    hrow = lambda b, j: per * (N_CTX + b * nb)
    prv = lambda j: jnp.maximum(per * j - 1, 0)
    nxt = lambda j: jnp.minimum(per * (j + 1), per * nb - 1)
    hspec = lambda f: pl.BlockSpec((hw, 4 * LANE), lambda b, j: (hrow(b, j) + f(j), 1))
    tspec = lambda rows, f: pl.BlockSpec((rows, LANE), lambda b, j: (f(j), 0))
    cspec = pl.BlockSpec((1, ATT_KVH, PAST_LEN, LANE), lambda b, j: (b, 0, 0, 0))
    return pl.pallas_call(
        _lat_attn_kernel,
        grid=(N_LAT, nb),
        in_specs=[pl.BlockSpec(memory_space=pltpu.SMEM),
                  pl.BlockSpec((TB, ATT_PCOLS), lambda b, j: (qrow(b, j), 0)), hspec(prv), hspec(nxt),
                  tspec(TB, lambda j: j), tspec(TB, lambda j: j), tspec(hw, prv), tspec(hw, prv),
                  tspec(hw, nxt), tspec(hw, nxt), cspec, cspec],
        out_specs=pl.BlockSpec((TB, BR), lambda b, j: (b * nb + j, 0)),
        out_shape=jax.ShapeDtypeStruct((N_LAT * LAT_LEN, BR), F32),
        compiler_params=_cparams(("parallel", "parallel")),
        name="lat_attn",
    )(sink, p_att, p_att, p_att, rope_cos, rope_sin, rope_cos, rope_sin, rope_cos, rope_sin, k_ctx, v_ctx)


def _merge_kernel(h_ref, mod_ref, nw_ref, yr_ref, bonus_ref, g_ref, lnw_ref, lnb_ref, ys_ref, xs_ref, z_ref,
                  dvec_ref, snw_ref, yac_ref, yal_ref, wg_ref, wb_ref, wo_ref, o_ref):
    m = mod_ref[0]
    h = h_ref[...]
    pg = _dot(_normmod(h, nw_ref[...], m[:, :D_MODEL], m[:, D_MODEL:2 * D_MODEL]).astype(BF16), wg_ref[0])
    y = yr_ref[0] + yr_ref[1] + bonus_ref[...]
    mu = _seg_sum(y, HD) * (1.0 / HD)
    yc = y - mu
    var = _seg_sum(yc * yc, HD) * (1.0 / HD)
    y_a = (yc * lax.rsqrt(var + RWKV_GN_EPS) * lnw_ref[...] + lnb_ref[...]) * g_ref[...]
    y = (ys_ref[0] + ys_ref[1] + dvec_ref[...] * xs_ref[...]) * _silu(z_ref[...])
    y_b = y * lax.rsqrt(jnp.mean(y * y, axis=-1, keepdims=True) + NORM_EPS) * snw_ref[...]
    y_c = jnp.where(pl.program_id(0) < N_CTX, yac_ref[...], yal_ref[...])
    wide = [_dot(br.astype(BF16), wb_ref[0, n]) for n, br in enumerate((y_a, y_b, y_c))]
    merged = None
    for n in range(3):
        term = _sigmoid(pg[:, n * D_MODEL:(n + 1) * D_MODEL]) * wide[n]
        merged = term if merged is None else merged + term
    g1 = m[:, 2 * D_MODEL:3 * D_MODEL]
    o_ref[...] = h + g1 * _dot(merged.astype(BF16), wo_ref[0])


def _merge(h, mod, yr, bonus, g, ys, xs, p_ssd, ya_ctx, ya_lat, lp, w_gate, w_branch, w_out, l):
    tm = TB
    row = lambda w: pl.BlockSpec((tm, w), lambda i: (i, 0))
    row2 = pl.BlockSpec((2, tm, BR), lambda i: (0, i, 0))
    return pl.pallas_call(
        _merge_kernel,
        grid=(T_ALL // tm,),
        in_specs=[row(D_MODEL), pl.BlockSpec((1, 1, 6 * D_MODEL), lambda i: (_blk_type(i), 0, 0)),
                  _full((1, D_MODEL)),
                  row2, row(BR), row(BR), _full((1, BR)), _full((1, BR)),
                  row2, row(BR), row(BR), _full((1, BR)), _full((1, BR)),
                  pl.BlockSpec((tm, BR), lambda i: (jnp.minimum(i, N_CTX - 1), 0)),
                  pl.BlockSpec((tm, BR), lambda i: (jnp.maximum(i - N_CTX, 0), 0)),
                  pl.BlockSpec((1, D_MODEL, GATE_COLS), lambda i: (l, 0, 0)),
                  pl.BlockSpec((1, 3, BR, D_MODEL), lambda i: (l, 0, 0, 0)),
                  pl.BlockSpec((1, D_MODEL, D_MODEL), lambda i: (l, 0, 0))],
        out_specs=row(D_MODEL),
        out_shape=jax.ShapeDtypeStruct((T_ALL, D_MODEL), F32),
        compiler_params=_cparams(("parallel",)),
        name="merge",
    )(h, mod, lp['norm1_w'], yr, bonus, g, lp['rwkv_lnx_w'], lp['rwkv_lnx_b'], ys, xs, p_ssd, lp['ssd_d_vec'],
      lp['ssd_norm_w'], ya_ctx, ya_lat, w_gate, w_branch, w_out)


def _route_kernel(h_ref, mod_ref, nw_ref, wr_ref, xn_ref, gate_ref, slot_ref, slot_t_ref, cnt_ref):
    m = mod_ref[0]
    xn = _normmod(h_ref[...], nw_ref[...], m[:, 3 * D_MODEL:4 * D_MODEL], m[:, 4 * D_MODEL:5 * D_MODEL])
    xn_ref[...] = xn.astype(BF16)
    logits = _dot_split(xn, wr_ref[...])
    lane_i = lax.broadcasted_iota(jnp.int32, logits.shape, 1)
    lane = lane_i.astype(F32)
    lane_grp = (lane_i >> 3).astype(F32)
    neg = -jnp.inf
    big = float(LANE)
    is_g = (lane_i >= MOE_EXPERTS) & (lane_i < MOE_EXPERTS + MOE_GROUPS)
    gl = jnp.where(is_g, logits, neg)
    gmax = jnp.max(gl, axis=-1, keepdims=True)
    gsel = jnp.min(jnp.where(gl == gmax, lane - MOE_EXPERTS, big), axis=-1, keepdims=True)
    g_w = 1.0 / jnp.sum(jnp.where(is_g, jnp.exp(gl - gmax), 0.0), axis=-1, keepdims=True)
    el = jnp.where((lane_i < MOE_EXPERTS) & (lane_grp == gsel), logits, neg)
    m1 = jnp.max(el, axis=-1, keepdims=True)
    i1 = jnp.min(jnp.where(el == m1, lane, big), axis=-1, keepdims=True)
    el2 = jnp.where(lane == i1, neg, el)
    m2 = jnp.max(el2, axis=-1, keepdims=True)
    i2 = jnp.min(jnp.where(el2 == m2, lane, big), axis=-1, keepdims=True)
    e2 = jnp.exp(m2 - m1)
    w1 = 1.0 / (1.0 + e2)
    gate_ref[...] = jnp.where(lane == i1, w1 * g_w, jnp.where(lane == i2, e2 * w1 * g_w, 0.0))
    member = lane == gsel
    ti = lax.broadcasted_iota(jnp.int32, (TB, TB), 0)
    tj = lax.broadcasted_iota(jnp.int32, (TB, TB), 1)
    rank = _dot((tj <= ti).astype(BF16), jnp.where(member, 1.0, 0.0).astype(BF16))
    slot = jnp.where(member, rank - 1.0, -1.0)
    slot_ref[...] = slot
    slot_t_ref[...] = slot.T[:8]
    cnt_ref[0] = jnp.max(slot, axis=0, keepdims=True) + 1.0


def _route(h, mod, nw, wr):
    tm = TB
    return pl.pallas_call(
        _route_kernel,
        grid=(T_ALL // tm,),
        in_specs=[pl.BlockSpec((tm, D_MODEL), lambda i: (i, 0)),
                  pl.BlockSpec((1, 1, 6 * D_MODEL), lambda i: (_blk_type(i), 0, 0)),
                  _full((1, D_MODEL)), _full((D_MODEL, LANE))],
        out_specs=[pl.BlockSpec((tm, D_MODEL), lambda i: (i, 0)), pl.BlockSpec((tm, LANE), lambda i: (i, 0)),
                   pl.BlockSpec((tm, LANE), lambda i: (i, 0)), pl.BlockSpec((8, tm), lambda i: (0, i)),
                   pl.BlockSpec((1, 1, LANE), lambda i: (i, 0, 0))],
        out_shape=[jax.ShapeDtypeStruct((T_ALL, D_MODEL), BF16), jax.ShapeDtypeStruct((T_ALL, LANE), F32),
                   jax.ShapeDtypeStruct((T_ALL, LANE), F32), jax.ShapeDtypeStruct((8, T_ALL), F32),
                   jax.ShapeDtypeStruct((NBLK, 1, LANE), F32)],
        compiler_params=_cparams(("parallel",)),
        name="moe_route",
    )(h, mod, nw, wr)


MOE_EB = 4
MOE_TM = 4 * TB
MOE_CAP = 96
MOE_ROWS = (MOE_TM // TB) * MOE_CAP


def _expert_block(x, gate, lane0, w1_ref, w3_ref, w2_ref):
    lane = lax.broadcasted_iota(jnp.int32, gate.shape, 1)
    h1 = [_dot(x, w1_ref[0, k]) for k in range(MOE_EB)]
    h3 = [_dot(x, w3_ref[0, k]) for k in range(MOE_EB)]
    acts = []
    for k in range(MOE_EB):
        gcol = jnp.sum(jnp.where(lane == lane0 + k, gate, 0.0), axis=-1, keepdims=True)
        acts.append((_silu(h1[k]) * h3[k] * gcol).astype(BF16))
    return _dot(jnp.concatenate(acts, axis=1), w2_ref[0].reshape(MOE_EB * D_EXPERT, D_MODEL))


def _experts_kernel(cnt_ref, xn_ref, gate_ref, slot_ref, slot_t_ref, w1_ref, w3_ref, w2_ref, h_ref, mod_ref,
                    o_ref, xg_ref, gg_ref, yg_ref):
    i = pl.program_id(0)
    hg = pl.program_id(1)
    g = hg // 2
    nsub = MOE_TM // TB
    fits = cnt_ref[i * nsub, g] <= MOE_CAP
    for sb in range(1, nsub):
        fits = jnp.logical_and(fits, cnt_ref[i * nsub + sb, g] <= MOE_CAP)
    lane0 = hg * MOE_EB

    @pl.when(hg == 0)
    def _():
        o_ref[...] = jnp.zeros_like(o_ref)

    @pl.when(jnp.logical_and(fits, hg % 2 == 0))
    def _():
        sub8 = lax.broadcasted_iota(jnp.int32, (8, MOE_TM), 0)
        srow = jnp.max(jnp.where(sub8 == g, slot_t_ref[...], -2.0), axis=0, keepdims=True)
        rid = lax.broadcasted_iota(jnp.int32, (MOE_CAP, TB), 0).astype(F32)
        for sb in range(nsub):
            tok = slice(sb * TB, (sb + 1) * TB)
            sel = jnp.where(rid == srow[:, tok], 1.0, 0.0).astype(BF16)
            xg_ref[sb * MOE_CAP:(sb + 1) * MOE_CAP] = _dot(sel, xn_ref[tok, :]).astype(BF16)
            gh, gl = _split16(gate_ref[tok, :])
            gg_ref[sb * MOE_CAP:(sb + 1) * MOE_CAP] = _dot(sel, gh) + _dot(sel, gl)

    @pl.when(fits)
    def _():
        y = _expert_block(xg_ref[...], gg_ref[...], lane0, w1_ref, w3_ref, w2_ref)
        yg_ref[...] = jnp.where(hg % 2 == 0, y, yg_ref[...] + y)

    @pl.when(jnp.logical_and(fits, hg % 2 == 1))
    def _():
        lane = lax.broadcasted_iota(jnp.int32, (TB, LANE), 1)
        cid = lax.broadcasted_iota(jnp.int32, (TB, MOE_CAP), 1).astype(F32)
        for sb in range(nsub):
            tok = slice(sb * TB, (sb + 1) * TB)
            scol = jnp.max(jnp.where(lane == g, slot_ref[tok, :], -2.0), axis=-1, keepdims=True)
            sel = jnp.where(cid == scol, 1.0, 0.0).astype(BF16)
            yh, yl = _split16(yg_ref[sb * MOE_CAP:(sb + 1) * MOE_CAP])
            o_ref[tok, :] += _dot(sel, yh) + _dot(sel, yl)

    @pl.when(jnp.logical_not(fits))
    def _():
        o_ref[...] += _expert_block(xn_ref[...], gate_ref[...], lane0, w1_ref, w3_ref, w2_ref)

    @pl.when(hg == 2 * MOE_GROUPS - 1)
    def _():
        g2 = mod_ref[0][:, 5 * D_MODEL:6 * D_MODEL]
        o_ref[...] = h_ref[...] + g2 * o_ref[...]


def _experts(cnt, xn, gate, slot, slot_t, w1, w3, w2, h, mod, l):
    tm = MOE_TM
    grid_spec = pltpu.PrefetchScalarGridSpec(
        num_scalar_prefetch=1,
        grid=(T_ALL // tm, MOE_EXPERTS // MOE_EB),
        in_specs=[pl.BlockSpec((tm, D_MODEL), lambda i, e, c: (i, 0)),
                  pl.BlockSpec((tm, LANE), lambda i, e, c: (i, 0)),
                  pl.BlockSpec((tm, LANE), lambda i, e, c: (i, 0)),
                  pl.BlockSpec((8, tm), lambda i, e, c: (0, i)),
                  pl.BlockSpec((1, MOE_EB, D_MODEL, D_EXPERT), lambda i, e, c: (l, e, 0, 0)),
                  pl.BlockSpec((1, MOE_EB, D_MODEL, D_EXPERT), lambda i, e, c: (l, e, 0, 0)),
                  pl.BlockSpec((1, MOE_EB, D_EXPERT, D_MODEL), lambda i, e, c: (l, e, 0, 0)),
                  pl.BlockSpec((tm, D_MODEL), lambda i, e, c: (i, 0)),
                  pl.BlockSpec((1, 1, 6 * D_MODEL), lambda i, e, c: (_blk_type((tm // TB) * i), 0, 0))],
        out_specs=pl.BlockSpec((tm, D_MODEL), lambda i, e, c: (i, 0)),
        scratch_shapes=[pltpu.VMEM((MOE_ROWS, D_MODEL), BF16), pltpu.VMEM((MOE_ROWS, LANE), F32),
                        pltpu.VMEM((MOE_ROWS, D_MODEL), F32)])
    return pl.pallas_call(
        _experts_kernel,
        grid_spec=grid_spec,
        out_shape=jax.ShapeDtypeStruct((T_ALL, D_MODEL), F32),
        compiler_params=_cparams(("parallel", "arbitrary")),
        name="moe_experts",
    )(cnt, xn, gate, slot, slot_t, w1, w3, w2, h, mod)


def _final_norm_kernel(h_ref, w_ref, o_ref):
    x = h_ref[...]
    o_ref[...] = x * lax.rsqrt(jnp.mean(x * x, axis=-1, keepdims=True) + NORM_EPS) * w_ref[...]


def _final_norm(h, w, row0, rows, name):
    tm = 2 * TB
    return pl.pallas_call(
        _final_norm_kernel,
        grid=(rows // tm,),
        in_specs=[pl.BlockSpec((tm, D_MODEL), lambda i: (row0 // tm + i, 0)), _full((1, D_MODEL))],
        out_specs=pl.BlockSpec((tm, D_MODEL), lambda i: (i, 0)),
        out_shape=jax.ShapeDtypeStruct((rows, D_MODEL), F32),
        compiler_params=_cparams(("parallel",)),
        name=name,
    )(h, w)


def _rope_tables():
    pos = jnp.arange(LAT_LEN)
    row = (pos // GRID_W).astype(F32)
    col = (pos % GRID_W).astype(F32)
    half = HD // 2
    inv = 1.0 / (ROPE_BASE ** (jnp.arange(0, half, 2, dtype=F32) / half))
    ar, ac = row[:, None] * inv[None, :], col[:, None] * inv[None, :]
    cos = jnp.concatenate([jnp.cos(ar), jnp.cos(ar), jnp.cos(ac), jnp.cos(ac)], axis=1)
    sin = jnp.concatenate([-jnp.sin(ar), jnp.sin(ar), -jnp.sin(ac), jnp.sin(ac)], axis=1)
    return jnp.tile(cos, (1, 2)), jnp.tile(sin, (1, 2))


def _dup_heads(x):
    x = jnp.swapaxes(x, -2, -3)
    return jnp.concatenate([x, x], axis=-1)


def _proj_weights(w_in):
    o_ssd, o_att = RWKV_COLS, RWKV_COLS + 1296
    o_gate = o_att + BR + 2 * ATT_KVH * HD
    pad = jnp.zeros((DEPTH, D_MODEL, LANE - HEADS), F32)
    dt0 = o_ssd + BR + SSD_XBC
    w_ssd = jnp.concatenate([w_in[:, :, o_ssd:dt0], w_in[:, :, dt0:dt0 + HEADS], pad,
                             w_in[:, :, dt0 + HEADS:dt0 + 2 * HEADS], pad], axis=2)
    kv = [w_in[:, :, o_att + BR + n * HD:o_att + BR + (n + 1) * HD] for n in range(2 * ATT_KVH)]
    w_att = jnp.concatenate([w_in[:, :, o_att:o_att + BR]] + [x for c in kv for x in (c, c)], axis=2)
    return (w_in[:, :, :o_ssd].astype(BF16), w_ssd.astype(BF16), w_att.astype(BF16),
            w_in[:, :, o_gate:].astype(BF16))


def _layer_params(l, prm):
    lp = {}
    row = lambda name: prm[name][l].reshape(1, -1)
    for name in ('norm1_w', 'norm2_w', 'rwkv_mu', 'rwkv_k_k', 'rwkv_k_a', 'rwkv_r_k', 'rwkv_lnx_w', 'rwkv_lnx_b',
                 'ssd_conv_b', 'ssd_norm_w'):
        lp[name] = row(name)
    lp['rwkv_w0'] = prm['rwkv_w0'][l]
    lp['rwkv_a0'] = prm['rwkv_a0'][l]
    z64 = jnp.zeros((HD, BR), F32)
    w2, a2 = prm['rwkv_w2'][l], prm['rwkv_a2'][l]
    lp['rwkv_w2p'] = jnp.stack([jnp.concatenate([w2[0], z64]), jnp.concatenate([z64, w2[1]])])
    lp['rwkv_a2p'] = jnp.stack([jnp.concatenate([a2[0], z64]), jnp.concatenate([z64, a2[1]])])
    lp['rwkv_g2'] = prm['rwkv_g2'][l]
    lp['ssd_conv_w8'] = jnp.concatenate([prm['ssd_conv_w'][l], jnp.zeros((8 - SSD_CONV, SSD_XBC), F32)])
    dtb = prm['ssd_dt_bias'][l]
    zp = jnp.zeros((LANE - HEADS,), F32)
    lp['ssd_dt_bias_p'] = jnp.concatenate([dtb[0], zp, dtb[1], zp]).reshape(1, 2 * LANE)
    lp['ssd_a_log_p'] = jnp.pad(prm['ssd_a_log'][l], ((0, 0), (0, LANE - HEADS))).reshape(2, 1, LANE)
    lp['ssd_d_vec'] = jnp.repeat(prm['ssd_d'][l], HD).reshape(1, BR)
    lp['att_sink'] = prm['att_sink'][l]
    wr = jnp.concatenate([prm['moe_w_expert'][l].reshape(D_MODEL, MOE_EXPERTS), prm['moe_w_group'][l]], axis=1)
    lp['w_route'] = jnp.pad(wr, ((0, 0), (0, LANE - MOE_EXPERTS - MOE_GROUPS)))
    return lp


def kernel(x_prompt, x_sample, cache_attn_k, cache_attn_v, state_rwkv, state_ssd, c, c_ctx, w_ada, b_ada, norm1_w, norm2_w, w_in, rwkv_mu, rwkv_w0, rwkv_w2, rwkv_a0, rwkv_a2, rwkv_g2, rwkv_k_k, rwkv_k_a, rwkv_r_k, rwkv_lnx_w, rwkv_lnx_b, ssd_conv_w, ssd_conv_b, ssd_dt_bias, ssd_a_log, ssd_d, ssd_norm_w, att_sink, w_branch, w_out, moe_w_group, moe_w_expert, moe_w1, moe_w3, moe_w2, final_norm_w):
    prm = dict(norm1_w=norm1_w, norm2_w=norm2_w, rwkv_mu=rwkv_mu, rwkv_w0=rwkv_w0, rwkv_w2=rwkv_w2,
               rwkv_a0=rwkv_a0, rwkv_a2=rwkv_a2, rwkv_g2=rwkv_g2, rwkv_k_k=rwkv_k_k, rwkv_k_a=rwkv_k_a,
               rwkv_r_k=rwkv_r_k.reshape(DEPTH, BR), rwkv_lnx_w=rwkv_lnx_w, rwkv_lnx_b=rwkv_lnx_b,
               ssd_conv_w=ssd_conv_w, ssd_conv_b=ssd_conv_b, ssd_dt_bias=ssd_dt_bias, ssd_a_log=ssd_a_log,
               ssd_d=ssd_d, ssd_norm_w=ssd_norm_w, att_sink=att_sink,
               moe_w_group=moe_w_group, moe_w_expert=moe_w_expert)
    w_rwkv, w_ssd, w_att, w_gate = _proj_weights(w_in)
    wb16, wo16 = w_branch.astype(BF16), w_out.astype(BF16)
    w1_16, w3_16, w2_16 = moe_w1.astype(BF16), moe_w3.astype(BF16), moe_w2.astype(BF16)
    kc_dup, vc_dup = _dup_heads(cache_attn_k), _dup_heads(cache_attn_v)
    cond8 = jnp.concatenate([c_ctx[None, :], c, jnp.zeros((8 - 1 - N_LAT, D_MODEL), F32)], axis=0)
    mod_all = _adaln(cond8, w_ada, b_ada)
    rope_cos, rope_sin = _rope_tables()
    h = jnp.concatenate([x_prompt.reshape(T_CTX, D_MODEL), x_sample.reshape(N_LAT * LAT_LEN, D_MODEL)], axis=0)
    ks, vs, srs, sss = [], [], [], []
    for l in range(DEPTH):
        lp = _layer_params(l, prm)
        mod = mod_all[l].reshape(8, 1, 6 * D_MODEL)
        p_rwkv = _normproj(h, mod, lp['norm1_w'], w_rwkv, l, RWKV_COLS, "proj_rwkv")
        p_ssd = _normproj(h, mod, lp['norm1_w'], w_ssd, l, SSD_PCOLS, "proj_ssd")
        p_att = _normproj(h, mod, lp['norm1_w'], w_att, l, ATT_PCOLS, "proj_att")
        r, v, kk, g, bonus, lw, kd, b = _rwkv_prep(p_rwkv, lp)
        yr, sf = _rwkv_scan(r, v, kk, lw, kd, b, state_rwkv[:, l])
        srs.append(sf)
        xs, bc, dt = _ssd_prep(p_ssd, lp)
        ys, hf = _ssd_scan(xs, bc, dt, lp['ssd_a_log_p'], state_ssd[:, l])
        sss.append(hf)
        ya_ctx = _ctx_attn(p_att, lp['att_sink'])
        ya_lat = _lat_attn(p_att, lp['att_sink'], rope_cos, rope_sin, kc_dup[:, l], vc_dup[:, l])
        kv = [p_att[:T_CTX, BR + n * LANE:BR + n * LANE + HD].reshape(N_CTX, TB, HD) for n in range(2 * ATT_KVH)]
        ks.append(jnp.stack(kv[:ATT_KVH], axis=2))
        vs.append(jnp.stack(kv[ATT_KVH:], axis=2))
        h = _merge(h, mod, yr, bonus, g, ys, xs, p_ssd, ya_ctx, ya_lat, lp, w_gate, wb16, wo16, l)
        xn, gate, slot, slot_t, cnt = _route(h, mod, lp['norm2_w'], lp['w_route'])
        cnt = cnt[:, 0, :MOE_GROUPS].astype(jnp.int32)
        h = _experts(cnt, xn, gate, slot, slot_t, w1_16, w3_16, w2_16, h, mod, l)
    fw = final_norm_w.reshape(1, D_MODEL)
    y_ctx = _final_norm(h, fw, 0, T_CTX, "final_norm_ctx")
    y_lat = _final_norm(h, fw, T_CTX, N_LAT * LAT_LEN, "final_norm_lat")
    return (y_ctx.reshape(N_CTX, TB, D_MODEL), y_lat.reshape(N_LAT, LAT_LEN, D_MODEL),
            jnp.stack(ks, axis=1), jnp.stack(vs, axis=1),
            jnp.stack(srs, axis=1)[:N_CTX], jnp.stack(sss, axis=1)[:N_CTX])
```

```python
import math

import jax
import jax.numpy as jnp
from jax import lax
from jax.experimental import pallas as pl
from jax.experimental.pallas import tpu as pltpu

F32 = jnp.float32
BF16 = jnp.bfloat16
HIGHEST = lax.Precision.HIGHEST

D_MODEL = 1024
DEPTH = 4
N_CTX = 16
N_LAT = 2
LAT_LEN = 4096
PAST_LEN = 512
GRID_W = 64
NORM_EPS = 1e-6
TB = 256
NLB = LAT_LEN // TB
NBLK = N_CTX + N_LAT * NLB
T_CTX = N_CTX * TB
T_ALL = NBLK * TB
N_SEQ = N_CTX + N_LAT
HALO = 8

HEADS = 8
HD = 64
BR = HEADS * HD
RWKV_COLS = 1920
RWKV_CHUNK = 64
RWKV_GN_EPS = 64e-5
SSD_CONV = 5
SSD_XBC = 768
SSD_PCOLS = 1536
ATT_KVH = 2
ATT_GROUP = 4
ATT_PCOLS = BR + 4 * 2 * HD
ATT_WINDOW = 128
ATT_SCALE = HD ** -0.5
ROPE_BASE = 10000.0
GATE_COLS = 3 * D_MODEL
MOE_GROUPS = 4
MOE_EXPERTS = 32
D_EXPERT = 256
LANE = 128

VMEM_LIMIT = 48 * 1024 * 1024


def _cparams(sem):
    return pltpu.CompilerParams(dimension_semantics=sem, vmem_limit_bytes=VMEM_LIMIT)


def _dot(a, b, precision=None):
    return jnp.dot(a, b, precision=precision, preferred_element_type=F32)


def _dot_nt(a, b):
    return lax.dot_general(a, b, (((1,), (1,)), ((), ())), preferred_element_type=F32)


def _dot_tn(a, b):
    return lax.dot_general(a, b, (((0,), (0,)), ((), ())), preferred_element_type=F32)


def _split16(x):
    hi = x.astype(BF16)
    return hi, (x - hi.astype(F32)).astype(BF16)


def _dot16(a, b):
    return _dot(a.astype(BF16), b.astype(BF16))


def _dot_split(a, b):
    ah, al = _split16(a)
    bh, bl = _split16(b)
    return _dot(jnp.concatenate([ah, al, ah], axis=1), jnp.concatenate([bh, bh, bl], axis=0))


def _dot_exact_rhs(a, b16):
    ah, al = _split16(a)
    return _dot(jnp.concatenate([ah, al], axis=1), jnp.concatenate([b16, b16], axis=0))


def _cumsum_rows(tri16, x):
    x0 = x.astype(BF16)
    r1 = x - x0.astype(F32)
    x1 = r1.astype(BF16)
    x2 = (r1 - x1.astype(F32)).astype(BF16)
    return _dot(tri16, x0) + _dot(tri16, x1) + _dot(tri16, x2)


def _sigmoid(x):
    return 1.0 / (1.0 + jnp.exp(-x))


def _silu(x):
    return x * _sigmoid(x)


def _softplus(x):
    return jnp.maximum(x, 0.0) + jnp.log(1.0 + jnp.exp(-jnp.abs(x)))


def _full(shape):
    return pl.BlockSpec(shape, lambda *_: (0,) * len(shape))


def _blk_type(i):
    return jnp.where(i < N_CTX, 0, 1 + (i - N_CTX) // NLB)


def _blk_seq(i):
    return jnp.where(i < N_CTX, i, N_CTX + (i - N_CTX) // NLB)


def _blk_first(i):
    return jnp.logical_or(i < N_CTX, (i - N_CTX) % NLB == 0)


def _blk_last(i):
    return jnp.logical_or(i < N_CTX, (i - N_CTX) % NLB == NLB - 1)


def _scan_blk(d, i):
    return i + d * (NBLK - 1 - 2 * i)


def _scan_ends(d, blk):
    fwd = d == 0
    a, b = _blk_first(blk), _blk_last(blk)
    first = jnp.logical_or(jnp.logical_and(fwd, a), jnp.logical_and(jnp.logical_not(fwd), b))
    last = jnp.logical_or(jnp.logical_and(fwd, b), jnp.logical_and(jnp.logical_not(fwd), a))
    return first, last


def _lat_of_seq(seq):
    return jnp.clip(seq - N_CTX, 0, N_LAT - 1)


def _adaln_kernel(c_ref, w_ref, b_ref, o_ref):
    o_ref[0] = _dot(_silu(c_ref[...]), w_ref[0], HIGHEST) + b_ref[0]


def _adaln(cond8, w_ada, b_ada):
    return pl.pallas_call(
        _adaln_kernel,
        grid=(DEPTH, 6),
        in_specs=[pl.BlockSpec((8, D_MODEL), lambda l, j: (0, 0)),
                  pl.BlockSpec((1, D_MODEL, D_MODEL), lambda l, j: (l, 0, j)),
                  pl.BlockSpec((1, 1, D_MODEL), lambda l, j: (l, 0, j))],
        out_specs=pl.BlockSpec((1, 8, D_MODEL), lambda l, j: (l, 0, j)),
        out_shape=jax.ShapeDtypeStruct((DEPTH, 8, 6 * D_MODEL), F32),
        compiler_params=_cparams(("parallel", "parallel")),
        name="adaln",
    )(cond8, w_ada, b_ada.reshape(DEPTH, 1, 6 * D_MODEL))


def _normmod(x, nw, shift, scale):
    y = x * lax.rsqrt(jnp.mean(x * x, axis=-1, keepdims=True) + NORM_EPS) * nw
    return y * (1.0 + scale) + shift


def _normproj_kernel(h_ref, mod_ref, nw_ref, w_ref, o_ref, xn_ref):
    @pl.when(pl.program_id(1) == 0)
    def _():
        m = mod_ref[0]
        xn_ref[...] = _normmod(h_ref[...], nw_ref[...], m[:, :D_MODEL], m[:, D_MODEL:2 * D_MODEL]).astype(BF16)
    o_ref[...] = _dot(xn_ref[...], w_ref[0])


def _normproj(h, mod, nw, w, l, tn, name):
    tm = 4 * TB
    n = w.shape[2]
    return pl.pallas_call(
        _normproj_kernel,
        grid=(T_ALL // tm, n // tn),
        in_specs=[pl.BlockSpec((tm, D_MODEL), lambda i, j: (i, 0)),
                  pl.BlockSpec((1, 1, 6 * D_MODEL), lambda i, j: (_blk_type((tm // TB) * i), 0, 0)),
                  pl.BlockSpec((1, D_MODEL), lambda i, j: (0, 0)),
                  pl.BlockSpec((1, D_MODEL, tn), lambda i, j: (l, 0, j))],
        out_specs=pl.BlockSpec((tm, tn), lambda i, j: (i, j)),
        out_shape=jax.ShapeDtypeStruct((T_ALL, n), F32),
        scratch_shapes=[pltpu.VMEM((tm, D_MODEL), BF16)],
        compiler_params=_cparams(("parallel", "arbitrary")),
        name=name,
    )(h, mod, nw, w)


def _with_halo(prev_ref, cur_ref, next_ref, i):
    prev = jnp.where(_blk_first(i), 0.0, prev_ref[...])
    nxt = jnp.where(_blk_last(i), 0.0, next_ref[...])
    return jnp.concatenate([prev, cur_ref[...], nxt], axis=0)


def _shifted(ext, s):
    n = ext.shape[0]
    return pltpu.roll(ext, (-s) % n, axis=0)[HALO:HALO + TB]


def _seg_sum(x, width):
    ii = lax.broadcasted_iota(jnp.int32, (LANE, LANE), 0)
    jj = lax.broadcasted_iota(jnp.int32, (LANE, LANE), 1)
    shift = int(math.log2(width))
    ones = ((ii >> shift) == (jj >> shift)).astype(BF16)
    parts = [_dot_exact_rhs(x[:, c:c + LANE], ones) for c in range(0, x.shape[1], LANE)]
    return jnp.concatenate(parts, axis=1)


def _rwkv_prep_kernel(pp_ref, p_ref, pn_ref, mu_ref, w0_ref, w2_ref, a0_ref, a2_ref, g2_ref, kk_w_ref, ka_ref,
                      rk_ref, r_ref, v_ref, kk_ref, g_ref, bonus_ref, lw_ref, kd_ref, b_ref):
    i = pl.program_id(0)
    ext = _with_halo(pp_ref, p_ref, pn_ref, i)
    p = p_ref[...]
    p = p + mu_ref[...] * (0.5 * (_shifted(ext, -1) + _shifted(ext, 1)) - p)
    r, k, v = p[:, 0:BR], p[:, BR:2 * BR], p[:, 2 * BR:3 * BR]
    wd, ad, gd = p[:, 1536:1664], p[:, 1664:1792], p[:, 1792:1920]
    kk = k * kk_w_ref[...]
    kk = kk / jnp.maximum(jnp.sqrt(_seg_sum(kk * kk, HD)), 1e-12)
    r_ref[...] = r
    v_ref[...] = v
    kk_ref[...] = kk
    tw = jnp.tanh(wd)
    g_ref[...] = _dot_split(_sigmoid(gd), g2_ref[...])
    wl = [_dot_split(tw, w2_ref[d]) for d in range(2)]
    al = [_dot_split(ad, a2_ref[d]) for d in range(2)]
    kd_sum = jnp.zeros_like(k)
    for d in range(2):
        lw_ref[d] = -_sigmoid(w0_ref[d:d + 1, :] + wl[d]) * math.exp(-0.5)
        a = _sigmoid(a0_ref[d:d + 1, :] + al[d])
        kd = k * (1.0 + (a - 1.0) * ka_ref[...])
        kd_ref[d] = kd
        b_ref[d] = kk * a
        kd_sum = kd_sum + kd
    bonus_ref[...] = _seg_sum(r * kd_sum * rk_ref[...], HD) * v


def _halo_specs(cols):
    per = TB // HALO
    return [pl.BlockSpec((HALO, cols), lambda i: (jnp.maximum(i * per - 1, 0), 0)),
            pl.BlockSpec((TB, cols), lambda i: (i, 0)),
            pl.BlockSpec((HALO, cols), lambda i: (jnp.minimum((i + 1) * per, T_ALL // HALO - 1), 0))]


def _rwkv_prep(p_rwkv, lp):
    tok = jax.ShapeDtypeStruct((T_ALL, BR), F32)
    tok2 = jax.ShapeDtypeStruct((2, T_ALL, BR), F32)
    spec1 = pl.BlockSpec((TB, BR), lambda i: (i, 0))
    spec2 = pl.BlockSpec((2, TB, BR), lambda i: (0, i, 0))
    return pl.pallas_call(
        _rwkv_prep_kernel,
        grid=(NBLK,),
        in_specs=_halo_specs(RWKV_COLS) + [
            _full((1, RWKV_COLS)), _full((2, BR)), _full((2, LANE, BR)), _full((2, BR)), _full((2, LANE, BR)),
            _full((LANE, BR)), _full((1, BR)), _full((1, BR)), _full((1, BR))],
        out_specs=[spec1, spec1, spec1, spec1, spec1, spec2, spec2, spec2],
        out_shape=[tok, tok, tok, tok, tok, tok2, tok2, tok2],
        compiler_params=_cparams(("parallel",)),
        name="rwkv_prep",
    )(p_rwkv, p_rwkv, p_rwkv, lp['rwkv_mu'], lp['rwkv_w0'], lp['rwkv_w2p'], lp['rwkv_a0'], lp['rwkv_a2p'],
      lp['rwkv_g2'], lp['rwkv_k_k'], lp['rwkv_k_a'], lp['rwkv_r_k'])


def _pair_bd(x):
    h0 = lax.broadcasted_iota(jnp.int32, x.shape, 1) < HD
    return jnp.concatenate([jnp.where(h0, x, 0.0), jnp.where(h0, 0.0, x)], axis=0)


def _rwkv_scan_kernel(r_ref, v_ref, kk_ref, lw_ref, kd_ref, b_ref, s0_ref, y_ref, sf_ref,
                      s_ref, lrhs_ref, ly_ref, tinv_ref, v16_ref, ebt_ref, wc_ref):
    d = pl.program_id(0)
    blk = _scan_blk(d, pl.program_id(1))
    first, last = _scan_ends(d, blk)
    sgn = 1 - 2 * d
    C = RWKV_CHUNK
    nchunk = TB // C
    npair = BR // LANE
    ii = lax.broadcasted_iota(jnp.int32, (LANE, LANE), 0)
    jj = lax.broadcasted_iota(jnp.int32, (LANE, LANE), 1)
    same = (ii >> 6) == (jj >> 6)

    @pl.when(jnp.logical_and(first, blk < N_CTX))
    def _():
        s_ref[...] = jnp.zeros_like(s_ref)

    @pl.when(jnp.logical_and(first, blk >= N_CTX))
    def _():
        for p in range(npair):
            a = jnp.concatenate([s0_ref[0, 0, 2 * p], s0_ref[0, 0, 2 * p + 1]], axis=1)
            s_ref[p] = jnp.where(same, jnp.concatenate([a, a], axis=0).T, 0.0)

    ci = lax.broadcasted_iota(jnp.int32, (C, C), 0)
    cj = lax.broadcasted_iota(jnp.int32, (C, C), 1)
    tri = ((cj - ci) * sgn <= 0).astype(BF16)
    rel = ((jj & (C - 1)) - (ii & (C - 1))) * sgn
    before = jnp.logical_and(same, rel < 0)
    before_incl = jnp.logical_and(same, rel <= 0)

    def rows_of(c):
        cc = c + d * (nchunk - 1 - 2 * c)
        return pl.ds(pl.multiple_of(cc * C, C), C)

    eye = (ii == jj).astype(F32)
    blk16 = (ii >> 4) == (jj >> 4)
    pairs = range(npair)

    def decay(c):
        lw = lw_ref[0, rows_of(c), :]
        return lw, _cumsum_rows(tri, lw)

    def prep(c, lw, cum):
        rows = rows_of(c)
        tot = jnp.sum(lw, axis=0, keepdims=True)
        e_neg = jnp.exp(-cum)
        e_end = jnp.exp(tot - cum)
        r = r_ref[rows, :]
        v = v_ref[rows, :]
        kk = kk_ref[rows, :]
        kd = kd_ref[0, rows, :]
        b = b_ref[0, rows, :]
        k_t, b_t = kd * e_neg, b * e_neg
        kk_h, r_h = kk * jnp.exp(cum - lw), r * jnp.exp(cum)
        k_e, b_e = kd * e_end, b * e_end
        n_of = {}
        for p in pairs:
            sl = slice(p * LANE, (p + 1) * LANE)
            KK, R, KT, BT, V, KE, BE = (_pair_bd(x[:, sl]) for x in (kk_h, r_h, k_t, b_t, v, k_e, b_e))
            kk16, r16 = KK.astype(BF16), R.astype(BF16)
            g = _dot_nt(jnp.concatenate([kk16, r16], axis=0),
                        jnp.concatenate([BT, KT], axis=0).astype(BF16))
            n_of[c, p] = jnp.where(before, g[:LANE, :LANE], 0.0)
            a1 = jnp.where(before, g[:LANE, LANE:], 0.0)
            a4 = jnp.where(before_incl, g[LANE:, :LANE], 0.0)
            a3 = jnp.where(before_incl, g[LANE:, LANE:], 0.0)
            lrhs_ref[c, p] = jnp.concatenate([kk16, a1.astype(BF16)], axis=1)
            ly_ref[c, p] = jnp.concatenate([r16, a3.astype(BF16), a4.astype(BF16)], axis=1)
            v16_ref[c, p] = V.astype(BF16)
            ebt_ref[c, p] = jnp.concatenate([KE.T, BE.T], axis=1).astype(BF16)
            wc_ref[c, p] = jnp.exp(jnp.broadcast_to(tot[:, sl], (LANE, LANE)).T)
        return n_of

    def inverse_stages(n_of):
        chains = list(n_of)
        t = {}

        def level(name, f):
            def run():
                t[name] = {k: f(k) for k in chains}
            return run

        def store():
            for k in chains:
                tinv_ref[k] = t['tinv'][k].astype(BF16)

        return [
            level('dg', lambda k: jnp.where(blk16, n_of[k], 0.0)),
            level('low', lambda k: n_of[k] - t['dg'][k]),
            level('d2', lambda k: _dot16(t['dg'][k], t['dg'][k])),
            level('d4', lambda k: _dot16(t['d2'][k], t['d2'][k])),
            level('x1', lambda k: _dot16(eye - t['dg'][k], eye + t['d2'][k])),
            level('d8', lambda k: _dot16(t['d4'][k], t['d4'][k])),
            level('x2', lambda k: _dot16(t['x1'][k], eye + t['d4'][k])),
            level('xd', lambda k: _dot16(t['x2'][k], eye + t['d8'][k])),
            level('m', lambda k: _dot16(t['xd'][k], t['low'][k])),
            level('m2', lambda k: _dot16(t['m'][k], t['m'][k])),
            level('t1', lambda k: _dot16(eye - t['m'][k], eye + t['m2'][k])),
            level('tinv', lambda k: _dot16(t['t1'][k], t['xd'][k])),
            store,
        ]

    def state_stages(c):
        rows = rows_of(c)
        t = {}

        def load():
            t['s'] = [s_ref[p] for p in pairs]
            t['s16'] = [x.astype(BF16) for x in t['s']]
            t['v16'] = [v16_ref[c, p] for p in pairs]
            t['rhs'] = [_dot(lrhs_ref[c, p], jnp.concatenate([t['s16'][p], t['v16'][p]], axis=0)).astype(BF16)
                        for p in pairs]

        def solve():
            t['u16'] = [(-_dot(tinv_ref[c, p], t['rhs'][p])).astype(BF16) for p in pairs]

        def update():
            for p in pairs:
                s_ref[p] = t['s'][p] * wc_ref[c, p] + _dot(
                    ebt_ref[c, p], jnp.concatenate([t['v16'][p], t['u16'][p]], axis=0))

        def emit():
            for p in pairs:
                y = _dot(ly_ref[c, p], jnp.concatenate([t['s16'][p], t['v16'][p], t['u16'][p]], axis=0))
                y_ref[0, rows, p * LANE:(p + 1) * LANE] = y[:C] + y[C:]

        return [load, solve, update, emit]

    n_all = {}
    for c in range(nchunk):
        n_all.update(prep(c, *decay(c)))
    for stage in inverse_stages(n_all):
        stage()
    for c in range(nchunk):
        for stage in state_stages(c):
            stage()

    @pl.when(last)
    def _():
        for p in range(npair):
            sp = s_ref[p]
            folded = sp[:HD] + sp[HD:]
            z = jnp.concatenate([folded, folded], axis=0).T
            sf_ref[0, 0, 2 * p] = z[:HD, :HD]
            sf_ref[0, 0, 2 * p + 1] = z[HD:, :HD]


def _rwkv_scan(r, v, kk, lw, kd, b, s0):
    np_ = BR // LANE
    nchunk = TB // RWKV_CHUNK
    spec1 = pl.BlockSpec((TB, BR), lambda d, i: (_scan_blk(d, i), 0))
    spec2 = pl.BlockSpec((1, TB, BR), lambda d, i: (d, _scan_blk(d, i), 0))
    seq = lambda d, i: _blk_seq(_scan_blk(d, i))
    s0spec = pl.BlockSpec((1, 1, HEADS, HD, HD), lambda d, i: (_lat_of_seq(seq(d, i)), d, 0, 0, 0))
    sfspec = pl.BlockSpec((1, 1, HEADS, HD, HD), lambda d, i: (seq(d, i), d, 0, 0, 0))
    per = lambda rows, cols, dt: pltpu.VMEM((nchunk, np_, rows, cols), dt)
    return pl.pallas_call(
        _rwkv_scan_kernel,
        grid=(2, NBLK),
        in_specs=[spec1, spec1, spec1, spec2, spec2, spec2, s0spec],
        out_specs=[spec2, sfspec],
        out_shape=[jax.ShapeDtypeStruct((2, T_ALL, BR), F32),
                   jax.ShapeDtypeStruct((N_SEQ, 2, HEADS, HD, HD), F32)],
        scratch_shapes=[pltpu.VMEM((np_, LANE, LANE), F32),
                        per(LANE, 2 * LANE, BF16), per(LANE, 3 * LANE, BF16), per(LANE, LANE, BF16),
                        per(LANE, LANE, BF16), per(LANE, 2 * LANE, BF16), per(LANE, LANE, F32)],
        compiler_params=_cparams(("arbitrary", "arbitrary")),
        name="rwkv_scan",
    )(r, v, kk, lw, kd, b, s0)


def _ssd_prep_kernel(pp_ref, p_ref, pn_ref, cw_ref, cb_ref, dtb_ref, x_ref, bc_ref, dt_ref):
    i = pl.program_id(0)
    ext = _with_halo(pp_ref, p_ref, pn_ref, i)[:, BR:BR + SSD_XBC]
    acc = cb_ref[...] + cw_ref[0:1, :] * _shifted(ext, -(SSD_CONV // 2))
    for j in range(1, SSD_CONV):
        acc = acc + cw_ref[j:j + 1, :] * _shifted(ext, j - SSD_CONV // 2)
    xbc = _silu(acc)
    x_ref[...] = xbc[:, :BR]
    bc_ref[...] = xbc[:, BR:]
    dt_ref[...] = _softplus(p_ref[:, BR + SSD_XBC:] + dtb_ref[...])


def _ssd_prep(p_ssd, lp):
    return pl.pallas_call(
        _ssd_prep_kernel,
        grid=(NBLK,),
        in_specs=_halo_specs(SSD_PCOLS) + [_full((8, SSD_XBC)), _full((1, SSD_XBC)), _full((1, 2 * LANE))],
        out_specs=[pl.BlockSpec((TB, BR), lambda i: (i, 0)),
                   pl.BlockSpec((TB, 2 * LANE), lambda i: (i, 0)),
                   pl.BlockSpec((TB, 2 * LANE), lambda i: (i, 0))],
        out_shape=[jax.ShapeDtypeStruct((T_ALL, BR), F32),
                   jax.ShapeDtypeStruct((T_ALL, 2 * LANE), F32),
                   jax.ShapeDtypeStruct((T_ALL, 2 * LANE), F32)],
        compiler_params=_cparams(("parallel",)),
        name="ssd_prep",
    )(p_ssd, p_ssd, p_ssd, lp['ssd_conv_w8'], lp['ssd_conv_b'], lp['ssd_dt_bias_p'])


def _ssd_scan_kernel(x_ref, bc_ref, dt_ref, alog_ref, h0_ref, y_ref, hf_ref, hs_ref):
    d = pl.program_id(0)
    blk = _scan_blk(d, pl.program_id(1))
    first, last = _scan_ends(d, blk)
    sgn = 1 - 2 * d
    npair = BR // LANE
    half64 = lax.broadcasted_iota(jnp.int32, (HD, LANE), 1) < HD

    @pl.when(jnp.logical_and(first, blk < N_CTX))
    def _():
        hs_ref[...] = jnp.zeros_like(hs_ref)

    @pl.when(jnp.logical_and(first, blk >= N_CTX))
    def _():
        zero = jnp.zeros((HD, LANE), F32)
        blocks = []
        for q in range(npair):
            a = jnp.concatenate([h0_ref[0, 0, 2 * q], h0_ref[0, 0, 2 * q + 1]], axis=1)
            t = jnp.concatenate([a, a], axis=0).T
            blocks.append(jnp.where(half64, t[:HD], t[HD:]))
        rows = [jnp.concatenate([blocks[q] if q // 2 == g else zero for q in range(npair)], axis=1)
                for g in range(2)]
        hs_ref[...] = jnp.concatenate(rows, axis=0)

    x = x_ref[...]
    bm = bc_ref[:, :LANE]
    cm = bc_ref[:, LANE:]
    dt = dt_ref[...]
    a_neg = -jnp.exp(alog_ref[0])
    dta = dt * a_neg
    qi = lax.broadcasted_iota(jnp.int32, (TB, TB), 0)
    qj = lax.broadcasted_iota(jnp.int32, (TB, TB), 1)
    before_incl = (qj - qi) * sgn <= 0
    a_cum = _cumsum_rows(before_incl.astype(BF16), dta)
    tot = jnp.sum(dta, axis=0, keepdims=True)
    a_cum_t = a_cum.T
    dt_t = dt.T
    eh = lax.broadcasted_iota(jnp.int32, (LANE, BR), 0)
    ec = lax.broadcasted_iota(jnp.int32, (LANE, BR), 1)
    expand = ((ec >> 6) == eh).astype(BF16)
    e_in = _dot_exact_rhs(jnp.exp(a_cum), expand)
    to_end = _dot_exact_rhs(jnp.exp(tot - a_cum) * dt, expand)
    dec = _dot_exact_rhs(jnp.broadcast_to(jnp.exp(tot), (8, LANE)), expand)[0:1]
    hs = hs_ref[...]
    cb16, bb16, xb16 = cm.astype(BF16), bm.astype(BF16), x.astype(BF16)
    glane = lax.broadcasted_iota(jnp.int32, (TB, LANE), 1) >> 6
    half = lax.broadcasted_iota(jnp.int32, (TB, LANE), 1) < HD
    cbg = [_dot_nt(jnp.where(glane == g, cm, 0.0).astype(BF16), bb16) for g in range(2)]
    y_off = _dot(cb16, hs.astype(BF16)) * e_in
    upd = _dot_tn(bb16, (x * to_end).astype(BF16))
    scores = []
    for h in range(HEADS):
        seg = a_cum[:, h:h + 1] - a_cum_t[h:h + 1, :]
        decay = jnp.exp(jnp.where(before_incl, seg, -jnp.inf))
        scores.append((cbg[h // 4] * decay * dt_t[h:h + 1, :]).astype(BF16))
    y_heads = [_dot(scores[h], xb16[:, (h // 2) * LANE:(h // 2 + 1) * LANE]) for h in range(HEADS)]
    y_parts = [jnp.where(half, y_heads[2 * q], y_heads[2 * q + 1]) for q in range(npair)]
    y_ref[0] = jnp.concatenate(y_parts, axis=1) + y_off
    ui = lax.broadcasted_iota(jnp.int32, (LANE, BR), 0)
    uj = lax.broadcasted_iota(jnp.int32, (LANE, BR), 1)
    hs_ref[...] = hs * dec + jnp.where((ui >> 6) == (uj >> 8), upd, 0.0)

    @pl.when(last)
    def _():
        hn = hs_ref[...]
        for q in range(npair):
            g = q // 2
            w = hn[g * HD:(g + 1) * HD, q * LANE:(q + 1) * LANE]
            z = jnp.concatenate([w, w], axis=0).T
            hf_ref[0, 0, 2 * q] = z[:HD, :HD]
            hf_ref[0, 0, 2 * q + 1] = z[HD:, :HD]


def _ssd_scan(x, bc, dt, alog, h0):
    seq = lambda d, i: _blk_seq(_scan_blk(d, i))
    return pl.pallas_call(
        _ssd_scan_kernel,
        grid=(2, NBLK),
        in_specs=[pl.BlockSpec((TB, BR), lambda d, i: (_scan_blk(d, i), 0)),
                  pl.BlockSpec((TB, 2 * LANE), lambda d, i: (_scan_blk(d, i), 0)),
                  pl.BlockSpec((TB, LANE), lambda d, i: (_scan_blk(d, i), d)),
                  pl.BlockSpec((1, 1, LANE), lambda d, i: (d, 0, 0)),
                  pl.BlockSpec((1, 1, HEADS, HD, HD), lambda d, i: (_lat_of_seq(seq(d, i)), d, 0, 0, 0))],
        out_specs=[pl.BlockSpec((1, TB, BR), lambda d, i: (d, _scan_blk(d, i), 0)),
                   pl.BlockSpec((1, 1, HEADS, HD, HD), lambda d, i: (seq(d, i), d, 0, 0, 0))],
        out_shape=[jax.ShapeDtypeStruct((2, T_ALL, BR), F32),
                   jax.ShapeDtypeStruct((N_SEQ, 2, HEADS, HD, HD), F32)],
        scratch_shapes=[pltpu.VMEM((LANE, BR), F32)],
        compiler_params=_cparams(("arbitrary", "arbitrary")),
        name="ssd_scan",
    )(x, bc, dt, alog, h0)


def _gqa(q, kd, vd, sink_ref, bias):
    nq = q.shape[0]
    log2e = math.log2(math.e)
    half = lax.broadcasted_iota(jnp.int32, (nq, LANE), 1) < HD
    q = q * (ATT_SCALE * log2e)
    scores = []
    for g in range(ATT_KVH):
        rows = []
        for j in (2 * g, 2 * g + 1):
            q2 = q[:, j * LANE:(j + 1) * LANE]
            rows += [jnp.where(half, q2, 0.0), jnp.where(half, 0.0, q2)]
        qg = jnp.concatenate(rows, axis=0).astype(BF16)
        scores.append(_dot_nt(qg, kd[g].astype(BF16)))
    outs = []
    for g in range(ATT_KVH):
        s = scores[g]
        if bias is not None:
            kb = bias.shape[1]
            s = jnp.concatenate([s[:, :kb] + jnp.concatenate([bias] * ATT_GROUP, axis=0), s[:, kb:]], axis=1)
        sink = jnp.concatenate(
            [jnp.full((nq, 1), sink_ref[ATT_GROUP * g + h] * log2e, F32) for h in range(ATT_GROUP)], axis=0)
        mx = jnp.maximum(jnp.max(s, axis=-1, keepdims=True), sink)
        e = jnp.exp2(s - mx)
        inv = 1.0 / (jnp.sum(e, axis=-1, keepdims=True) + jnp.exp2(sink - mx))
        o = _dot(e.astype(BF16), vd[g].astype(BF16)) * inv
        outs += [jnp.where(half, o[0:nq], o[nq:2 * nq]), jnp.where(half, o[2 * nq:3 * nq], o[3 * nq:])]
    return jnp.concatenate(outs, axis=1)


def _ctx_attn_kernel(sink_ref, p_ref, o_ref):
    kd = [p_ref[:, BR + g * LANE:BR + (g + 1) * LANE] for g in range(ATT_KVH)]
    vd = [p_ref[:, BR + (2 + g) * LANE:BR + (3 + g) * LANE] for g in range(ATT_KVH)]
    o_ref[...] = _gqa(p_ref[:, :BR], kd, vd, sink_ref, None)


def _ctx_attn(p_att, sink):
    return pl.pallas_call(
        _ctx_attn_kernel,
        grid=(N_CTX,),
        in_specs=[pl.BlockSpec(memory_space=pltpu.SMEM), pl.BlockSpec((TB, ATT_PCOLS), lambda i: (i, 0))],
        out_specs=pl.BlockSpec((TB, BR), lambda i: (i, 0)),
        out_shape=jax.ShapeDtypeStruct((T_CTX, BR), F32),
        compiler_params=_cparams(("parallel",)),
        name="ctx_attn",
    )(sink, p_att)


def _rope(x, cos, sin_signed):
    lanes = x.shape[1]
    reps = lanes // LANE
    if reps > 1:
        cos = jnp.concatenate([cos] * reps, axis=1)
        sin_signed = jnp.concatenate([sin_signed] * reps, axis=1)
    lo = (lax.broadcasted_iota(jnp.int32, x.shape, 1) & 31) < 16
    partner = jnp.where(lo, pltpu.roll(x, lanes - 16, axis=1), pltpu.roll(x, 16, axis=1))
    return x * cos + partner * sin_signed


def _lat_attn_kernel(sink_ref, pq_ref, pp_ref, pn_ref, cq_ref, sq_ref, cp_ref, sp_ref, cn_ref, sn_ref,
                     kc_ref, vc_ref, o_ref):
    j = pl.program_id(1)
    hw = ATT_WINDOW
    q = _rope(pq_ref[:, :BR], cq_ref[...], sq_ref[...])
    kd, vd = [], []
    for g in range(ATT_KVH):
        kc, vc = BR + g * LANE, BR + (2 + g) * LANE
        kh, vh = g * LANE, (2 + g) * LANE
        kd.append(jnp.concatenate([_rope(pp_ref[:, kh:kh + LANE], cp_ref[...], sp_ref[...]),
                                   _rope(pq_ref[:, kc:kc + LANE], cq_ref[...], sq_ref[...]),
                                   _rope(pn_ref[:, kh:kh + LANE], cn_ref[...], sn_ref[...]),
                                   kc_ref[0, g]], axis=0))
        vd.append(jnp.concatenate([pp_ref[:, vh:vh + LANE], pq_ref[:, vc:vc + LANE], pn_ref[:, vh:vh + LANE],
                                   vc_ref[0, g]], axis=0))
    nloc = TB + 2 * hw
    qi = lax.broadcasted_iota(jnp.int32, (TB, nloc), 0)
    kj = lax.broadcasted_iota(jnp.int32, (TB, nloc), 1)
    rel = kj - hw - qi
    kpos = j * TB - hw + kj
    valid = (rel <= ATT_WINDOW) & (rel >= -ATT_WINDOW) & (kpos >= 0) & (kpos < LAT_LEN)
    o_ref[...] = _gqa(q, kd, vd, sink_ref, jnp.where(valid, 0.0, -jnp.inf))


def _lat_attn(p_att, sink, rope_cos, rope_sin, k_ctx, v_ctx):
    nb = LAT_LEN // TB
    hw = ATT_WINDOW
    per = TB // hw
    qrow = lambda b, j: N_CTX + b * nb + j
    hrow = lambda b, j: per * (N_CTX + b * nb)
    prv = lambda j: jnp.maximum(per * j - 1, 0)
    nxt = lambda j: jnp.minimum(per * (j + 1), per * nb - 1)
    hspec = lambda f: pl.BlockSpec((hw, 4 * LANE), lambda b, j: (hrow(b, j) + f(j), 1))
    tspec = lambda rows, f: pl.BlockSpec((rows, LANE), lambda b, j: (f(j), 0))
    cspec = pl.BlockSpec((1, ATT_KVH, PAST_LEN, LANE), lambda b, j: (b, 0, 0, 0))
    return pl.pallas_call(
        _lat_attn_kernel,
        grid=(N_LAT, nb),
        in_specs=[pl.BlockSpec(memory_space=pltpu.SMEM),
                  pl.BlockSpec((TB, ATT_PCOLS), lambda b, j: (qrow(b, j), 0)), hspec(prv), hspec(nxt),
                  tspec(TB, lambda j: j), tspec(TB, lambda j: j), tspec(hw, prv), tspec(hw, prv),
                  tspec(hw, nxt), tspec(hw, nxt), cspec, cspec],
        out_specs=pl.BlockSpec((TB, BR), lambda b, j: (b * nb + j, 0)),
        out_shape=jax.ShapeDtypeStruct((N_LAT * LAT_LEN, BR), F32),
        compiler_params=_cparams(("parallel", "parallel")),
        name="lat_attn",
    )(sink, p_att, p_att, p_att, rope_cos, rope_sin, rope_cos, rope_sin, rope_cos, rope_sin, k_ctx, v_ctx)


def _merge_kernel(h_ref, mod_ref, nw_ref, yr_ref, bonus_ref, g_ref, lnw_ref, lnb_ref, ys_ref, xs_ref, z_ref,
                  dvec_ref, snw_ref, yac_ref, yal_ref, wg_ref, wb_ref, wo_ref, o_ref):
    m = mod_ref[0]
    h = h_ref[...]
    pg = _dot(_normmod(h, nw_ref[...], m[:, :D_MODEL], m[:, D_MODEL:2 * D_MODEL]).astype(BF16), wg_ref[0])
    y = yr_ref[0] + yr_ref[1] + bonus_ref[...]
    mu = _seg_sum(y, HD) * (1.0 / HD)
    yc = y - mu
    var = _seg_sum(yc * yc, HD) * (1.0 / HD)
    y_a = (yc * lax.rsqrt(var + RWKV_GN_EPS) * lnw_ref[...] + lnb_ref[...]) * g_ref[...]
    y = (ys_ref[0] + ys_ref[1] + dvec_ref[...] * xs_ref[...]) * _silu(z_ref[...])
    y_b = y * lax.rsqrt(jnp.mean(y * y, axis=-1, keepdims=True) + NORM_EPS) * snw_ref[...]
    y_c = jnp.where(pl.program_id(0) < N_CTX, yac_ref[...], yal_ref[...])
    wide = [_dot(br.astype(BF16), wb_ref[0, n]) for n, br in enumerate((y_a, y_b, y_c))]
    merged = None
    for n in range(3):
        term = _sigmoid(pg[:, n * D_MODEL:(n + 1) * D_MODEL]) * wide[n]
        merged = term if merged is None else merged + term
    g1 = m[:, 2 * D_MODEL:3 * D_MODEL]
    o_ref[...] = h + g1 * _dot(merged.astype(BF16), wo_ref[0])


def _merge(h, mod, yr, bonus, g, ys, xs, p_ssd, ya_ctx, ya_lat, lp, w_gate, w_branch, w_out, l):
    tm = TB
    row = lambda w: pl.BlockSpec((tm, w), lambda i: (i, 0))
    row2 = pl.BlockSpec((2, tm, BR), lambda i: (0, i, 0))
    return pl.pallas_call(
        _merge_kernel,
        grid=(T_ALL // tm,),
        in_specs=[row(D_MODEL), pl.BlockSpec((1, 1, 6 * D_MODEL), lambda i: (_blk_type(i), 0, 0)),
                  _full((1, D_MODEL)),
                  row2, row(BR), row(BR), _full((1, BR)), _full((1, BR)),
                  row2, row(BR), row(BR), _full((1, BR)), _full((1, BR)),
                  pl.BlockSpec((tm, BR), lambda i: (jnp.minimum(i, N_CTX - 1), 0)),
                  pl.BlockSpec((tm, BR), lambda i: (jnp.maximum(i - N_CTX, 0), 0)),
                  pl.BlockSpec((1, D_MODEL, GATE_COLS), lambda i: (l, 0, 0)),
                  pl.BlockSpec((1, 3, BR, D_MODEL), lambda i: (l, 0, 0, 0)),
                  pl.BlockSpec((1, D_MODEL, D_MODEL), lambda i: (l, 0, 0))],
        out_specs=row(D_MODEL),
        out_shape=jax.ShapeDtypeStruct((T_ALL, D_MODEL), F32),
        compiler_params=_cparams(("parallel",)),
        name="merge",
    )(h, mod, lp['norm1_w'], yr, bonus, g, lp['rwkv_lnx_w'], lp['rwkv_lnx_b'], ys, xs, p_ssd, lp['ssd_d_vec'],
      lp['ssd_norm_w'], ya_ctx, ya_lat, w_gate, w_branch, w_out)


def _route_kernel(h_ref, mod_ref, nw_ref, wr_ref, xn_ref, gate_ref, slot_ref, slot_t_ref, cnt_ref):
    m = mod_ref[0]
    xn = _normmod(h_ref[...], nw_ref[...], m[:, 3 * D_MODEL:4 * D_MODEL], m[:, 4 * D_MODEL:5 * D_MODEL])
    xn_ref[...] = xn.astype(BF16)
    logits = _dot_split(xn, wr_ref[...])
    lane_i = lax.broadcasted_iota(jnp.int32, logits.shape, 1)
    lane = lane_i.astype(F32)
    lane_grp = (lane_i >> 3).astype(F32)
    neg = -jnp.inf
    big = float(LANE)
    is_g = (lane_i >= MOE_EXPERTS) & (lane_i < MOE_EXPERTS + MOE_GROUPS)
    gl = jnp.where(is_g, logits, neg)
    gmax = jnp.max(gl, axis=-1, keepdims=True)
    gsel = jnp.min(jnp.where(gl == gmax, lane - MOE_EXPERTS, big), axis=-1, keepdims=True)
    g_w = 1.0 / jnp.sum(jnp.where(is_g, jnp.exp(gl - gmax), 0.0), axis=-1, keepdims=True)
    el = jnp.where((lane_i < MOE_EXPERTS) & (lane_grp == gsel), logits, neg)
    m1 = jnp.max(el, axis=-1, keepdims=True)
    i1 = jnp.min(jnp.where(el == m1, lane, big), axis=-1, keepdims=True)
    el2 = jnp.where(lane == i1, neg, el)
    m2 = jnp.max(el2, axis=-1, keepdims=True)
    i2 = jnp.min(jnp.where(el2 == m2, lane, big), axis=-1, keepdims=True)
    e2 = jnp.exp(m2 - m1)
    w1 = 1.0 / (1.0 + e2)
    gate_ref[...] = jnp.where(lane == i1, w1 * g_w, jnp.where(lane == i2, e2 * w1 * g_w, 0.0))
    member = lane == gsel
    ti = lax.broadcasted_iota(jnp.int32, (TB, TB), 0)
    tj = lax.broadcasted_iota(jnp.int32, (TB, TB), 1)
    rank = _dot((tj <= ti).astype(BF16), jnp.where(member, 1.0, 0.0).astype(BF16))
    slot = jnp.where(member, rank - 1.0, -1.0)
    slot_ref[...] = slot
    slot_t_ref[...] = slot.T[:8]
    cnt_ref[0] = jnp.max(slot, axis=0, keepdims=True) + 1.0


def _route(h, mod, nw, wr):
    tm = TB
    return pl.pallas_call(
        _route_kernel,
        grid=(T_ALL // tm,),
        in_specs=[pl.BlockSpec((tm, D_MODEL), lambda i: (i, 0)),
                  pl.BlockSpec((1, 1, 6 * D_MODEL), lambda i: (_blk_type(i), 0, 0)),
                  _full((1, D_MODEL)), _full((D_MODEL, LANE))],
        out_specs=[pl.BlockSpec((tm, D_MODEL), lambda i: (i, 0)), pl.BlockSpec((tm, LANE), lambda i: (i, 0)),
                   pl.BlockSpec((tm, LANE), lambda i: (i, 0)), pl.BlockSpec((8, tm), lambda i: (0, i)),
                   pl.BlockSpec((1, 1, LANE), lambda i: (i, 0, 0))],
        out_shape=[jax.ShapeDtypeStruct((T_ALL, D_MODEL), BF16), jax.ShapeDtypeStruct((T_ALL, LANE), F32),
                   jax.ShapeDtypeStruct((T_ALL, LANE), F32), jax.ShapeDtypeStruct((8, T_ALL), F32),
                   jax.ShapeDtypeStruct((NBLK, 1, LANE), F32)],
        compiler_params=_cparams(("parallel",)),
        name="moe_route",
    )(h, mod, nw, wr)


MOE_EB = 4
MOE_TM = 4 * TB
MOE_CAP = 96
MOE_ROWS = (MOE_TM // TB) * MOE_CAP


def _expert_block(x, gate, lane0, w1_ref, w3_ref, w2_ref):
    lane = lax.broadcasted_iota(jnp.int32, gate.shape, 1)
    h1 = [_dot(x, w1_ref[0, k]) for k in range(MOE_EB)]
    h3 = [_dot(x, w3_ref[0, k]) for k in range(MOE_EB)]
    acts = []
    for k in range(MOE_EB):
        gcol = jnp.sum(jnp.where(lane == lane0 + k, gate, 0.0), axis=-1, keepdims=True)
        acts.append((_silu(h1[k]) * h3[k] * gcol).astype(BF16))
    return _dot(jnp.concatenate(acts, axis=1), w2_ref[0].reshape(MOE_EB * D_EXPERT, D_MODEL))


def _experts_kernel(cnt_ref, xn_ref, gate_ref, slot_ref, slot_t_ref, w1_ref, w3_ref, w2_ref, h_ref, mod_ref,
                    o_ref, xg_ref, gg_ref, yg_ref):
    i = pl.program_id(0)
    hg = pl.program_id(1)
    g = hg // 2
    nsub = MOE_TM // TB
    fits = cnt_ref[i * nsub, g] <= MOE_CAP
    for sb in range(1, nsub):
        fits = jnp.logical_and(fits, cnt_ref[i * nsub + sb, g] <= MOE_CAP)
    lane0 = hg * MOE_EB

    @pl.when(hg == 0)
    def _():
        o_ref[...] = jnp.zeros_like(o_ref)

    @pl.when(jnp.logical_and(fits, hg % 2 == 0))
    def _():
        sub8 = lax.broadcasted_iota(jnp.int32, (8, MOE_TM), 0)
        srow = jnp.max(jnp.where(sub8 == g, slot_t_ref[...], -2.0), axis=0, keepdims=True)
        rid = lax.broadcasted_iota(jnp.int32, (MOE_CAP, TB), 0).astype(F32)
        for sb in range(nsub):
            tok = slice(sb * TB, (sb + 1) * TB)
            sel = jnp.where(rid == srow[:, tok], 1.0, 0.0).astype(BF16)
            xg_ref[sb * MOE_CAP:(sb + 1) * MOE_CAP] = _dot(sel, xn_ref[tok, :]).astype(BF16)
            gh, gl = _split16(gate_ref[tok, :])
            gg = _dot(sel, jnp.concatenate([gh, gl], axis=1))
            gg_ref[sb * MOE_CAP:(sb + 1) * MOE_CAP] = gg[:, :LANE] + gg[:, LANE:]

    @pl.when(fits)
    def _():
        y = _expert_block(xg_ref[...], gg_ref[...], lane0, w1_ref, w3_ref, w2_ref)
        yg_ref[...] = jnp.where(hg % 2 == 0, y, yg_ref[...] + y)

    @pl.when(jnp.logical_and(fits, hg % 2 == 1))
    def _():
        lane = lax.broadcasted_iota(jnp.int32, (TB, LANE), 1)
        cid = lax.broadcasted_iota(jnp.int32, (TB, 2 * MOE_CAP), 1)
        cid = jnp.where(cid < MOE_CAP, cid, cid - MOE_CAP).astype(F32)
        for sb in range(nsub):
            tok = slice(sb * TB, (sb + 1) * TB)
            scol = jnp.max(jnp.where(lane == g, slot_ref[tok, :], -2.0), axis=-1, keepdims=True)
            sel2 = jnp.where(cid == scol, 1.0, 0.0).astype(BF16)
            yh, yl = _split16(yg_ref[sb * MOE_CAP:(sb + 1) * MOE_CAP])
            o_ref[tok, :] += _dot(sel2, jnp.concatenate([yh, yl], axis=0))

    @pl.when(jnp.logical_not(fits))
    def _():
        o_ref[...] += _expert_block(xn_ref[...], gate_ref[...], lane0, w1_ref, w3_ref, w2_ref)

    @pl.when(hg == 2 * MOE_GROUPS - 1)
    def _():
        g2 = mod_ref[0][:, 5 * D_MODEL:6 * D_MODEL]
        o_ref[...] = h_ref[...] + g2 * o_ref[...]


def _experts(cnt, xn, gate, slot, slot_t, w1, w3, w2, h, mod, l):
    tm = MOE_TM
    grid_spec = pltpu.PrefetchScalarGridSpec(
        num_scalar_prefetch=1,
        grid=(T_ALL // tm, MOE_EXPERTS // MOE_EB),
        in_specs=[pl.BlockSpec((tm, D_MODEL), lambda i, e, c: (i, 0)),
                  pl.BlockSpec((tm, LANE), lambda i, e, c: (i, 0)),
                  pl.BlockSpec((tm, LANE), lambda i, e, c: (i, 0)),
                  pl.BlockSpec((8, tm), lambda i, e, c: (0, i)),
                  pl.BlockSpec((1, MOE_EB, D_MODEL, D_EXPERT), lambda i, e, c: (l, e, 0, 0)),
                  pl.BlockSpec((1, MOE_EB, D_MODEL, D_EXPERT), lambda i, e, c: (l, e, 0, 0)),
                  pl.BlockSpec((1, MOE_EB, D_EXPERT, D_MODEL), lambda i, e, c: (l, e, 0, 0)),
                  pl.BlockSpec((tm, D_MODEL), lambda i, e, c: (i, 0)),
                  pl.BlockSpec((1, 1, 6 * D_MODEL), lambda i, e, c: (_blk_type((tm // TB) * i), 0, 0))],
        out_specs=pl.BlockSpec((tm, D_MODEL), lambda i, e, c: (i, 0)),
        scratch_shapes=[pltpu.VMEM((MOE_ROWS, D_MODEL), BF16), pltpu.VMEM((MOE_ROWS, LANE), F32),
                        pltpu.VMEM((MOE_ROWS, D_MODEL), F32)])
    return pl.pallas_call(
        _experts_kernel,
        grid_spec=grid_spec,
        out_shape=jax.ShapeDtypeStruct((T_ALL, D_MODEL), F32),
        compiler_params=_cparams(("parallel", "arbitrary")),
        name="moe_experts",
    )(cnt, xn, gate, slot, slot_t, w1, w3, w2, h, mod)


def _final_norm_kernel(h_ref, w_ref, o_ref):
    x = h_ref[...]
    o_ref[...] = x * lax.rsqrt(jnp.mean(x * x, axis=-1, keepdims=True) + NORM_EPS) * w_ref[...]


def _final_norm(h, w, row0, rows, name):
    tm = 2 * TB
    return pl.pallas_call(
        _final_norm_kernel,
        grid=(rows // tm,),
        in_specs=[pl.BlockSpec((tm, D_MODEL), lambda i: (row0 // tm + i, 0)), _full((1, D_MODEL))],
        out_specs=pl.BlockSpec((tm, D_MODEL), lambda i: (i, 0)),
        out_shape=jax.ShapeDtypeStruct((rows, D_MODEL), F32),
        compiler_params=_cparams(("parallel",)),
        name=name,
    )(h, w)


def _rope_tables():
    pos = jnp.arange(LAT_LEN)
    row = (pos // GRID_W).astype(F32)
    col = (pos % GRID_W).astype(F32)
    half = HD // 2
    inv = 1.0 / (ROPE_BASE ** (jnp.arange(0, half, 2, dtype=F32) / half))
    ar, ac = row[:, None] * inv[None, :], col[:, None] * inv[None, :]
    cos = jnp.concatenate([jnp.cos(ar), jnp.cos(ar), jnp.cos(ac), jnp.cos(ac)], axis=1)
    sin = jnp.concatenate([-jnp.sin(ar), jnp.sin(ar), -jnp.sin(ac), jnp.sin(ac)], axis=1)
    return jnp.tile(cos, (1, 2)), jnp.tile(sin, (1, 2))


def _dup_heads(x):
    x = jnp.swapaxes(x, -2, -3)
    return jnp.concatenate([x, x], axis=-1)


def _proj_weights(w_in):
    o_ssd, o_att = RWKV_COLS, RWKV_COLS + 1296
    o_gate = o_att + BR + 2 * ATT_KVH * HD
    pad = jnp.zeros((DEPTH, D_MODEL, LANE - HEADS), F32)
    dt0 = o_ssd + BR + SSD_XBC
    w_ssd = jnp.concatenate([w_in[:, :, o_ssd:dt0], w_in[:, :, dt0:dt0 + HEADS], pad,
                             w_in[:, :, dt0 + HEADS:dt0 + 2 * HEADS], pad], axis=2)
    kv = [w_in[:, :, o_att + BR + n * HD:o_att + BR + (n + 1) * HD] for n in range(2 * ATT_KVH)]
    w_att = jnp.concatenate([w_in[:, :, o_att:o_att + BR]] + [x for c in kv for x in (c, c)], axis=2)
    return (w_in[:, :, :o_ssd].astype(BF16), w_ssd.astype(BF16), w_att.astype(BF16),
            w_in[:, :, o_gate:].astype(BF16))


def _layer_params(l, prm):
    lp = {}
    row = lambda name: prm[name][l].reshape(1, -1)
    for name in ('norm1_w', 'norm2_w', 'rwkv_mu', 'rwkv_k_k', 'rwkv_k_a', 'rwkv_r_k', 'rwkv_lnx_w', 'rwkv_lnx_b',
                 'ssd_conv_b', 'ssd_norm_w'):
        lp[name] = row(name)
    lp['rwkv_w0'] = prm['rwkv_w0'][l]
    lp['rwkv_a0'] = prm['rwkv_a0'][l]
    z64 = jnp.zeros((HD, BR), F32)
    w2, a2 = prm['rwkv_w2'][l], prm['rwkv_a2'][l]
    lp['rwkv_w2p'] = jnp.stack([jnp.concatenate([w2[0], z64]), jnp.concatenate([z64, w2[1]])])
    lp['rwkv_a2p'] = jnp.stack([jnp.concatenate([a2[0], z64]), jnp.concatenate([z64, a2[1]])])
    lp['rwkv_g2'] = prm['rwkv_g2'][l]
    lp['ssd_conv_w8'] = jnp.concatenate([prm['ssd_conv_w'][l], jnp.zeros((8 - SSD_CONV, SSD_XBC), F32)])
    dtb = prm['ssd_dt_bias'][l]
    zp = jnp.zeros((LANE - HEADS,), F32)
    lp['ssd_dt_bias_p'] = jnp.concatenate([dtb[0], zp, dtb[1], zp]).reshape(1, 2 * LANE)
    lp['ssd_a_log_p'] = jnp.pad(prm['ssd_a_log'][l], ((0, 0), (0, LANE - HEADS))).reshape(2, 1, LANE)
    lp['ssd_d_vec'] = jnp.repeat(prm['ssd_d'][l], HD).reshape(1, BR)
    lp['att_sink'] = prm['att_sink'][l]
    wr = jnp.concatenate([prm['moe_w_expert'][l].reshape(D_MODEL, MOE_EXPERTS), prm['moe_w_group'][l]], axis=1)
    lp['w_route'] = jnp.pad(wr, ((0, 0), (0, LANE - MOE_EXPERTS - MOE_GROUPS)))
    return lp


def kernel(x_prompt, x_sample, cache_attn_k, cache_attn_v, state_rwkv, state_ssd, c, c_ctx, w_ada, b_ada, norm1_w, norm2_w, w_in, rwkv_mu, rwkv_w0, rwkv_w2, rwkv_a0, rwkv_a2, rwkv_g2, rwkv_k_k, rwkv_k_a, rwkv_r_k, rwkv_lnx_w, rwkv_lnx_b, ssd_conv_w, ssd_conv_b, ssd_dt_bias, ssd_a_log, ssd_d, ssd_norm_w, att_sink, w_branch, w_out, moe_w_group, moe_w_expert, moe_w1, moe_w3, moe_w2, final_norm_w):
    prm = dict(norm1_w=norm1_w, norm2_w=norm2_w, rwkv_mu=rwkv_mu, rwkv_w0=rwkv_w0, rwkv_w2=rwkv_w2,
               rwkv_a0=rwkv_a0, rwkv_a2=rwkv_a2, rwkv_g2=rwkv_g2, rwkv_k_k=rwkv_k_k, rwkv_k_a=rwkv_k_a,
               rwkv_r_k=rwkv_r_k.reshape(DEPTH, BR), rwkv_lnx_w=rwkv_lnx_w, rwkv_lnx_b=rwkv_lnx_b,
               ssd_conv_w=ssd_conv_w, ssd_conv_b=ssd_conv_b, ssd_dt_bias=ssd_dt_bias, ssd_a_log=ssd_a_log,
               ssd_d=ssd_d, ssd_norm_w=ssd_norm_w, att_sink=att_sink,
               moe_w_group=moe_w_group, moe_w_expert=moe_w_expert)
    w_rwkv, w_ssd, w_att, w_gate = _proj_weights(w_in)
    wb16, wo16 = w_branch.astype(BF16), w_out.astype(BF16)
    w1_16, w3_16, w2_16 = moe_w1.astype(BF16), moe_w3.astype(BF16), moe_w2.astype(BF16)
    kc_dup, vc_dup = _dup_heads(cache_attn_k), _dup_heads(cache_attn_v)
    cond8 = jnp.concatenate([c_ctx[None, :], c, jnp.zeros((8 - 1 - N_LAT, D_MODEL), F32)], axis=0)
    mod_all = _adaln(cond8, w_ada, b_ada)
    rope_cos, rope_sin = _rope_tables()
    h = jnp.concatenate([x_prompt.reshape(T_CTX, D_MODEL), x_sample.reshape(N_LAT * LAT_LEN, D_MODEL)], axis=0)
    ks, vs, srs, sss = [], [], [], []
    for l in range(DEPTH):
        lp = _layer_params(l, prm)
        mod = mod_all[l].reshape(8, 1, 6 * D_MODEL)
        p_rwkv = _normproj(h, mod, lp['norm1_w'], w_rwkv, l, RWKV_COLS, "proj_rwkv")
        p_ssd = _normproj(h, mod, lp['norm1_w'], w_ssd, l, SSD_PCOLS, "proj_ssd")
        p_att = _normproj(h, mod, lp['norm1_w'], w_att, l, ATT_PCOLS, "proj_att")
        r, v, kk, g, bonus, lw, kd, b = _rwkv_prep(p_rwkv, lp)
        yr, sf = _rwkv_scan(r, v, kk, lw, kd, b, state_rwkv[:, l])
        srs.append(sf)
        xs, bc, dt = _ssd_prep(p_ssd, lp)
        ys, hf = _ssd_scan(xs, bc, dt, lp['ssd_a_log_p'], state_ssd[:, l])
        sss.append(hf)
        ya_ctx = _ctx_attn(p_att, lp['att_sink'])
        ya_lat = _lat_attn(p_att, lp['att_sink'], rope_cos, rope_sin, kc_dup[:, l], vc_dup[:, l])
        kv = [p_att[:T_CTX, BR + n * LANE:BR + n * LANE + HD].reshape(N_CTX, TB, HD) for n in range(2 * ATT_KVH)]
        ks.append(jnp.stack(kv[:ATT_KVH], axis=2))
        vs.append(jnp.stack(kv[ATT_KVH:], axis=2))
        h = _merge(h, mod, yr, bonus, g, ys, xs, p_ssd, ya_ctx, ya_lat, lp, w_gate, wb16, wo16, l)
        xn, gate, slot, slot_t, cnt = _route(h, mod, lp['norm2_w'], lp['w_route'])
        cnt = cnt[:, 0, :MOE_GROUPS].astype(jnp.int32)
        h = _experts(cnt, xn, gate, slot, slot_t, w1_16, w3_16, w2_16, h, mod, l)
    fw = final_norm_w.reshape(1, D_MODEL)
    y_ctx = _final_norm(h, fw, 0, T_CTX, "final_norm_ctx")
    y_lat = _final_norm(h, fw, T_CTX, N_LAT * LAT_LEN, "final_norm_lat")
    return (y_ctx.reshape(N_CTX, TB, D_MODEL), y_lat.reshape(N_LAT, LAT_LEN, D_MODEL),
            jnp.stack(ks, axis=1), jnp.stack(vs, axis=1),
            jnp.stack(srs, axis=1)[:N_CTX], jnp.stack(sss, axis=1)[:N_CTX])
```

```python
import math

import jax
import jax.numpy as jnp
from jax import lax
from jax.experimental import pallas as pl
from jax.experimental.pallas import tpu as pltpu

F32 = jnp.float32
BF16 = jnp.bfloat16
HIGHEST = lax.Precision.HIGHEST

D_MODEL = 1024
DEPTH = 4
N_CTX = 16
N_LAT = 2
LAT_LEN = 4096
PAST_LEN = 512
GRID_W = 64
NORM_EPS = 1e-6
TB = 256
NLB = LAT_LEN // TB
NBLK = N_CTX + N_LAT * NLB
T_CTX = N_CTX * TB
T_ALL = NBLK * TB
N_SEQ = N_CTX + N_LAT
HALO = 8

HEADS = 8
HD = 64
BR = HEADS * HD
RWKV_COLS = 1920
RWKV_CHUNK = 64
RWKV_GN_EPS = 64e-5
SSD_CONV = 5
SSD_XBC = 768
SSD_PCOLS = 1536
ATT_KVH = 2
ATT_GROUP = 4
ATT_PCOLS = BR + 4 * 2 * HD
ATT_WINDOW = 128
ATT_SCALE = HD ** -0.5
ROPE_BASE = 10000.0
GATE_COLS = 3 * D_MODEL
MOE_GROUPS = 4
MOE_EXPERTS = 32
D_EXPERT = 256
LANE = 128

VMEM_LIMIT = 48 * 1024 * 1024


def _cparams(sem):
    return pltpu.CompilerParams(dimension_semantics=sem, vmem_limit_bytes=VMEM_LIMIT)


def _dot(a, b, precision=None):
    return jnp.dot(a, b, precision=precision, preferred_element_type=F32)


def _dot_nt(a, b):
    return lax.dot_general(a, b, (((1,), (1,)), ((), ())), preferred_element_type=F32)


def _dot_tn(a, b):
    return lax.dot_general(a, b, (((0,), (0,)), ((), ())), preferred_element_type=F32)


def _split16(x):
    hi = x.astype(BF16)
    return hi, (x - hi.astype(F32)).astype(BF16)


def _dot16(a, b):
    return _dot(a.astype(BF16), b.astype(BF16))


def _dot_split(a, b):
    ah, al = _split16(a)
    bh, bl = _split16(b)
    return _dot(jnp.concatenate([ah, al, ah], axis=1), jnp.concatenate([bh, bh, bl], axis=0))


def _dot_exact_rhs(a, b16):
    ah, al = _split16(a)
    return _dot(jnp.concatenate([ah, al], axis=1), jnp.concatenate([b16, b16], axis=0))


def _cumsum_rows(tri16, x):
    x0 = x.astype(BF16)
    r1 = x - x0.astype(F32)
    x1 = r1.astype(BF16)
    x2 = (r1 - x1.astype(F32)).astype(BF16)
    return _dot(tri16, x0) + _dot(tri16, x1) + _dot(tri16, x2)


def _sigmoid(x):
    return 1.0 / (1.0 + jnp.exp(-x))


def _silu(x):
    return x * _sigmoid(x)


def _softplus(x):
    return jnp.maximum(x, 0.0) + jnp.log(1.0 + jnp.exp(-jnp.abs(x)))


def _full(shape):
    return pl.BlockSpec(shape, lambda *_: (0,) * len(shape))


def _blk_type(i):
    return jnp.where(i < N_CTX, 0, 1 + (i - N_CTX) // NLB)


def _blk_seq(i):
    return jnp.where(i < N_CTX, i, N_CTX + (i - N_CTX) // NLB)


def _blk_first(i):
    return jnp.logical_or(i < N_CTX, (i - N_CTX) % NLB == 0)


def _blk_last(i):
    return jnp.logical_or(i < N_CTX, (i - N_CTX) % NLB == NLB - 1)


def _scan_blk(d, i):
    return i + d * (NBLK - 1 - 2 * i)


def _scan_ends(d, blk):
    fwd = d == 0
    a, b = _blk_first(blk), _blk_last(blk)
    first = jnp.logical_or(jnp.logical_and(fwd, a), jnp.logical_and(jnp.logical_not(fwd), b))
    last = jnp.logical_or(jnp.logical_and(fwd, b), jnp.logical_and(jnp.logical_not(fwd), a))
    return first, last


def _lat_of_seq(seq):
    return jnp.clip(seq - N_CTX, 0, N_LAT - 1)


def _adaln_kernel(c_ref, w_ref, b_ref, o_ref):
    o_ref[0] = _dot(_silu(c_ref[...]), w_ref[0], HIGHEST) + b_ref[0]


def _adaln(cond8, w_ada, b_ada):
    return pl.pallas_call(
        _adaln_kernel,
        grid=(DEPTH, 6),
        in_specs=[pl.BlockSpec((8, D_MODEL), lambda l, j: (0, 0)),
                  pl.BlockSpec((1, D_MODEL, D_MODEL), lambda l, j: (l, 0, j)),
                  pl.BlockSpec((1, 1, D_MODEL), lambda l, j: (l, 0, j))],
        out_specs=pl.BlockSpec((1, 8, D_MODEL), lambda l, j: (l, 0, j)),
        out_shape=jax.ShapeDtypeStruct((DEPTH, 8, 6 * D_MODEL), F32),
        compiler_params=_cparams(("parallel", "parallel")),
        name="adaln",
    )(cond8, w_ada, b_ada.reshape(DEPTH, 1, 6 * D_MODEL))


def _normmod(x, nw, shift, scale):
    y = x * lax.rsqrt(jnp.mean(x * x, axis=-1, keepdims=True) + NORM_EPS) * nw
    return y * (1.0 + scale) + shift


def _normproj_kernel(h_ref, mod_ref, nw_ref, w_ref, o_ref, xn_ref):
    @pl.when(pl.program_id(1) == 0)
    def _():
        m = mod_ref[0]
        xn_ref[...] = _normmod(h_ref[...], nw_ref[...], m[:, :D_MODEL], m[:, D_MODEL:2 * D_MODEL]).astype(BF16)
    o_ref[...] = _dot(xn_ref[...], w_ref[0])


def _normproj(h, mod, nw, w, l, tn, name):
    tm = 4 * TB
    n = w.shape[2]
    return pl.pallas_call(
        _normproj_kernel,
        grid=(T_ALL // tm, n // tn),
        in_specs=[pl.BlockSpec((tm, D_MODEL), lambda i, j: (i, 0)),
                  pl.BlockSpec((1, 1, 6 * D_MODEL), lambda i, j: (_blk_type((tm // TB) * i), 0, 0)),
                  pl.BlockSpec((1, D_MODEL), lambda i, j: (0, 0)),
                  pl.BlockSpec((1, D_MODEL, tn), lambda i, j: (l, 0, j))],
        out_specs=pl.BlockSpec((tm, tn), lambda i, j: (i, j)),
        out_shape=jax.ShapeDtypeStruct((T_ALL, n), F32),
        scratch_shapes=[pltpu.VMEM((tm, D_MODEL), BF16)],
        compiler_params=_cparams(("parallel", "arbitrary")),
        name=name,
    )(h, mod, nw, w)


def _with_halo(prev_ref, cur_ref, next_ref, i):
    prev = jnp.where(_blk_first(i), 0.0, prev_ref[...])
    nxt = jnp.where(_blk_last(i), 0.0, next_ref[...])
    return jnp.concatenate([prev, cur_ref[...], nxt], axis=0)


def _shifted(ext, s):
    n = ext.shape[0]
    return pltpu.roll(ext, (-s) % n, axis=0)[HALO:HALO + TB]


def _seg_sum(x, width):
    ii = lax.broadcasted_iota(jnp.int32, (LANE, LANE), 0)
    jj = lax.broadcasted_iota(jnp.int32, (LANE, LANE), 1)
    shift = int(math.log2(width))
    ones = ((ii >> shift) == (jj >> shift)).astype(BF16)
    parts = [_dot_exact_rhs(x[:, c:c + LANE], ones) for c in range(0, x.shape[1], LANE)]
    return jnp.concatenate(parts, axis=1)


def _rwkv_prep_kernel(pp_ref, p_ref, pn_ref, mu_ref, w0_ref, w2_ref, a0_ref, a2_ref, g2_ref, kk_w_ref, ka_ref,
                      rk_ref, r_ref, v_ref, kk_ref, g_ref, bonus_ref, lw_ref, kd_ref, b_ref):
    i = pl.program_id(0)
    ext = _with_halo(pp_ref, p_ref, pn_ref, i)
    p = p_ref[...]
    p = p + mu_ref[...] * (0.5 * (_shifted(ext, -1) + _shifted(ext, 1)) - p)
    r, k, v = p[:, 0:BR], p[:, BR:2 * BR], p[:, 2 * BR:3 * BR]
    wd, ad, gd = p[:, 1536:1664], p[:, 1664:1792], p[:, 1792:1920]
    kk = k * kk_w_ref[...]
    kk = kk / jnp.maximum(jnp.sqrt(_seg_sum(kk * kk, HD)), 1e-12)
    r_ref[...] = r
    v_ref[...] = v
    kk_ref[...] = kk
    tw = jnp.tanh(wd)
    g_ref[...] = _dot_split(_sigmoid(gd), g2_ref[...])
    wl = [_dot_split(tw, w2_ref[d]) for d in range(2)]
    al = [_dot_split(ad, a2_ref[d]) for d in range(2)]
    kd_sum = jnp.zeros_like(k)
    for d in range(2):
        lw_ref[d] = -_sigmoid(w0_ref[d:d + 1, :] + wl[d]) * math.exp(-0.5)
        a = _sigmoid(a0_ref[d:d + 1, :] + al[d])
        kd = k * (1.0 + (a - 1.0) * ka_ref[...])
        kd_ref[d] = kd
        b_ref[d] = kk * a
        kd_sum = kd_sum + kd
    bonus_ref[...] = _seg_sum(r * kd_sum * rk_ref[...], HD) * v


def _halo_specs(cols):
    per = TB // HALO
    return [pl.BlockSpec((HALO, cols), lambda i: (jnp.maximum(i * per - 1, 0), 0)),
            pl.BlockSpec((TB, cols), lambda i: (i, 0)),
            pl.BlockSpec((HALO, cols), lambda i: (jnp.minimum((i + 1) * per, T_ALL // HALO - 1), 0))]


def _rwkv_prep(p_rwkv, lp):
    tok = jax.ShapeDtypeStruct((T_ALL, BR), F32)
    tok2 = jax.ShapeDtypeStruct((2, T_ALL, BR), F32)
    spec1 = pl.BlockSpec((TB, BR), lambda i: (i, 0))
    spec2 = pl.BlockSpec((2, TB, BR), lambda i: (0, i, 0))
    return pl.pallas_call(
        _rwkv_prep_kernel,
        grid=(NBLK,),
        in_specs=_halo_specs(RWKV_COLS) + [
            _full((1, RWKV_COLS)), _full((2, BR)), _full((2, LANE, BR)), _full((2, BR)), _full((2, LANE, BR)),
            _full((LANE, BR)), _full((1, BR)), _full((1, BR)), _full((1, BR))],
        out_specs=[spec1, spec1, spec1, spec1, spec1, spec2, spec2, spec2],
        out_shape=[tok, tok, tok, tok, tok, tok2, tok2, tok2],
        compiler_params=_cparams(("parallel",)),
        name="rwkv_prep",
    )(p_rwkv, p_rwkv, p_rwkv, lp['rwkv_mu'], lp['rwkv_w0'], lp['rwkv_w2p'], lp['rwkv_a0'], lp['rwkv_a2p'],
      lp['rwkv_g2'], lp['rwkv_k_k'], lp['rwkv_k_a'], lp['rwkv_r_k'])


def _pair_bd(x):
    h0 = lax.broadcasted_iota(jnp.int32, x.shape, 1) < HD
    return jnp.concatenate([jnp.where(h0, x, 0.0), jnp.where(h0, 0.0, x)], axis=0)


def _rwkv_scan_kernel(r_ref, v_ref, kk_ref, lw_ref, kd_ref, b_ref, s0_ref, y_ref, sf_ref,
                      s_ref, lrhs_ref, ly_ref, tinv_ref, v16_ref, ebt_ref, wc_ref):
    d = pl.program_id(0)
    blk = _scan_blk(d, pl.program_id(1))
    first, last = _scan_ends(d, blk)
    sgn = 1 - 2 * d
    C = RWKV_CHUNK
    nchunk = TB // C
    npair = BR // LANE
    ii = lax.broadcasted_iota(jnp.int32, (LANE, LANE), 0)
    jj = lax.broadcasted_iota(jnp.int32, (LANE, LANE), 1)
    same = (ii >> 6) == (jj >> 6)

    @pl.when(jnp.logical_and(first, blk < N_CTX))
    def _():
        s_ref[...] = jnp.zeros_like(s_ref)

    @pl.when(jnp.logical_and(first, blk >= N_CTX))
    def _():
        for p in range(npair):
            a = jnp.concatenate([s0_ref[0, 0, 2 * p], s0_ref[0, 0, 2 * p + 1]], axis=1)
            s_ref[p] = jnp.where(same, jnp.concatenate([a, a], axis=0).T, 0.0)

    ci = lax.broadcasted_iota(jnp.int32, (C, C), 0)
    cj = lax.broadcasted_iota(jnp.int32, (C, C), 1)
    tri = ((cj - ci) * sgn <= 0).astype(BF16)
    rel = ((jj & (C - 1)) - (ii & (C - 1))) * sgn
    before = jnp.logical_and(same, rel < 0)
    before_incl = jnp.logical_and(same, rel <= 0)

    def rows_of(c):
        cc = c + d * (nchunk - 1 - 2 * c)
        return pl.ds(pl.multiple_of(cc * C, C), C)

    eye = (ii == jj).astype(F32)
    blk16 = (ii >> 4) == (jj >> 4)
    pairs = range(npair)

    def decay(c):
        lw = lw_ref[0, rows_of(c), :]
        return lw, _cumsum_rows(tri, lw)

    def prep(c, lw, cum):
        rows = rows_of(c)
        tot = jnp.sum(lw, axis=0, keepdims=True)
        e_neg = jnp.exp(-cum)
        e_end = jnp.exp(tot - cum)
        r = r_ref[rows, :]
        v = v_ref[rows, :]
        kk = kk_ref[rows, :]
        kd = kd_ref[0, rows, :]
        b = b_ref[0, rows, :]
        k_t, b_t = kd * e_neg, b * e_neg
        kk_h, r_h = kk * jnp.exp(cum - lw), r * jnp.exp(cum)
        k_e, b_e = kd * e_end, b * e_end
        n_of = {}
        for p in pairs:
            sl = slice(p * LANE, (p + 1) * LANE)
            KK, R, KT, BT, V, KE, BE = (_pair_bd(x[:, sl]) for x in (kk_h, r_h, k_t, b_t, v, k_e, b_e))
            kk16, r16 = KK.astype(BF16), R.astype(BF16)
            g = _dot_nt(jnp.concatenate([kk16, r16], axis=0),
                        jnp.concatenate([BT, KT], axis=0).astype(BF16))
            n_of[c, p] = jnp.where(before, g[:LANE, :LANE], 0.0)
            a1 = jnp.where(before, g[:LANE, LANE:], 0.0)
            a4 = jnp.where(before_incl, g[LANE:, :LANE], 0.0)
            a3 = jnp.where(before_incl, g[LANE:, LANE:], 0.0)
            lrhs_ref[c, p] = jnp.concatenate([kk16, a1.astype(BF16)], axis=1)
            ly_ref[c, p] = jnp.concatenate([r16, a3.astype(BF16), a4.astype(BF16)], axis=1)
            v16_ref[c, p] = V.astype(BF16)
            ebt_ref[c, p] = jnp.concatenate([KE.T, BE.T], axis=1).astype(BF16)
            wc_ref[c, p] = jnp.exp(jnp.broadcast_to(tot[:, sl], (LANE, LANE)).T)
        return n_of

    def inverse_stages(n_of):
        chains = list(n_of)
        t = {}

        def level(name, f):
            def run():
                t[name] = {k: f(k) for k in chains}
            return run

        def store():
            for k in chains:
                tinv_ref[k] = t['tinv'][k].astype(BF16)

        return [
            level('dg', lambda k: jnp.where(blk16, n_of[k], 0.0)),
            level('low', lambda k: n_of[k] - t['dg'][k]),
            level('d2', lambda k: _dot16(t['dg'][k], t['dg'][k])),
            level('d4', lambda k: _dot16(t['d2'][k], t['d2'][k])),
            level('x1', lambda k: _dot16(eye - t['dg'][k], eye + t['d2'][k])),
            level('d8', lambda k: _dot16(t['d4'][k], t['d4'][k])),
            level('x2', lambda k: _dot16(t['x1'][k], eye + t['d4'][k])),
            level('xd', lambda k: _dot16(t['x2'][k], eye + t['d8'][k])),
            level('m', lambda k: _dot16(t['xd'][k], t['low'][k])),
            level('m2', lambda k: _dot16(t['m'][k], t['m'][k])),
            level('t1', lambda k: _dot16(eye - t['m'][k], eye + t['m2'][k])),
            level('tinv', lambda k: _dot16(t['t1'][k], t['xd'][k])),
            store,
        ]

    def state_stages(c):
        rows = rows_of(c)
        t = {}

        def load():
            t['s'] = [s_ref[p] for p in pairs]
            t['s16'] = [x.astype(BF16) for x in t['s']]
            t['v16'] = [v16_ref[c, p] for p in pairs]
            t['rhs'] = [_dot(lrhs_ref[c, p], jnp.concatenate([t['s16'][p], t['v16'][p]], axis=0)).astype(BF16)
                        for p in pairs]

        def solve():
            t['u16'] = [(-_dot(tinv_ref[c, p], t['rhs'][p])).astype(BF16) for p in pairs]

        def update():
            for p in pairs:
                s_ref[p] = t['s'][p] * wc_ref[c, p] + _dot(
                    ebt_ref[c, p], jnp.concatenate([t['v16'][p], t['u16'][p]], axis=0))

        def emit():
            for p in pairs:
                y = _dot(ly_ref[c, p], jnp.concatenate([t['s16'][p], t['v16'][p], t['u16'][p]], axis=0))
                y_ref[0, rows, p * LANE:(p + 1) * LANE] = y[:C] + y[C:]

        return [load, solve, update, emit]

    n_all = {}
    for c in range(nchunk):
        n_all.update(prep(c, *decay(c)))
    for stage in inverse_stages(n_all):
        stage()
    for c in range(nchunk):
        for stage in state_stages(c):
            stage()

    @pl.when(last)
    def _():
        for p in range(npair):
            sp = s_ref[p]
            folded = sp[:HD] + sp[HD:]
            z = jnp.concatenate([folded, folded], axis=0).T
            sf_ref[0, 0, 2 * p] = z[:HD, :HD]
            sf_ref[0, 0, 2 * p + 1] = z[HD:, :HD]


def _rwkv_scan(r, v, kk, lw, kd, b, s0):
    np_ = BR // LANE
    nchunk = TB // RWKV_CHUNK
    spec1 = pl.BlockSpec((TB, BR), lambda d, i: (_scan_blk(d, i), 0))
    spec2 = pl.BlockSpec((1, TB, BR), lambda d, i: (d, _scan_blk(d, i), 0))
    seq = lambda d, i: _blk_seq(_scan_blk(d, i))
    s0spec = pl.BlockSpec((1, 1, HEADS, HD, HD), lambda d, i: (_lat_of_seq(seq(d, i)), d, 0, 0, 0))
    sfspec = pl.BlockSpec((1, 1, HEADS, HD, HD), lambda d, i: (seq(d, i), d, 0, 0, 0))
    per = lambda rows, cols, dt: pltpu.VMEM((nchunk, np_, rows, cols), dt)
    return pl.pallas_call(
        _rwkv_scan_kernel,
        grid=(2, NBLK),
        in_specs=[spec1, spec1, spec1, spec2, spec2, spec2, s0spec],
        out_specs=[spec2, sfspec],
        out_shape=[jax.ShapeDtypeStruct((2, T_ALL, BR), F32),
                   jax.ShapeDtypeStruct((N_SEQ, 2, HEADS, HD, HD), F32)],
        scratch_shapes=[pltpu.VMEM((np_, LANE, LANE), F32),
                        per(LANE, 2 * LANE, BF16), per(LANE, 3 * LANE, BF16), per(LANE, LANE, BF16),
                        per(LANE, LANE, BF16), per(LANE, 2 * LANE, BF16), per(LANE, LANE, F32)],
        compiler_params=_cparams(("arbitrary", "arbitrary")),
        name="rwkv_scan",
    )(r, v, kk, lw, kd, b, s0)


def _ssd_prep_kernel(pp_ref, p_ref, pn_ref, cw_ref, cb_ref, dtb_ref, x_ref, bc_ref, dt_ref):
    i = pl.program_id(0)
    ext = _with_halo(pp_ref, p_ref, pn_ref, i)[:, BR:BR + SSD_XBC]
    acc = cb_ref[...] + cw_ref[0:1, :] * _shifted(ext, -(SSD_CONV // 2))
    for j in range(1, SSD_CONV):
        acc = acc + cw_ref[j:j + 1, :] * _shifted(ext, j - SSD_CONV // 2)
    xbc = _silu(acc)
    x_ref[...] = xbc[:, :BR]
    bc_ref[...] = xbc[:, BR:]
    dt_ref[...] = _softplus(p_ref[:, BR + SSD_XBC:] + dtb_ref[...])


def _ssd_prep(p_ssd, lp):
    return pl.pallas_call(
        _ssd_prep_kernel,
        grid=(NBLK,),
        in_specs=_halo_specs(SSD_PCOLS) + [_full((8, SSD_XBC)), _full((1, SSD_XBC)), _full((1, 2 * LANE))],
        out_specs=[pl.BlockSpec((TB, BR), lambda i: (i, 0)),
                   pl.BlockSpec((TB, 2 * LANE), lambda i: (i, 0)),
                   pl.BlockSpec((TB, 2 * LANE), lambda i: (i, 0))],
        out_shape=[jax.ShapeDtypeStruct((T_ALL, BR), F32),
                   jax.ShapeDtypeStruct((T_ALL, 2 * LANE), F32),
                   jax.ShapeDtypeStruct((T_ALL, 2 * LANE), F32)],
        compiler_params=_cparams(("parallel",)),
        name="ssd_prep",
    )(p_ssd, p_ssd, p_ssd, lp['ssd_conv_w8'], lp['ssd_conv_b'], lp['ssd_dt_bias_p'])


def _ssd_scan_kernel(x_ref, bc_ref, dt_ref, alog_ref, h0_ref, y_ref, hf_ref, hs_ref):
    d = pl.program_id(0)
    blk = _scan_blk(d, pl.program_id(1))
    first, last = _scan_ends(d, blk)
    sgn = 1 - 2 * d
    npair = BR // LANE
    half64 = lax.broadcasted_iota(jnp.int32, (HD, LANE), 1) < HD

    @pl.when(jnp.logical_and(first, blk < N_CTX))
    def _():
        hs_ref[...] = jnp.zeros_like(hs_ref)

    @pl.when(jnp.logical_and(first, blk >= N_CTX))
    def _():
        zero = jnp.zeros((HD, LANE), F32)
        blocks = []
        for q in range(npair):
            a = jnp.concatenate([h0_ref[0, 0, 2 * q], h0_ref[0, 0, 2 * q + 1]], axis=1)
            t = jnp.concatenate([a, a], axis=0).T
            blocks.append(jnp.where(half64, t[:HD], t[HD:]))
        rows = [jnp.concatenate([blocks[q] if q // 2 == g else zero for q in range(npair)], axis=1)
                for g in range(2)]
        hs_ref[...] = jnp.concatenate(rows, axis=0)

    x = x_ref[...]
    bm = bc_ref[:, :LANE]
    cm = bc_ref[:, LANE:]
    dt = dt_ref[...]
    a_neg = -jnp.exp(alog_ref[0])
    dta = dt * a_neg
    qi = lax.broadcasted_iota(jnp.int32, (TB, TB), 0)
    qj = lax.broadcasted_iota(jnp.int32, (TB, TB), 1)
    before_incl = (qj - qi) * sgn <= 0
    a_cum = _cumsum_rows(before_incl.astype(BF16), dta)
    tot = jnp.sum(dta, axis=0, keepdims=True)
    a_cum_t = a_cum.T
    dt_t = dt.T
    eh = lax.broadcasted_iota(jnp.int32, (LANE, BR), 0)
    ec = lax.broadcasted_iota(jnp.int32, (LANE, BR), 1)
    expand = ((ec >> 6) == eh).astype(BF16)
    e_in = _dot_exact_rhs(jnp.exp(a_cum), expand)
    to_end = _dot_exact_rhs(jnp.exp(tot - a_cum) * dt, expand)
    dec = _dot_exact_rhs(jnp.broadcast_to(jnp.exp(tot), (8, LANE)), expand)[0:1]
    hs = hs_ref[...]
    cb16, bb16, xb16 = cm.astype(BF16), bm.astype(BF16), x.astype(BF16)
    glane = lax.broadcasted_iota(jnp.int32, (TB, LANE), 1) >> 6
    half = lax.broadcasted_iota(jnp.int32, (TB, LANE), 1) < HD
    cbg = [_dot_nt(jnp.where(glane == g, cm, 0.0).astype(BF16), bb16) for g in range(2)]
    y_off = _dot(cb16, hs.astype(BF16)) * e_in
    upd = _dot_tn(bb16, (x * to_end).astype(BF16))
    scores = []
    for h in range(HEADS):
        seg = a_cum[:, h:h + 1] - a_cum_t[h:h + 1, :]
        decay = jnp.exp(jnp.where(before_incl, seg, -jnp.inf))
        scores.append((cbg[h // 4] * decay * dt_t[h:h + 1, :]).astype(BF16))
    y_heads = [_dot(scores[h], xb16[:, (h // 2) * LANE:(h // 2 + 1) * LANE]) for h in range(HEADS)]
    y_parts = [jnp.where(half, y_heads[2 * q], y_heads[2 * q + 1]) for q in range(npair)]
    y_ref[0] = jnp.concatenate(y_parts, axis=1) + y_off
    ui = lax.broadcasted_iota(jnp.int32, (LANE, BR), 0)
    uj = lax.broadcasted_iota(jnp.int32, (LANE, BR), 1)
    hs_ref[...] = hs * dec + jnp.where((ui >> 6) == (uj >> 8), upd, 0.0)

    @pl.when(last)
    def _():
        hn = hs_ref[...]
        for q in range(npair):
            g = q // 2
            w = hn[g * HD:(g + 1) * HD, q * LANE:(q + 1) * LANE]
            z = jnp.concatenate([w, w], axis=0).T
            hf_ref[0, 0, 2 * q] = z[:HD, :HD]
            hf_ref[0, 0, 2 * q + 1] = z[HD:, :HD]


def _ssd_scan(x, bc, dt, alog, h0):
    seq = lambda d, i: _blk_seq(_scan_blk(d, i))
    return pl.pallas_call(
        _ssd_scan_kernel,
        grid=(2, NBLK),
        in_specs=[pl.BlockSpec((TB, BR), lambda d, i: (_scan_blk(d, i), 0)),
                  pl.BlockSpec((TB, 2 * LANE), lambda d, i: (_scan_blk(d, i), 0)),
                  pl.BlockSpec((TB, LANE), lambda d, i: (_scan_blk(d, i), d)),
                  pl.BlockSpec((1, 1, LANE), lambda d, i: (d, 0, 0)),
                  pl.BlockSpec((1, 1, HEADS, HD, HD), lambda d, i: (_lat_of_seq(seq(d, i)), d, 0, 0, 0))],
        out_specs=[pl.BlockSpec((1, TB, BR), lambda d, i: (d, _scan_blk(d, i), 0)),
                   pl.BlockSpec((1, 1, HEADS, HD, HD), lambda d, i: (seq(d, i), d, 0, 0, 0))],
        out_shape=[jax.ShapeDtypeStruct((2, T_ALL, BR), F32),
                   jax.ShapeDtypeStruct((N_SEQ, 2, HEADS, HD, HD), F32)],
        scratch_shapes=[pltpu.VMEM((LANE, BR), F32)],
        compiler_params=_cparams(("arbitrary", "arbitrary")),
        name="ssd_scan",
    )(x, bc, dt, alog, h0)


def _gqa(q, kd, vd, sink_ref, bias):
    nq = q.shape[0]
    log2e = math.log2(math.e)
    half = lax.broadcasted_iota(jnp.int32, (nq, LANE), 1) < HD
    q = q * (ATT_SCALE * log2e)
    scores = []
    for g in range(ATT_KVH):
        rows = []
        for j in (2 * g, 2 * g + 1):
            q2 = q[:, j * LANE:(j + 1) * LANE]
            rows += [jnp.where(half, q2, 0.0), jnp.where(half, 0.0, q2)]
        qg = jnp.concatenate(rows, axis=0).astype(BF16)
        scores.append(_dot_nt(qg, kd[g].astype(BF16)))
    outs = []
    for g in range(ATT_KVH):
        s = scores[g]
        if bias is not None:
            kb = bias.shape[1]
            s = jnp.concatenate([s[:, :kb] + jnp.concatenate([bias] * ATT_GROUP, axis=0), s[:, kb:]], axis=1)
        sink = jnp.concatenate(
            [jnp.full((nq, 1), sink_ref[ATT_GROUP * g + h] * log2e, F32) for h in range(ATT_GROUP)], axis=0)
        mx = jnp.maximum(jnp.max(s, axis=-1, keepdims=True), sink)
        e = jnp.exp2(s - mx)
        inv = 1.0 / (jnp.sum(e, axis=-1, keepdims=True) + jnp.exp2(sink - mx))
        o = _dot(e.astype(BF16), vd[g].astype(BF16)) * inv
        outs += [jnp.where(half, o[0:nq], o[nq:2 * nq]), jnp.where(half, o[2 * nq:3 * nq], o[3 * nq:])]
    return jnp.concatenate(outs, axis=1)


def _ctx_attn_kernel(sink_ref, p_ref, o_ref):
    kd = [p_ref[:, BR + g * LANE:BR + (g + 1) * LANE] for g in range(ATT_KVH)]
    vd = [p_ref[:, BR + (2 + g) * LANE:BR + (3 + g) * LANE] for g in range(ATT_KVH)]
    o_ref[...] = _gqa(p_ref[:, :BR], kd, vd, sink_ref, None)


def _ctx_attn(p_att, sink):
    return pl.pallas_call(
        _ctx_attn_kernel,
        grid=(N_CTX,),
        in_specs=[pl.BlockSpec(memory_space=pltpu.SMEM), pl.BlockSpec((TB, ATT_PCOLS), lambda i: (i, 0))],
        out_specs=pl.BlockSpec((TB, BR), lambda i: (i, 0)),
        out_shape=jax.ShapeDtypeStruct((T_CTX, BR), F32),
        compiler_params=_cparams(("parallel",)),
        name="ctx_attn",
    )(sink, p_att)


def _rope(x, cos, sin_signed):
    lanes = x.shape[1]
    reps = lanes // LANE
    if reps > 1:
        cos = jnp.concatenate([cos] * reps, axis=1)
        sin_signed = jnp.concatenate([sin_signed] * reps, axis=1)
    lo = (lax.broadcasted_iota(jnp.int32, x.shape, 1) & 31) < 16
    partner = jnp.where(lo, pltpu.roll(x, lanes - 16, axis=1), pltpu.roll(x, 16, axis=1))
    return x * cos + partner * sin_signed


def _lat_attn_kernel(sink_ref, pq_ref, pp_ref, pn_ref, cq_ref, sq_ref, cp_ref, sp_ref, cn_ref, sn_ref,
                     kc_ref, vc_ref, o_ref):
    j = pl.program_id(1)
    hw = ATT_WINDOW
    q = _rope(pq_ref[:, :BR], cq_ref[...], sq_ref[...])
    kd, vd = [], []
    for g in range(ATT_KVH):
        kc, vc = BR + g * LANE, BR + (2 + g) * LANE
        kh, vh = g * LANE, (2 + g) * LANE
        kd.append(jnp.concatenate([_rope(pp_ref[:, kh:kh + LANE], cp_ref[...], sp_ref[...]),
                                   _rope(pq_ref[:, kc:kc + LANE], cq_ref[...], sq_ref[...]),
                                   _rope(pn_ref[:, kh:kh + LANE], cn_ref[...], sn_ref[...]),
                                   kc_ref[0, g]], axis=0))
        vd.append(jnp.concatenate([pp_ref[:, vh:vh + LANE], pq_ref[:, vc:vc + LANE], pn_ref[:, vh:vh + LANE],
                                   vc_ref[0, g]], axis=0))
    nloc = TB + 2 * hw
    qi = lax.broadcasted_iota(jnp.int32, (TB, nloc), 0)
    kj = lax.broadcasted_iota(jnp.int32, (TB, nloc), 1)
    rel = kj - hw - qi
    kpos = j * TB - hw + kj
    valid = (rel <= ATT_WINDOW) & (rel >= -ATT_WINDOW) & (kpos >= 0) & (kpos < LAT_LEN)
    o_ref[...] = _gqa(q, kd, vd, sink_ref, jnp.where(valid, 0.0, -jnp.inf))


def _lat_attn(p_att, sink, rope_cos, rope_sin, k_ctx, v_ctx):
    nb = LAT_LEN // TB
    hw = ATT_WINDOW
    per = TB // hw
    qrow = lambda b, j: N_CTX + b * nb + j
    hrow = lambda b, j: per * (N_CTX + b * nb)
    prv = lambda j: jnp.maximum(per * j - 1, 0)
    nxt = lambda j: jnp.minimum(per * (j + 1), per * nb - 1)
    hspec = lambda f: pl.BlockSpec((hw, 4 * LANE), lambda b, j: (hrow(b, j) + f(j), 1))
    tspec = lambda rows, f: pl.BlockSpec((rows, LANE), lambda b, j: (f(j), 0))
    cspec = pl.BlockSpec((1, ATT_KVH, PAST_LEN, LANE), lambda b, j: (b, 0, 0, 0))
    return pl.pallas_call(
        _lat_attn_kernel,
        grid=(N_LAT, nb),
        in_specs=[pl.BlockSpec(memory_space=pltpu.SMEM),
                  pl.BlockSpec((TB, ATT_PCOLS), lambda b, j: (qrow(b, j), 0)), hspec(prv), hspec(nxt),
                  tspec(TB, lambda j: j), tspec(TB, lambda j: j), tspec(hw, prv), tspec(hw, prv),
                  tspec(hw, nxt), tspec(hw, nxt), cspec, cspec],
        out_specs=pl.BlockSpec((TB, BR), lambda b, j: (b * nb + j, 0)),
        out_shape=jax.ShapeDtypeStruct((N_LAT * LAT_LEN, BR), F32),
        compiler_params=_cparams(("parallel", "parallel")),
        name="lat_attn",
    )(sink, p_att, p_att, p_att, rope_cos, rope_sin, rope_cos, rope_sin, rope_cos, rope_sin, k_ctx, v_ctx)


def _merge_kernel(h_ref, mod_ref, nw_ref, yr_ref, bonus_ref, g_ref, lnw_ref, lnb_ref, ys_ref, xs_ref, z_ref,
                  dvec_ref, snw_ref, yac_ref, yal_ref, wg_ref, wb_ref, wo_ref, o_ref):
    m = mod_ref[0]
    h = h_ref[...]
    pg = _dot(_normmod(h, nw_ref[...], m[:, :D_MODEL], m[:, D_MODEL:2 * D_MODEL]).astype(BF16), wg_ref[0])
    y = yr_ref[0] + yr_ref[1] + bonus_ref[...]
    mu = _seg_sum(y, HD) * (1.0 / HD)
    yc = y - mu
    var = _seg_sum(yc * yc, HD) * (1.0 / HD)
    y_a = (yc * lax.rsqrt(var + RWKV_GN_EPS) * lnw_ref[...] + lnb_ref[...]) * g_ref[...]
    y = (ys_ref[0] + ys_ref[1] + dvec_ref[...] * xs_ref[...]) * _silu(z_ref[...])
    y_b = y * lax.rsqrt(jnp.mean(y * y, axis=-1, keepdims=True) + NORM_EPS) * snw_ref[...]
    y_c = jnp.where(pl.program_id(0) < N_CTX, yac_ref[...], yal_ref[...])
    wide = [_dot(br.astype(BF16), wb_ref[0, n]) for n, br in enumerate((y_a, y_b, y_c))]
    merged = None
    for n in range(3):
        term = _sigmoid(pg[:, n * D_MODEL:(n + 1) * D_MODEL]) * wide[n]
        merged = term if merged is None else merged + term
    g1 = m[:, 2 * D_MODEL:3 * D_MODEL]
    o_ref[...] = h + g1 * _dot(merged.astype(BF16), wo_ref[0])


def _merge(h, mod, yr, bonus, g, ys, xs, p_ssd, ya_ctx, ya_lat, lp, w_gate, w_branch, w_out, l):
    tm = TB
    row = lambda w: pl.BlockSpec((tm, w), lambda i: (i, 0))
    row2 = pl.BlockSpec((2, tm, BR), lambda i: (0, i, 0))
    return pl.pallas_call(
        _merge_kernel,
        grid=(T_ALL // tm,),
        in_specs=[row(D_MODEL), pl.BlockSpec((1, 1, 6 * D_MODEL), lambda i: (_blk_type(i), 0, 0)),
                  _full((1, D_MODEL)),
                  row2, row(BR), row(BR), _full((1, BR)), _full((1, BR)),
                  row2, row(BR), row(BR), _full((1, BR)), _full((1, BR)),
                  pl.BlockSpec((tm, BR), lambda i: (jnp.minimum(i, N_CTX - 1), 0)),
                  pl.BlockSpec((tm, BR), lambda i: (jnp.maximum(i - N_CTX, 0), 0)),
                  pl.BlockSpec((1, D_MODEL, GATE_COLS), lambda i: (l, 0, 0)),
                  pl.BlockSpec((1, 3, BR, D_MODEL), lambda i: (l, 0, 0, 0)),
                  pl.BlockSpec((1, D_MODEL, D_MODEL), lambda i: (l, 0, 0))],
        out_specs=row(D_MODEL),
        out_shape=jax.ShapeDtypeStruct((T_ALL, D_MODEL), F32),
        compiler_params=_cparams(("parallel",)),
        name="merge",
    )(h, mod, lp['norm1_w'], yr, bonus, g, lp['rwkv_lnx_w'], lp['rwkv_lnx_b'], ys, xs, p_ssd, lp['ssd_d_vec'],
      lp['ssd_norm_w'], ya_ctx, ya_lat, w_gate, w_branch, w_out)


def _route_kernel(h_ref, mod_ref, nw_ref, wr_ref, xn_ref, gate_ref, slot_ref, slot_t_ref, cnt_ref):
    m = mod_ref[0]
    xn = _normmod(h_ref[...], nw_ref[...], m[:, 3 * D_MODEL:4 * D_MODEL], m[:, 4 * D_MODEL:5 * D_MODEL])
    xn_ref[...] = xn.astype(BF16)
    logits = _dot_split(xn, wr_ref[...])
    lane_i = lax.broadcasted_iota(jnp.int32, logits.shape, 1)
    lane = lane_i.astype(F32)
    lane_grp = (lane_i >> 3).astype(F32)
    neg = -jnp.inf
    big = float(LANE)
    is_g = (lane_i >= MOE_EXPERTS) & (lane_i < MOE_EXPERTS + MOE_GROUPS)
    gl = jnp.where(is_g, logits, neg)
    gmax = jnp.max(gl, axis=-1, keepdims=True)
    gsel = jnp.min(jnp.where(gl == gmax, lane - MOE_EXPERTS, big), axis=-1, keepdims=True)
    g_w = 1.0 / jnp.sum(jnp.where(is_g, jnp.exp(gl - gmax), 0.0), axis=-1, keepdims=True)
    el = jnp.where((lane_i < MOE_EXPERTS) & (lane_grp == gsel), logits, neg)
    m1 = jnp.max(el, axis=-1, keepdims=True)
    i1 = jnp.min(jnp.where(el == m1, lane, big), axis=-1, keepdims=True)
    el2 = jnp.where(lane == i1, neg, el)
    m2 = jnp.max(el2, axis=-1, keepdims=True)
    i2 = jnp.min(jnp.where(el2 == m2, lane, big), axis=-1, keepdims=True)
    e2 = jnp.exp(m2 - m1)
    w1 = 1.0 / (1.0 + e2)
    gate_ref[...] = jnp.where(lane == i1, w1 * g_w, jnp.where(lane == i2, e2 * w1 * g_w, 0.0))
    member = lane == gsel
    ti = lax.broadcasted_iota(jnp.int32, (TB, TB), 0)
    tj = lax.broadcasted_iota(jnp.int32, (TB, TB), 1)
    rank = _dot((tj <= ti).astype(BF16), jnp.where(member, 1.0, 0.0).astype(BF16))
    slot = jnp.where(member, rank - 1.0, -1.0)
    slot_ref[...] = slot
    slot_t_ref[...] = slot.T[:8]
    cnt_ref[0] = jnp.max(slot, axis=0, keepdims=True) + 1.0


def _route(h, mod, nw, wr):
    tm = TB
    return pl.pallas_call(
        _route_kernel,
        grid=(T_ALL // tm,),
        in_specs=[pl.BlockSpec((tm, D_MODEL), lambda i: (i, 0)),
                  pl.BlockSpec((1, 1, 6 * D_MODEL), lambda i: (_blk_type(i), 0, 0)),
                  _full((1, D_MODEL)), _full((D_MODEL, LANE))],
        out_specs=[pl.BlockSpec((tm, D_MODEL), lambda i: (i, 0)), pl.BlockSpec((tm, LANE), lambda i: (i, 0)),
                   pl.BlockSpec((tm, LANE), lambda i: (i, 0)), pl.BlockSpec((8, tm), lambda i: (0, i)),
                   pl.BlockSpec((1, 1, LANE), lambda i: (i, 0, 0))],
        out_shape=[jax.ShapeDtypeStruct((T_ALL, D_MODEL), BF16), jax.ShapeDtypeStruct((T_ALL, LANE), F32),
                   jax.ShapeDtypeStruct((T_ALL, LANE), F32), jax.ShapeDtypeStruct((8, T_ALL), F32),
                   jax.ShapeDtypeStruct((NBLK, 1, LANE), F32)],
        compiler_params=_cparams(("parallel",)),
        name="moe_route",
    )(h, mod, nw, wr)


MOE_EB = 4
MOE_TM = 4 * TB
MOE_CAP = 96
MOE_ROWS = (MOE_TM // TB) * MOE_CAP


def _expert_block(x, gate, lane0, w1_ref, w3_ref, w2_ref):
    lane = lax.broadcasted_iota(jnp.int32, gate.shape, 1)
    h1 = [_dot(x, w1_ref[0, k]) for k in range(MOE_EB)]
    h3 = [_dot(x, w3_ref[0, k]) for k in range(MOE_EB)]
    acts = []
    for k in range(MOE_EB):
        gcol = jnp.sum(jnp.where(lane == lane0 + k, gate, 0.0), axis=-1, keepdims=True)
        acts.append((_silu(h1[k]) * h3[k] * gcol).astype(BF16))
    return _dot(jnp.concatenate(acts, axis=1), w2_ref[0].reshape(MOE_EB * D_EXPERT, D_MODEL))


def _experts_kernel(cnt_ref, xn_ref, gate_ref, slot_ref, slot_t_ref, w1_ref, w3_ref, w2_ref, h_ref, mod_ref,
                    o_ref, xg_ref, gg_ref, yg_ref):
    i = pl.program_id(0)
    hg = pl.program_id(1)
    g = hg // 2
    nsub = MOE_TM // TB
    fits = cnt_ref[i * nsub, g] <= MOE_CAP
    for sb in range(1, nsub):
        fits = jnp.logical_and(fits, cnt_ref[i * nsub + sb, g] <= MOE_CAP)
    lane0 = hg * MOE_EB

    @pl.when(hg == 0)
    def _():
        o_ref[...] = jnp.zeros_like(o_ref)

    @pl.when(jnp.logical_and(fits, hg % 2 == 0))
    def _():
        sub8 = lax.broadcasted_iota(jnp.int32, (8, MOE_TM), 0)
        srow = jnp.max(jnp.where(sub8 == g, slot_t_ref[...], -2.0), axis=0, keepdims=True)
        rid = lax.broadcasted_iota(jnp.int32, (MOE_CAP, TB), 0).astype(F32)
        for sb in range(nsub):
            tok = slice(sb * TB, (sb + 1) * TB)
            sel = jnp.where(rid == srow[:, tok], 1.0, 0.0).astype(BF16)
            xg_ref[sb * MOE_CAP:(sb + 1) * MOE_CAP] = _dot(sel, xn_ref[tok, :]).astype(BF16)
            gh, gl = _split16(gate_ref[tok, :])
            gg = _dot(sel, jnp.concatenate([gh, gl], axis=1))
            gg_ref[sb * MOE_CAP:(sb + 1) * MOE_CAP] = gg[:, :LANE] + gg[:, LANE:]
        yg_ref[...] = _expert_block(xg_ref[...], gg_ref[...], lane0, w1_ref, w3_ref, w2_ref)

    @pl.when(jnp.logical_and(fits, hg % 2 == 1))
    def _():
        yg_ref[...] += _expert_block(xg_ref[...], gg_ref[...], lane0, w1_ref, w3_ref, w2_ref)
        lane = lax.broadcasted_iota(jnp.int32, (TB, LANE), 1)
        cid = lax.broadcasted_iota(jnp.int32, (TB, 2 * MOE_CAP), 1)
        cid = jnp.where(cid < MOE_CAP, cid, cid - MOE_CAP).astype(F32)
        for sb in range(nsub):
            tok = slice(sb * TB, (sb + 1) * TB)
            scol = jnp.max(jnp.where(lane == g, slot_ref[tok, :], -2.0), axis=-1, keepdims=True)
            sel2 = jnp.where(cid == scol, 1.0, 0.0).astype(BF16)
            yh, yl = _split16(yg_ref[sb * MOE_CAP:(sb + 1) * MOE_CAP])
            o_ref[tok, :] += _dot(sel2, jnp.concatenate([yh, yl], axis=0))

    @pl.when(jnp.logical_not(fits))
    def _():
        o_ref[...] += _expert_block(xn_ref[...], gate_ref[...], lane0, w1_ref, w3_ref, w2_ref)

    @pl.when(hg == 2 * MOE_GROUPS - 1)
    def _():
        g2 = mod_ref[0][:, 5 * D_MODEL:6 * D_MODEL]
        o_ref[...] = h_ref[...] + g2 * o_ref[...]


def _experts(cnt, xn, gate, slot, slot_t, w1, w3, w2, h, mod, l):
    tm = MOE_TM
    grid_spec = pltpu.PrefetchScalarGridSpec(
        num_scalar_prefetch=1,
        grid=(T_ALL // tm, MOE_EXPERTS // MOE_EB),
        in_specs=[pl.BlockSpec((tm, D_MODEL), lambda i, e, c: (i, 0)),
                  pl.BlockSpec((tm, LANE), lambda i, e, c: (i, 0)),
                  pl.BlockSpec((tm, LANE), lambda i, e, c: (i, 0)),
                  pl.BlockSpec((8, tm), lambda i, e, c: (0, i)),
                  pl.BlockSpec((1, MOE_EB, D_MODEL, D_EXPERT), lambda i, e, c: (l, e, 0, 0)),
                  pl.BlockSpec((1, MOE_EB, D_MODEL, D_EXPERT), lambda i, e, c: (l, e, 0, 0)),
                  pl.BlockSpec((1, MOE_EB, D_EXPERT, D_MODEL), lambda i, e, c: (l, e, 0, 0)),
                  pl.BlockSpec((tm, D_MODEL), lambda i, e, c: (i, 0)),
                  pl.BlockSpec((1, 1, 6 * D_MODEL), lambda i, e, c: (_blk_type((tm // TB) * i), 0, 0))],
        out_specs=pl.BlockSpec((tm, D_MODEL), lambda i, e, c: (i, 0)),
        scratch_shapes=[pltpu.VMEM((MOE_ROWS, D_MODEL), BF16), pltpu.VMEM((MOE_ROWS, LANE), F32),
                        pltpu.VMEM((MOE_ROWS, D_MODEL), F32)])
    return pl.pallas_call(
        _experts_kernel,
        grid_spec=grid_spec,
        out_shape=jax.ShapeDtypeStruct((T_ALL, D_MODEL), F32),
        compiler_params=_cparams(("parallel", "arbitrary")),
        name="moe_experts",
    )(cnt, xn, gate, slot, slot_t, w1, w3, w2, h, mod)


def _final_norm_kernel(h_ref, w_ref, o_ref):
    x = h_ref[...]
    o_ref[...] = x * lax.rsqrt(jnp.mean(x * x, axis=-1, keepdims=True) + NORM_EPS) * w_ref[...]


def _final_norm(h, w, row0, rows, name):
    tm = 2 * TB
    return pl.pallas_call(
        _final_norm_kernel,
        grid=(rows // tm,),
        in_specs=[pl.BlockSpec((tm, D_MODEL), lambda i: (row0 // tm + i, 0)), _full((1, D_MODEL))],
        out_specs=pl.BlockSpec((tm, D_MODEL), lambda i: (i, 0)),
        out_shape=jax.ShapeDtypeStruct((rows, D_MODEL), F32),
        compiler_params=_cparams(("parallel",)),
        name=name,
    )(h, w)


def _rope_tables():
    pos = jnp.arange(LAT_LEN)
    row = (pos // GRID_W).astype(F32)
    col = (pos % GRID_W).astype(F32)
    half = HD // 2
    inv = 1.0 / (ROPE_BASE ** (jnp.arange(0, half, 2, dtype=F32) / half))
    ar, ac = row[:, None] * inv[None, :], col[:, None] * inv[None, :]
    cos = jnp.concatenate([jnp.cos(ar), jnp.cos(ar), jnp.cos(ac), jnp.cos(ac)], axis=1)
    sin = jnp.concatenate([-jnp.sin(ar), jnp.sin(ar), -jnp.sin(ac), jnp.sin(ac)], axis=1)
    return jnp.tile(cos, (1, 2)), jnp.tile(sin, (1, 2))


def _dup_heads(x):
    x = jnp.swapaxes(x, -2, -3)
    return jnp.concatenate([x, x], axis=-1)


def _proj_weights(w_in):
    o_ssd, o_att = RWKV_COLS, RWKV_COLS + 1296
    o_gate = o_att + BR + 2 * ATT_KVH * HD
    pad = jnp.zeros((DEPTH, D_MODEL, LANE - HEADS), F32)
    dt0 = o_ssd + BR + SSD_XBC
    w_ssd = jnp.concatenate([w_in[:, :, o_ssd:dt0], w_in[:, :, dt0:dt0 + HEADS], pad,
                             w_in[:, :, dt0 + HEADS:dt0 + 2 * HEADS], pad], axis=2)
    kv = [w_in[:, :, o_att + BR + n * HD:o_att + BR + (n + 1) * HD] for n in range(2 * ATT_KVH)]
    w_att = jnp.concatenate([w_in[:, :, o_att:o_att + BR]] + [x for c in kv for x in (c, c)], axis=2)
    return (w_in[:, :, :o_ssd].astype(BF16), w_ssd.astype(BF16), w_att.astype(BF16),
            w_in[:, :, o_gate:].astype(BF16))


def _layer_params(l, prm):
    lp = {}
    row = lambda name: prm[name][l].reshape(1, -1)
    for name in ('norm1_w', 'norm2_w', 'rwkv_mu', 'rwkv_k_k', 'rwkv_k_a', 'rwkv_r_k', 'rwkv_lnx_w', 'rwkv_lnx_b',
                 'ssd_conv_b', 'ssd_norm_w'):
        lp[name] = row(name)
    lp['rwkv_w0'] = prm['rwkv_w0'][l]
    lp['rwkv_a0'] = prm['rwkv_a0'][l]
    z64 = jnp.zeros((HD, BR), F32)
    w2, a2 = prm['rwkv_w2'][l], prm['rwkv_a2'][l]
    lp['rwkv_w2p'] = jnp.stack([jnp.concatenate([w2[0], z64]), jnp.concatenate([z64, w2[1]])])
    lp['rwkv_a2p'] = jnp.stack([jnp.concatenate([a2[0], z64]), jnp.concatenate([z64, a2[1]])])
    lp['rwkv_g2'] = prm['rwkv_g2'][l]
    lp['ssd_conv_w8'] = jnp.concatenate([prm['ssd_conv_w'][l], jnp.zeros((8 - SSD_CONV, SSD_XBC), F32)])
    dtb = prm['ssd_dt_bias'][l]
    zp = jnp.zeros((LANE - HEADS,), F32)
    lp['ssd_dt_bias_p'] = jnp.concatenate([dtb[0], zp, dtb[1], zp]).reshape(1, 2 * LANE)
    lp['ssd_a_log_p'] = jnp.pad(prm['ssd_a_log'][l], ((0, 0), (0, LANE - HEADS))).reshape(2, 1, LANE)
    lp['ssd_d_vec'] = jnp.repeat(prm['ssd_d'][l], HD).reshape(1, BR)
    lp['att_sink'] = prm['att_sink'][l]
    wr = jnp.concatenate([prm['moe_w_expert'][l].reshape(D_MODEL, MOE_EXPERTS), prm['moe_w_group'][l]], axis=1)
    lp['w_route'] = jnp.pad(wr, ((0, 0), (0, LANE - MOE_EXPERTS - MOE_GROUPS)))
    return lp


def kernel(x_prompt, x_sample, cache_attn_k, cache_attn_v, state_rwkv, state_ssd, c, c_ctx, w_ada, b_ada, norm1_w, norm2_w, w_in, rwkv_mu, rwkv_w0, rwkv_w2, rwkv_a0, rwkv_a2, rwkv_g2, rwkv_k_k, rwkv_k_a, rwkv_r_k, rwkv_lnx_w, rwkv_lnx_b, ssd_conv_w, ssd_conv_b, ssd_dt_bias, ssd_a_log, ssd_d, ssd_norm_w, att_sink, w_branch, w_out, moe_w_group, moe_w_expert, moe_w1, moe_w3, moe_w2, final_norm_w):
    prm = dict(norm1_w=norm1_w, norm2_w=norm2_w, rwkv_mu=rwkv_mu, rwkv_w0=rwkv_w0, rwkv_w2=rwkv_w2,
               rwkv_a0=rwkv_a0, rwkv_a2=rwkv_a2, rwkv_g2=rwkv_g2, rwkv_k_k=rwkv_k_k, rwkv_k_a=rwkv_k_a,
               rwkv_r_k=rwkv_r_k.reshape(DEPTH, BR), rwkv_lnx_w=rwkv_lnx_w, rwkv_lnx_b=rwkv_lnx_b,
               ssd_conv_w=ssd_conv_w, ssd_conv_b=ssd_conv_b, ssd_dt_bias=ssd_dt_bias, ssd_a_log=ssd_a_log,
               ssd_d=ssd_d, ssd_norm_w=ssd_norm_w, att_sink=att_sink,
               moe_w_group=moe_w_group, moe_w_expert=moe_w_expert)
    w_rwkv, w_ssd, w_att, w_gate = _proj_weights(w_in)
    wb16, wo16 = w_branch.astype(BF16), w_out.astype(BF16)
    w1_16, w3_16, w2_16 = moe_w1.astype(BF16), moe_w3.astype(BF16), moe_w2.astype(BF16)
    kc_dup, vc_dup = _dup_heads(cache_attn_k), _dup_heads(cache_attn_v)
    cond8 = jnp.concatenate([c_ctx[None, :], c, jnp.zeros((8 - 1 - N_LAT, D_MODEL), F32)], axis=0)
    mod_all = _adaln(cond8, w_ada, b_ada)
    rope_cos, rope_sin = _rope_tables()
    h = jnp.concatenate([x_prompt.reshape(T_CTX, D_MODEL), x_sample.reshape(N_LAT * LAT_LEN, D_MODEL)], axis=0)
    ks, vs, srs, sss = [], [], [], []
    for l in range(DEPTH):
        lp = _layer_params(l, prm)
        mod = mod_all[l].reshape(8, 1, 6 * D_MODEL)
        p_rwkv = _normproj(h, mod, lp['norm1_w'], w_rwkv, l, RWKV_COLS, "proj_rwkv")
        p_ssd = _normproj(h, mod, lp['norm1_w'], w_ssd, l, SSD_PCOLS, "proj_ssd")
        p_att = _normproj(h, mod, lp['norm1_w'], w_att, l, ATT_PCOLS, "proj_att")
        r, v, kk, g, bonus, lw, kd, b = _rwkv_prep(p_rwkv, lp)
        yr, sf = _rwkv_scan(r, v, kk, lw, kd, b, state_rwkv[:, l])
        srs.append(sf)
        xs, bc, dt = _ssd_prep(p_ssd, lp)
        ys, hf = _ssd_scan(xs, bc, dt, lp['ssd_a_log_p'], state_ssd[:, l])
        sss.append(hf)
        ya_ctx = _ctx_attn(p_att, lp['att_sink'])
        ya_lat = _lat_attn(p_att, lp['att_sink'], rope_cos, rope_sin, kc_dup[:, l], vc_dup[:, l])
        kv = [p_att[:T_CTX, BR + n * LANE:BR + n * LANE + HD].reshape(N_CTX, TB, HD) for n in range(2 * ATT_KVH)]
        ks.append(jnp.stack(kv[:ATT_KVH], axis=2))
        vs.append(jnp.stack(kv[ATT_KVH:], axis=2))
        h = _merge(h, mod, yr, bonus, g, ys, xs, p_ssd, ya_ctx, ya_lat, lp, w_gate, wb16, wo16, l)
        xn, gate, slot, slot_t, cnt = _route(h, mod, lp['norm2_w'], lp['w_route'])
        cnt = cnt[:, 0, :MOE_GROUPS].astype(jnp.int32)
        h = _experts(cnt, xn, gate, slot, slot_t, w1_16, w3_16, w2_16, h, mod, l)
    fw = final_norm_w.reshape(1, D_MODEL)
    y_ctx = _final_norm(h, fw, 0, T_CTX, "final_norm_ctx")
    y_lat = _final_norm(h, fw, T_CTX, N_LAT * LAT_LEN, "final_norm_lat")
    return (y_ctx.reshape(N_CTX, TB, D_MODEL), y_lat.reshape(N_LAT, LAT_LEN, D_MODEL),
            jnp.stack(ks, axis=1), jnp.stack(vs, axis=1),
            jnp.stack(srs, axis=1)[:N_CTX], jnp.stack(sss, axis=1)[:N_CTX])
```

```python
import math

import jax
import jax.numpy as jnp
from jax import lax
from jax.experimental import pallas as pl
from jax.experimental.pallas import tpu as pltpu

F32 = jnp.float32
BF16 = jnp.bfloat16
HIGHEST = lax.Precision.HIGHEST

D_MODEL = 1024
DEPTH = 4
N_CTX = 16
N_LAT = 2
LAT_LEN = 4096
PAST_LEN = 512
GRID_W = 64
NORM_EPS = 1e-6
TB = 256
NLB = LAT_LEN // TB
NBLK = N_CTX + N_LAT * NLB
T_CTX = N_CTX * TB
T_ALL = NBLK * TB
N_SEQ = N_CTX + N_LAT
HALO = 8

HEADS = 8
HD = 64
BR = HEADS * HD
RWKV_COLS = 1920
RWKV_CHUNK = 64
RWKV_GN_EPS = 64e-5
SSD_CONV = 5
SSD_XBC = 768
SSD_PCOLS = 1536
ATT_KVH = 2
ATT_GROUP = 4
ATT_PCOLS = BR + 4 * 2 * HD
ATT_WINDOW = 128
ATT_SCALE = HD ** -0.5
ROPE_BASE = 10000.0
GATE_COLS = 3 * D_MODEL
MOE_GROUPS = 4
MOE_EXPERTS = 32
D_EXPERT = 256
LANE = 128

VMEM_LIMIT = 48 * 1024 * 1024


def _cparams(sem):
    return pltpu.CompilerParams(dimension_semantics=sem, vmem_limit_bytes=VMEM_LIMIT)


def _dot(a, b, precision=None):
    return jnp.dot(a, b, precision=precision, preferred_element_type=F32)


def _dot_nt(a, b):
    return lax.dot_general(a, b, (((1,), (1,)), ((), ())), preferred_element_type=F32)


def _dot_tn(a, b):
    return lax.dot_general(a, b, (((0,), (0,)), ((), ())), preferred_element_type=F32)


def _split16(x):
    hi = x.astype(BF16)
    return hi, (x - hi.astype(F32)).astype(BF16)


def _dot16(a, b):
    return _dot(a.astype(BF16), b.astype(BF16))


def _dot_split(a, b):
    ah, al = _split16(a)
    bh, bl = _split16(b)
    return _dot(jnp.concatenate([ah, al, ah], axis=1), jnp.concatenate([bh, bh, bl], axis=0))


def _dot_exact_rhs(a, b16):
    ah, al = _split16(a)
    return _dot(jnp.concatenate([ah, al], axis=1), jnp.concatenate([b16, b16], axis=0))


def _cumsum_rows(tri16, x):
    x0 = x.astype(BF16)
    r1 = x - x0.astype(F32)
    x1 = r1.astype(BF16)
    x2 = (r1 - x1.astype(F32)).astype(BF16)
    return _dot(tri16, x0) + _dot(tri16, x1) + _dot(tri16, x2)


def _sigmoid(x):
    return 1.0 / (1.0 + jnp.exp(-x))


def _silu(x):
    return x * _sigmoid(x)


def _softplus(x):
    return jnp.maximum(x, 0.0) + jnp.log(1.0 + jnp.exp(-jnp.abs(x)))


def _full(shape):
    return pl.BlockSpec(shape, lambda *_: (0,) * len(shape))


def _blk_type(i):
    return jnp.where(i < N_CTX, 0, 1 + (i - N_CTX) // NLB)


def _blk_seq(i):
    return jnp.where(i < N_CTX, i, N_CTX + (i - N_CTX) // NLB)


def _blk_first(i):
    return jnp.logical_or(i < N_CTX, (i - N_CTX) % NLB == 0)


def _blk_last(i):
    return jnp.logical_or(i < N_CTX, (i - N_CTX) % NLB == NLB - 1)


def _scan_blk(d, i):
    return i + d * (NBLK - 1 - 2 * i)


def _scan_ends(d, blk):
    fwd = d == 0
    a, b = _blk_first(blk), _blk_last(blk)
    first = jnp.logical_or(jnp.logical_and(fwd, a), jnp.logical_and(jnp.logical_not(fwd), b))
    last = jnp.logical_or(jnp.logical_and(fwd, b), jnp.logical_and(jnp.logical_not(fwd), a))
    return first, last


def _lat_of_seq(seq):
    return jnp.clip(seq - N_CTX, 0, N_LAT - 1)


def _adaln_kernel(c_ref, w_ref, b_ref, o_ref):
    o_ref[0] = _dot(_silu(c_ref[...]), w_ref[0], HIGHEST) + b_ref[0]


def _adaln(cond8, w_ada, b_ada):
    return pl.pallas_call(
        _adaln_kernel,
        grid=(DEPTH, 6),
        in_specs=[pl.BlockSpec((8, D_MODEL), lambda l, j: (0, 0)),
                  pl.BlockSpec((1, D_MODEL, D_MODEL), lambda l, j: (l, 0, j)),
                  pl.BlockSpec((1, 1, D_MODEL), lambda l, j: (l, 0, j))],
        out_specs=pl.BlockSpec((1, 8, D_MODEL), lambda l, j: (l, 0, j)),
        out_shape=jax.ShapeDtypeStruct((DEPTH, 8, 6 * D_MODEL), F32),
        compiler_params=_cparams(("parallel", "parallel")),
        name="adaln",
    )(cond8, w_ada, b_ada.reshape(DEPTH, 1, 6 * D_MODEL))


def _normmod(x, nw, shift, scale):
    y = x * lax.rsqrt(jnp.mean(x * x, axis=-1, keepdims=True) + NORM_EPS) * nw
    return y * (1.0 + scale) + shift


def _normproj_kernel(h_ref, mod_ref, nw_ref, w_ref, o_ref, xn_ref):
    @pl.when(pl.program_id(1) == 0)
    def _():
        m = mod_ref[0]
        xn_ref[...] = _normmod(h_ref[...], nw_ref[...], m[:, :D_MODEL], m[:, D_MODEL:2 * D_MODEL]).astype(BF16)
    o_ref[...] = _dot(xn_ref[...], w_ref[0])


def _normproj(h, mod, nw, w, l, tn, name):
    tm = 4 * TB
    n = w.shape[2]
    return pl.pallas_call(
        _normproj_kernel,
        grid=(T_ALL // tm, n // tn),
        in_specs=[pl.BlockSpec((tm, D_MODEL), lambda i, j: (i, 0)),
                  pl.BlockSpec((1, 1, 6 * D_MODEL), lambda i, j: (_blk_type((tm // TB) * i), 0, 0)),
                  pl.BlockSpec((1, D_MODEL), lambda i, j: (0, 0)),
                  pl.BlockSpec((1, D_MODEL, tn), lambda i, j: (l, 0, j))],
        out_specs=pl.BlockSpec((tm, tn), lambda i, j: (i, j)),
        out_shape=jax.ShapeDtypeStruct((T_ALL, n), F32),
        scratch_shapes=[pltpu.VMEM((tm, D_MODEL), BF16)],
        compiler_params=_cparams(("parallel", "arbitrary")),
        name=name,
    )(h, mod, nw, w)


def _with_halo(prev_ref, cur_ref, next_ref, i):
    prev = jnp.where(_blk_first(i), 0.0, prev_ref[...])
    nxt = jnp.where(_blk_last(i), 0.0, next_ref[...])
    return jnp.concatenate([prev, cur_ref[...], nxt], axis=0)


def _shifted(ext, s):
    n = ext.shape[0]
    return pltpu.roll(ext, (-s) % n, axis=0)[HALO:HALO + TB]


def _seg_sum(x, width):
    ii = lax.broadcasted_iota(jnp.int32, (LANE, LANE), 0)
    jj = lax.broadcasted_iota(jnp.int32, (LANE, LANE), 1)
    shift = int(math.log2(width))
    ones = ((ii >> shift) == (jj >> shift)).astype(BF16)
    parts = [_dot_exact_rhs(x[:, c:c + LANE], ones) for c in range(0, x.shape[1], LANE)]
    return jnp.concatenate(parts, axis=1)


def _rwkv_prep_kernel(pp_ref, p_ref, pn_ref, mu_ref, w0_ref, w2_ref, a0_ref, a2_ref, g2_ref, kk_w_ref, ka_ref,
                      rk_ref, r_ref, v_ref, kk_ref, g_ref, bonus_ref, lw_ref, kd_ref, b_ref):
    i = pl.program_id(0)
    ext = _with_halo(pp_ref, p_ref, pn_ref, i)
    p = p_ref[...]
    p = p + mu_ref[...] * (0.5 * (_shifted(ext, -1) + _shifted(ext, 1)) - p)
    r, k, v = p[:, 0:BR], p[:, BR:2 * BR], p[:, 2 * BR:3 * BR]
    wd, ad, gd = p[:, 1536:1664], p[:, 1664:1792], p[:, 1792:1920]
    kk = k * kk_w_ref[...]
    kk = kk / jnp.maximum(jnp.sqrt(_seg_sum(kk * kk, HD)), 1e-12)
    r_ref[...] = r.astype(BF16)
    v_ref[...] = v.astype(BF16)
    kk_ref[...] = kk.astype(BF16)
    tw = jnp.tanh(wd)
    g_ref[...] = _dot_split(_sigmoid(gd), g2_ref[...])
    wl = [_dot_split(tw, w2_ref[d]) for d in range(2)]
    al = [_dot_split(ad, a2_ref[d]) for d in range(2)]
    kd_sum = jnp.zeros_like(k)
    for d in range(2):
        lw_ref[d] = -_sigmoid(w0_ref[d:d + 1, :] + wl[d]) * math.exp(-0.5)
        a = _sigmoid(a0_ref[d:d + 1, :] + al[d])
        kd = k * (1.0 + (a - 1.0) * ka_ref[...])
        kd_ref[d] = kd.astype(BF16)
        b_ref[d] = (kk * a).astype(BF16)
        kd_sum = kd_sum + kd
    bonus_ref[...] = _seg_sum(r * kd_sum * rk_ref[...], HD) * v


def _halo_specs(cols):
    per = TB // HALO
    return [pl.BlockSpec((HALO, cols), lambda i: (jnp.maximum(i * per - 1, 0), 0)),
            pl.BlockSpec((TB, cols), lambda i: (i, 0)),
            pl.BlockSpec((HALO, cols), lambda i: (jnp.minimum((i + 1) * per, T_ALL // HALO - 1), 0))]


def _rwkv_prep(p_rwkv, lp):
    tok = jax.ShapeDtypeStruct((T_ALL, BR), F32)
    tok2 = jax.ShapeDtypeStruct((2, T_ALL, BR), F32)
    tok16 = jax.ShapeDtypeStruct((T_ALL, BR), BF16)
    tok16x2 = jax.ShapeDtypeStruct((2, T_ALL, BR), BF16)
    spec1 = pl.BlockSpec((TB, BR), lambda i: (i, 0))
    spec2 = pl.BlockSpec((2, TB, BR), lambda i: (0, i, 0))
    return pl.pallas_call(
        _rwkv_prep_kernel,
        grid=(NBLK,),
        in_specs=_halo_specs(RWKV_COLS) + [
            _full((1, RWKV_COLS)), _full((2, BR)), _full((2, LANE, BR)), _full((2, BR)), _full((2, LANE, BR)),
            _full((LANE, BR)), _full((1, BR)), _full((1, BR)), _full((1, BR))],
        out_specs=[spec1, spec1, spec1, spec1, spec1, spec2, spec2, spec2],
        out_shape=[tok16, tok16, tok16, tok, tok, tok2, tok16x2, tok16x2],
        compiler_params=_cparams(("parallel",)),
        name="rwkv_prep",
    )(p_rwkv, p_rwkv, p_rwkv, lp['rwkv_mu'], lp['rwkv_w0'], lp['rwkv_w2p'], lp['rwkv_a0'], lp['rwkv_a2p'],
      lp['rwkv_g2'], lp['rwkv_k_k'], lp['rwkv_k_a'], lp['rwkv_r_k'])


def _pair_bd(x):
    h0 = lax.broadcasted_iota(jnp.int32, x.shape, 1) < HD
    return jnp.concatenate([jnp.where(h0, x, 0.0), jnp.where(h0, 0.0, x)], axis=0)


def _rwkv_scan_kernel(r_ref, v_ref, kk_ref, lw_ref, kd_ref, b_ref, s0_ref, y_ref, sf_ref,
                      s_ref, lrhs_ref, ly_ref, tinv_ref, v16_ref, ebt_ref, wc_ref):
    d = pl.program_id(0)
    blk = _scan_blk(d, pl.program_id(1))
    first, last = _scan_ends(d, blk)
    sgn = 1 - 2 * d
    C = RWKV_CHUNK
    nchunk = TB // C
    npair = BR // LANE
    ii = lax.broadcasted_iota(jnp.int32, (LANE, LANE), 0)
    jj = lax.broadcasted_iota(jnp.int32, (LANE, LANE), 1)
    same = (ii >> 6) == (jj >> 6)

    @pl.when(jnp.logical_and(first, blk < N_CTX))
    def _():
        s_ref[...] = jnp.zeros_like(s_ref)

    @pl.when(jnp.logical_and(first, blk >= N_CTX))
    def _():
        for p in range(npair):
            a = jnp.concatenate([s0_ref[0, 0, 2 * p], s0_ref[0, 0, 2 * p + 1]], axis=1)
            s_ref[p] = jnp.where(same, jnp.concatenate([a, a], axis=0).T, 0.0)

    ci = lax.broadcasted_iota(jnp.int32, (C, C), 0)
    cj = lax.broadcasted_iota(jnp.int32, (C, C), 1)
    tri = ((cj - ci) * sgn <= 0).astype(BF16)
    rel = ((jj & (C - 1)) - (ii & (C - 1))) * sgn
    before = jnp.logical_and(same, rel < 0)
    before_incl = jnp.logical_and(same, rel <= 0)

    def rows_of(c):
        cc = c + d * (nchunk - 1 - 2 * c)
        return pl.ds(pl.multiple_of(cc * C, C), C)

    eye = (ii == jj).astype(F32)
    blk16 = (ii >> 4) == (jj >> 4)
    pairs = range(npair)

    def decay(c):
        lw = lw_ref[0, rows_of(c), :]
        return lw, _cumsum_rows(tri, lw)

    def prep(c, lw, cum):
        rows = rows_of(c)
        tot = jnp.sum(lw, axis=0, keepdims=True)
        e_neg = jnp.exp(-cum)
        e_end = jnp.exp(tot - cum)
        r = r_ref[rows, :].astype(F32)
        v = v_ref[rows, :].astype(F32)
        kk = kk_ref[rows, :].astype(F32)
        kd = kd_ref[0, rows, :].astype(F32)
        b = b_ref[0, rows, :].astype(F32)
        k_t, b_t = kd * e_neg, b * e_neg
        kk_h, r_h = kk * jnp.exp(cum - lw), r * jnp.exp(cum)
        k_e, b_e = kd * e_end, b * e_end
        n_of = {}
        for p in pairs:
            sl = slice(p * LANE, (p + 1) * LANE)
            KK, R, KT, BT, V, KE, BE = (_pair_bd(x[:, sl]) for x in (kk_h, r_h, k_t, b_t, v, k_e, b_e))
            kk16, r16 = KK.astype(BF16), R.astype(BF16)
            g = _dot_nt(jnp.concatenate([kk16, r16], axis=0),
                        jnp.concatenate([BT, KT], axis=0).astype(BF16))
            n_of[c, p] = jnp.where(before, g[:LANE, :LANE], 0.0)
            a1 = jnp.where(before, g[:LANE, LANE:], 0.0)
            a4 = jnp.where(before_incl, g[LANE:, :LANE], 0.0)
            a3 = jnp.where(before_incl, g[LANE:, LANE:], 0.0)
            lrhs_ref[c, p] = jnp.concatenate([kk16, a1.astype(BF16)], axis=1)
            ly_ref[c, p] = jnp.concatenate([r16, a3.astype(BF16), a4.astype(BF16)], axis=1)
            v16_ref[c, p] = V.astype(BF16)
            ebt_ref[c, p] = jnp.concatenate([KE.T, BE.T], axis=1).astype(BF16)
            wc_ref[c, p] = jnp.exp(jnp.broadcast_to(tot[:, sl], (LANE, LANE)).T)
        return n_of

    def inverse_stages(n_of):
        chains = list(n_of)
        t = {}

        def level(name, f):
            def run():
                t[name] = {k: f(k) for k in chains}
            return run

        def store():
            for k in chains:
                tinv_ref[k] = t['tinv'][k].astype(BF16)

        return [
            level('dg', lambda k: jnp.where(blk16, n_of[k], 0.0)),
            level('low', lambda k: n_of[k] - t['dg'][k]),
            level('d2', lambda k: _dot16(t['dg'][k], t['dg'][k])),
            level('d4', lambda k: _dot16(t['d2'][k], t['d2'][k])),
            level('x1', lambda k: _dot16(eye - t['dg'][k], eye + t['d2'][k])),
            level('d8', lambda k: _dot16(t['d4'][k], t['d4'][k])),
            level('x2', lambda k: _dot16(t['x1'][k], eye + t['d4'][k])),
            level('xd', lambda k: _dot16(t['x2'][k], eye + t['d8'][k])),
            level('m', lambda k: _dot16(t['xd'][k], t['low'][k])),
            level('m2', lambda k: _dot16(t['m'][k], t['m'][k])),
            level('t1', lambda k: _dot16(eye - t['m'][k], eye + t['m2'][k])),
            level('tinv', lambda k: _dot16(t['t1'][k], t['xd'][k])),
            store,
        ]

    def state_stages(c):
        rows = rows_of(c)
        t = {}

        def load():
            t['s'] = [s_ref[p] for p in pairs]
            t['s16'] = [x.astype(BF16) for x in t['s']]
            t['v16'] = [v16_ref[c, p] for p in pairs]
            t['rhs'] = [_dot(lrhs_ref[c, p], jnp.concatenate([t['s16'][p], t['v16'][p]], axis=0)).astype(BF16)
                        for p in pairs]

        def solve():
            t['u16'] = [(-_dot(tinv_ref[c, p], t['rhs'][p])).astype(BF16) for p in pairs]

        def update():
            for p in pairs:
                s_ref[p] = t['s'][p] * wc_ref[c, p] + _dot(
                    ebt_ref[c, p], jnp.concatenate([t['v16'][p], t['u16'][p]], axis=0))

        def emit():
            for p in pairs:
                y = _dot(ly_ref[c, p], jnp.concatenate([t['s16'][p], t['v16'][p], t['u16'][p]], axis=0))
                y_ref[0, rows, p * LANE:(p + 1) * LANE] = y[:C] + y[C:]

        return [load, solve, update, emit]

    n_all = {}
    for c in range(nchunk):
        n_all.update(prep(c, *decay(c)))
    for stage in inverse_stages(n_all):
        stage()
    for c in range(nchunk):
        for stage in state_stages(c):
            stage()

    @pl.when(last)
    def _():
        for p in range(npair):
            sp = s_ref[p]
            folded = sp[:HD] + sp[HD:]
            z = jnp.concatenate([folded, folded], axis=0).T
            sf_ref[0, 0, 2 * p] = z[:HD, :HD]
            sf_ref[0, 0, 2 * p + 1] = z[HD:, :HD]


def _rwkv_scan(r, v, kk, lw, kd, b, s0):
    np_ = BR // LANE
    nchunk = TB // RWKV_CHUNK
    spec1 = pl.BlockSpec((TB, BR), lambda d, i: (_scan_blk(d, i), 0))
    spec2 = pl.BlockSpec((1, TB, BR), lambda d, i: (d, _scan_blk(d, i), 0))
    seq = lambda d, i: _blk_seq(_scan_blk(d, i))
    s0spec = pl.BlockSpec((1, 1, HEADS, HD, HD), lambda d, i: (_lat_of_seq(seq(d, i)), d, 0, 0, 0))
    sfspec = pl.BlockSpec((1, 1, HEADS, HD, HD), lambda d, i: (seq(d, i), d, 0, 0, 0))
    per = lambda rows, cols, dt: pltpu.VMEM((nchunk, np_, rows, cols), dt)
    return pl.pallas_call(
        _rwkv_scan_kernel,
        grid=(2, NBLK),
        in_specs=[spec1, spec1, spec1, spec2, spec2, spec2, s0spec],
        out_specs=[spec2, sfspec],
        out_shape=[jax.ShapeDtypeStruct((2, T_ALL, BR), F32),
                   jax.ShapeDtypeStruct((N_SEQ, 2, HEADS, HD, HD), F32)],
        scratch_shapes=[pltpu.VMEM((np_, LANE, LANE), F32),
                        per(LANE, 2 * LANE, BF16), per(LANE, 3 * LANE, BF16), per(LANE, LANE, BF16),
                        per(LANE, LANE, BF16), per(LANE, 2 * LANE, BF16), per(LANE, LANE, F32)],
        compiler_params=_cparams(("arbitrary", "arbitrary")),
        name="rwkv_scan",
    )(r, v, kk, lw, kd, b, s0)


def _ssd_prep_kernel(pp_ref, p_ref, pn_ref, cw_ref, cb_ref, dtb_ref, x_ref, bc_ref, dt_ref):
    i = pl.program_id(0)
    ext = _with_halo(pp_ref, p_ref, pn_ref, i)[:, BR:BR + SSD_XBC]
    acc = cb_ref[...] + cw_ref[0:1, :] * _shifted(ext, -(SSD_CONV // 2))
    for j in range(1, SSD_CONV):
        acc = acc + cw_ref[j:j + 1, :] * _shifted(ext, j - SSD_CONV // 2)
    xbc = _silu(acc)
    x_ref[...] = xbc[:, :BR]
    bc_ref[...] = xbc[:, BR:]
    dt_ref[...] = _softplus(p_ref[:, BR + SSD_XBC:] + dtb_ref[...])


def _ssd_prep(p_ssd, lp):
    return pl.pallas_call(
        _ssd_prep_kernel,
        grid=(NBLK,),
        in_specs=_halo_specs(SSD_PCOLS) + [_full((8, SSD_XBC)), _full((1, SSD_XBC)), _full((1, 2 * LANE))],
        out_specs=[pl.BlockSpec((TB, BR), lambda i: (i, 0)),
                   pl.BlockSpec((TB, 2 * LANE), lambda i: (i, 0)),
                   pl.BlockSpec((TB, 2 * LANE), lambda i: (i, 0))],
        out_shape=[jax.ShapeDtypeStruct((T_ALL, BR), F32),
                   jax.ShapeDtypeStruct((T_ALL, 2 * LANE), F32),
                   jax.ShapeDtypeStruct((T_ALL, 2 * LANE), F32)],
        compiler_params=_cparams(("parallel",)),
        name="ssd_prep",
    )(p_ssd, p_ssd, p_ssd, lp['ssd_conv_w8'], lp['ssd_conv_b'], lp['ssd_dt_bias_p'])


def _ssd_scan_kernel(x_ref, bc_ref, dt_ref, alog_ref, h0_ref, y_ref, hf_ref, hs_ref):
    d = pl.program_id(0)
    blk = _scan_blk(d, pl.program_id(1))
    first, last = _scan_ends(d, blk)
    sgn = 1 - 2 * d
    npair = BR // LANE
    half64 = lax.broadcasted_iota(jnp.int32, (HD, LANE), 1) < HD

    @pl.when(jnp.logical_and(first, blk < N_CTX))
    def _():
        hs_ref[...] = jnp.zeros_like(hs_ref)

    @pl.when(jnp.logical_and(first, blk >= N_CTX))
    def _():
        zero = jnp.zeros((HD, LANE), F32)
        blocks = []
        for q in range(npair):
            a = jnp.concatenate([h0_ref[0, 0, 2 * q], h0_ref[0, 0, 2 * q + 1]], axis=1)
            t = jnp.concatenate([a, a], axis=0).T
            blocks.append(jnp.where(half64, t[:HD], t[HD:]))
        rows = [jnp.concatenate([blocks[q] if q // 2 == g else zero for q in range(npair)], axis=1)
                for g in range(2)]
        hs_ref[...] = jnp.concatenate(rows, axis=0)

    x = x_ref[...]
    bm = bc_ref[:, :LANE]
    cm = bc_ref[:, LANE:]
    dt = dt_ref[...]
    a_neg = -jnp.exp(alog_ref[0])
    dta = dt * a_neg
    qi = lax.broadcasted_iota(jnp.int32, (TB, TB), 0)
    qj = lax.broadcasted_iota(jnp.int32, (TB, TB), 1)
    before_incl = (qj - qi) * sgn <= 0
    a_cum = _cumsum_rows(before_incl.astype(BF16), dta)
    tot = jnp.sum(dta, axis=0, keepdims=True)
    a_cum_t = a_cum.T
    dt_t = dt.T
    eh = lax.broadcasted_iota(jnp.int32, (LANE, BR), 0)
    ec = lax.broadcasted_iota(jnp.int32, (LANE, BR), 1)
    expand = ((ec >> 6) == eh).astype(BF16)
    e_in = _dot_exact_rhs(jnp.exp(a_cum), expand)
    to_end = _dot_exact_rhs(jnp.exp(tot - a_cum) * dt, expand)
    dec = _dot_exact_rhs(jnp.broadcast_to(jnp.exp(tot), (8, LANE)), expand)[0:1]
    hs = hs_ref[...]
    cb16, bb16, xb16 = cm.astype(BF16), bm.astype(BF16), x.astype(BF16)
    glane = lax.broadcasted_iota(jnp.int32, (TB, LANE), 1) >> 6
    half = lax.broadcasted_iota(jnp.int32, (TB, LANE), 1) < HD
    cbg = [_dot_nt(jnp.where(glane == g, cm, 0.0).astype(BF16), bb16) for g in range(2)]
    y_off = _dot(cb16, hs.astype(BF16)) * e_in
    upd = _dot_tn(bb16, (x * to_end).astype(BF16))
    scores = []
    for h in range(HEADS):
        seg = a_cum[:, h:h + 1] - a_cum_t[h:h + 1, :]
        decay = jnp.exp(jnp.where(before_incl, seg, -jnp.inf))
        scores.append((cbg[h // 4] * decay * dt_t[h:h + 1, :]).astype(BF16))
    y_heads = [_dot(scores[h], xb16[:, (h // 2) * LANE:(h // 2 + 1) * LANE]) for h in range(HEADS)]
    y_parts = [jnp.where(half, y_heads[2 * q], y_heads[2 * q + 1]) for q in range(npair)]
    y_ref[0] = jnp.concatenate(y_parts, axis=1) + y_off
    ui = lax.broadcasted_iota(jnp.int32, (LANE, BR), 0)
    uj = lax.broadcasted_iota(jnp.int32, (LANE, BR), 1)
    hs_ref[...] = hs * dec + jnp.where((ui >> 6) == (uj >> 8), upd, 0.0)

    @pl.when(last)
    def _():
        hn = hs_ref[...]
        for q in range(npair):
            g = q // 2
            w = hn[g * HD:(g + 1) * HD, q * LANE:(q + 1) * LANE]
            z = jnp.concatenate([w, w], axis=0).T
            hf_ref[0, 0, 2 * q] = z[:HD, :HD]
            hf_ref[0, 0, 2 * q + 1] = z[HD:, :HD]


def _ssd_scan(x, bc, dt, alog, h0):
    seq = lambda d, i: _blk_seq(_scan_blk(d, i))
    return pl.pallas_call(
        _ssd_scan_kernel,
        grid=(2, NBLK),
        in_specs=[pl.BlockSpec((TB, BR), lambda d, i: (_scan_blk(d, i), 0)),
                  pl.BlockSpec((TB, 2 * LANE), lambda d, i: (_scan_blk(d, i), 0)),
                  pl.BlockSpec((TB, LANE), lambda d, i: (_scan_blk(d, i), d)),
                  pl.BlockSpec((1, 1, LANE), lambda d, i: (d, 0, 0)),
                  pl.BlockSpec((1, 1, HEADS, HD, HD), lambda d, i: (_lat_of_seq(seq(d, i)), d, 0, 0, 0))],
        out_specs=[pl.BlockSpec((1, TB, BR), lambda d, i: (d, _scan_blk(d, i), 0)),
                   pl.BlockSpec((1, 1, HEADS, HD, HD), lambda d, i: (seq(d, i), d, 0, 0, 0))],
        out_shape=[jax.ShapeDtypeStruct((2, T_ALL, BR), F32),
                   jax.ShapeDtypeStruct((N_SEQ, 2, HEADS, HD, HD), F32)],
        scratch_shapes=[pltpu.VMEM((LANE, BR), F32)],
        compiler_params=_cparams(("arbitrary", "arbitrary")),
        name="ssd_scan",
    )(x, bc, dt, alog, h0)


def _gqa(q, kd, vd, sink_ref, bias):
    nq = q.shape[0]
    log2e = math.log2(math.e)
    half = lax.broadcasted_iota(jnp.int32, (nq, LANE), 1) < HD
    q = q * (ATT_SCALE * log2e)
    scores = []
    for g in range(ATT_KVH):
        rows = []
        for j in (2 * g, 2 * g + 1):
            q2 = q[:, j * LANE:(j + 1) * LANE]
            rows += [jnp.where(half, q2, 0.0), jnp.where(half, 0.0, q2)]
        qg = jnp.concatenate(rows, axis=0).astype(BF16)
        scores.append(_dot_nt(qg, kd[g].astype(BF16)))
    outs = []
    for g in range(ATT_KVH):
        s = scores[g]
        if bias is not None:
            kb = bias.shape[1]
            s = jnp.concatenate([s[:, :kb] + jnp.concatenate([bias] * ATT_GROUP, axis=0), s[:, kb:]], axis=1)
        sink = jnp.concatenate(
            [jnp.full((nq, 1), sink_ref[ATT_GROUP * g + h] * log2e, F32) for h in range(ATT_GROUP)], axis=0)
        mx = jnp.maximum(jnp.max(s, axis=-1, keepdims=True), sink)
        e = jnp.exp2(s - mx)
        inv = 1.0 / (jnp.sum(e, axis=-1, keepdims=True) + jnp.exp2(sink - mx))
        o = _dot(e.astype(BF16), vd[g].astype(BF16)) * inv
        outs += [jnp.where(half, o[0:nq], o[nq:2 * nq]), jnp.where(half, o[2 * nq:3 * nq], o[3 * nq:])]
    return jnp.concatenate(outs, axis=1)


def _ctx_attn_kernel(sink_ref, p_ref, o_ref):
    kd = [p_ref[:, BR + g * LANE:BR + (g + 1) * LANE] for g in range(ATT_KVH)]
    vd = [p_ref[:, BR + (2 + g) * LANE:BR + (3 + g) * LANE] for g in range(ATT_KVH)]
    o_ref[...] = _gqa(p_ref[:, :BR], kd, vd, sink_ref, None)


def _ctx_attn(p_att, sink):
    return pl.pallas_call(
        _ctx_attn_kernel,
        grid=(N_CTX,),
        in_specs=[pl.BlockSpec(memory_space=pltpu.SMEM), pl.BlockSpec((TB, ATT_PCOLS), lambda i: (i, 0))],
        out_specs=pl.BlockSpec((TB, BR), lambda i: (i, 0)),
        out_shape=jax.ShapeDtypeStruct((T_CTX, BR), F32),
        compiler_params=_cparams(("parallel",)),
        name="ctx_attn",
    )(sink, p_att)


def _rope(x, cos, sin_signed):
    lanes = x.shape[1]
    reps = lanes // LANE
    if reps > 1:
        cos = jnp.concatenate([cos] * reps, axis=1)
        sin_signed = jnp.concatenate([sin_signed] * reps, axis=1)
    lo = (lax.broadcasted_iota(jnp.int32, x.shape, 1) & 31) < 16
    partner = jnp.where(lo, pltpu.roll(x, lanes - 16, axis=1), pltpu.roll(x, 16, axis=1))
    return x * cos + partner * sin_signed


def _lat_attn_kernel(sink_ref, pq_ref, pp_ref, pn_ref, cq_ref, sq_ref, cp_ref, sp_ref, cn_ref, sn_ref,
                     kc_ref, vc_ref, o_ref):
    j = pl.program_id(1)
    hw = ATT_WINDOW
    q = _rope(pq_ref[:, :BR], cq_ref[...], sq_ref[...])
    kd, vd = [], []
    for g in range(ATT_KVH):
        kc, vc = BR + g * LANE, BR + (2 + g) * LANE
        kh, vh = g * LANE, (2 + g) * LANE
        kd.append(jnp.concatenate([_rope(pp_ref[:, kh:kh + LANE], cp_ref[...], sp_ref[...]),
                                   _rope(pq_ref[:, kc:kc + LANE], cq_ref[...], sq_ref[...]),
                                   _rope(pn_ref[:, kh:kh + LANE], cn_ref[...], sn_ref[...]),
                                   kc_ref[0, g]], axis=0))
        vd.append(jnp.concatenate([pp_ref[:, vh:vh + LANE], pq_ref[:, vc:vc + LANE], pn_ref[:, vh:vh + LANE],
                                   vc_ref[0, g]], axis=0))
    nloc = TB + 2 * hw
    qi = lax.broadcasted_iota(jnp.int32, (TB, nloc), 0)
    kj = lax.broadcasted_iota(jnp.int32, (TB, nloc), 1)
    rel = kj - hw - qi
    kpos = j * TB - hw + kj
    valid = (rel <= ATT_WINDOW) & (rel >= -ATT_WINDOW) & (kpos >= 0) & (kpos < LAT_LEN)
    o_ref[...] = _gqa(q, kd, vd, sink_ref, jnp.where(valid, 0.0, -jnp.inf))


def _lat_attn(p_att, sink, rope_cos, rope_sin, k_ctx, v_ctx):
    nb = LAT_LEN // TB
    hw = ATT_WINDOW
    per = TB // hw
    qrow = lambda b, j: N_CTX + b * nb + j
    hrow = lambda b, j: per * (N_CTX + b * nb)
    prv = lambda j: jnp.maximum(per * j - 1, 0)
    nxt = lambda j: jnp.minimum(per * (j + 1), per * nb - 1)
    hspec = lambda f: pl.BlockSpec((hw, 4 * LANE), lambda b, j: (hrow(b, j) + f(j), 1))
    tspec = lambda rows, f: pl.BlockSpec((rows, LANE), lambda b, j: (f(j), 0))
    cspec = pl.BlockSpec((1, ATT_KVH, PAST_LEN, LANE), lambda b, j: (b, 0, 0, 0))
    return pl.pallas_call(
        _lat_attn_kernel,
        grid=(N_LAT, nb),
        in_specs=[pl.BlockSpec(memory_space=pltpu.SMEM),
                  pl.BlockSpec((TB, ATT_PCOLS), lambda b, j: (qrow(b, j), 0)), hspec(prv), hspec(nxt),
                  tspec(TB, lambda j: j), tspec(TB, lambda j: j), tspec(hw, prv), tspec(hw, prv),
                  tspec(hw, nxt), tspec(hw, nxt), cspec, cspec],
        out_specs=pl.BlockSpec((TB, BR), lambda b, j: (b * nb + j, 0)),
        out_shape=jax.ShapeDtypeStruct((N_LAT * LAT_LEN, BR), F32),
        compiler_params=_cparams(("parallel", "parallel")),
        name="lat_attn",
    )(sink, p_att, p_att, p_att, rope_cos, rope_sin, rope_cos, rope_sin, rope_cos, rope_sin, k_ctx, v_ctx)


def _merge_kernel(h_ref, mod_ref, nw_ref, yr_ref, bonus_ref, g_ref, lnw_ref, lnb_ref, ys_ref, xs_ref, z_ref,
                  dvec_ref, snw_ref, yac_ref, yal_ref, wg_ref, wb_ref, wo_ref, o_ref):
    m = mod_ref[0]
    h = h_ref[...]
    pg = _dot(_normmod(h, nw_ref[...], m[:, :D_MODEL], m[:, D_MODEL:2 * D_MODEL]).astype(BF16), wg_ref[0])
    y = yr_ref[0] + yr_ref[1] + bonus_ref[...]
    mu = _seg_sum(y, HD) * (1.0 / HD)
    yc = y - mu
    var = _seg_sum(yc * yc, HD) * (1.0 / HD)
    y_a = (yc * lax.rsqrt(var + RWKV_GN_EPS) * lnw_ref[...] + lnb_ref[...]) * g_ref[...]
    y = (ys_ref[0] + ys_ref[1] + dvec_ref[...] * xs_ref[...]) * _silu(z_ref[...])
    y_b = y * lax.rsqrt(jnp.mean(y * y, axis=-1, keepdims=True) + NORM_EPS) * snw_ref[...]
    y_c = jnp.where(pl.program_id(0) < N_CTX, yac_ref[...], yal_ref[...])
    wide = [_dot(br.astype(BF16), wb_ref[0, n]) for n, br in enumerate((y_a, y_b, y_c))]
    merged = None
    for n in range(3):
        term = _sigmoid(pg[:, n * D_MODEL:(n + 1) * D_MODEL]) * wide[n]
        merged = term if merged is None else merged + term
    g1 = m[:, 2 * D_MODEL:3 * D_MODEL]
    o_ref[...] = h + g1 * _dot(merged.astype(BF16), wo_ref[0])


def _merge(h, mod, yr, bonus, g, ys, xs, p_ssd, ya_ctx, ya_lat, lp, w_gate, w_branch, w_out, l):
    tm = TB
    row = lambda w: pl.BlockSpec((tm, w), lambda i: (i, 0))
    row2 = pl.BlockSpec((2, tm, BR), lambda i: (0, i, 0))
    return pl.pallas_call(
        _merge_kernel,
        grid=(T_ALL // tm,),
        in_specs=[row(D_MODEL), pl.BlockSpec((1, 1, 6 * D_MODEL), lambda i: (_blk_type(i), 0, 0)),
                  _full((1, D_MODEL)),
                  row2, row(BR), row(BR), _full((1, BR)), _full((1, BR)),
                  row2, row(BR), row(BR), _full((1, BR)), _full((1, BR)),
                  pl.BlockSpec((tm, BR), lambda i: (jnp.minimum(i, N_CTX - 1), 0)),
                  pl.BlockSpec((tm, BR), lambda i: (jnp.maximum(i - N_CTX, 0), 0)),
                  pl.BlockSpec((1, D_MODEL, GATE_COLS), lambda i: (l, 0, 0)),
                  pl.BlockSpec((1, 3, BR, D_MODEL), lambda i: (l, 0, 0, 0)),
                  pl.BlockSpec((1, D_MODEL, D_MODEL), lambda i: (l, 0, 0))],
        out_specs=row(D_MODEL),
        out_shape=jax.ShapeDtypeStruct((T_ALL, D_MODEL), F32),
        compiler_params=_cparams(("parallel",)),
        name="merge",
    )(h, mod, lp['norm1_w'], yr, bonus, g, lp['rwkv_lnx_w'], lp['rwkv_lnx_b'], ys, xs, p_ssd, lp['ssd_d_vec'],
      lp['ssd_norm_w'], ya_ctx, ya_lat, w_gate, w_branch, w_out)


def _route_kernel(h_ref, mod_ref, nw_ref, wr_ref, xn_ref, gate_ref, slot_ref, slot_t_ref, cnt_ref):
    m = mod_ref[0]
    xn = _normmod(h_ref[...], nw_ref[...], m[:, 3 * D_MODEL:4 * D_MODEL], m[:, 4 * D_MODEL:5 * D_MODEL])
    xn_ref[...] = xn.astype(BF16)
    logits = _dot_split(xn, wr_ref[...])
    lane_i = lax.broadcasted_iota(jnp.int32, logits.shape, 1)
    lane = lane_i.astype(F32)
    lane_grp = (lane_i >> 3).astype(F32)
    neg = -jnp.inf
    big = float(LANE)
    is_g = (lane_i >= MOE_EXPERTS) & (lane_i < MOE_EXPERTS + MOE_GROUPS)
    gl = jnp.where(is_g, logits, neg)
    gmax = jnp.max(gl, axis=-1, keepdims=True)
    gsel = jnp.min(jnp.where(gl == gmax, lane - MOE_EXPERTS, big), axis=-1, keepdims=True)
    g_w = 1.0 / jnp.sum(jnp.where(is_g, jnp.exp(gl - gmax), 0.0), axis=-1, keepdims=True)
    el = jnp.where((lane_i < MOE_EXPERTS) & (lane_grp == gsel), logits, neg)
    m1 = jnp.max(el, axis=-1, keepdims=True)
    i1 = jnp.min(jnp.where(el == m1, lane, big), axis=-1, keepdims=True)
    el2 = jnp.where(lane == i1, neg, el)
    m2 = jnp.max(el2, axis=-1, keepdims=True)
    i2 = jnp.min(jnp.where(el2 == m2, lane, big), axis=-1, keepdims=True)
    e2 = jnp.exp(m2 - m1)
    w1 = 1.0 / (1.0 + e2)
    gate_ref[...] = jnp.where(lane == i1, w1 * g_w, jnp.where(lane == i2, e2 * w1 * g_w, 0.0))
    member = lane == gsel
    ti = lax.broadcasted_iota(jnp.int32, (TB, TB), 0)
    tj = lax.broadcasted_iota(jnp.int32, (TB, TB), 1)
    rank = _dot((tj <= ti).astype(BF16), jnp.where(member, 1.0, 0.0).astype(BF16))
    slot = jnp.where(member, rank - 1.0, -1.0)
    slot_ref[...] = slot
    slot_t_ref[...] = slot.T[:8]
    cnt_ref[0] = jnp.max(slot, axis=0, keepdims=True) + 1.0


def _route(h, mod, nw, wr):
    tm = TB
    return pl.pallas_call(
        _route_kernel,
        grid=(T_ALL // tm,),
        in_specs=[pl.BlockSpec((tm, D_MODEL), lambda i: (i, 0)),
                  pl.BlockSpec((1, 1, 6 * D_MODEL), lambda i: (_blk_type(i), 0, 0)),
                  _full((1, D_MODEL)), _full((D_MODEL, LANE))],
        out_specs=[pl.BlockSpec((tm, D_MODEL), lambda i: (i, 0)), pl.BlockSpec((tm, LANE), lambda i: (i, 0)),
                   pl.BlockSpec((tm, LANE), lambda i: (i, 0)), pl.BlockSpec((8, tm), lambda i: (0, i)),
                   pl.BlockSpec((1, 1, LANE), lambda i: (i, 0, 0))],
        out_shape=[jax.ShapeDtypeStruct((T_ALL, D_MODEL), BF16), jax.ShapeDtypeStruct((T_ALL, LANE), F32),
                   jax.ShapeDtypeStruct((T_ALL, LANE), F32), jax.ShapeDtypeStruct((8, T_ALL), F32),
                   jax.ShapeDtypeStruct((NBLK, 1, LANE), F32)],
        compiler_params=_cparams(("parallel",)),
        name="moe_route",
    )(h, mod, nw, wr)


MOE_EB = 4
MOE_TM = 4 * TB
MOE_CAP = 96
MOE_ROWS = (MOE_TM // TB) * MOE_CAP


def _expert_block(x, gate, lane0, w1_ref, w3_ref, w2_ref):
    lane = lax.broadcasted_iota(jnp.int32, gate.shape, 1)
    h1 = [_dot(x, w1_ref[0, k]) for k in range(MOE_EB)]
    h3 = [_dot(x, w3_ref[0, k]) for k in range(MOE_EB)]
    acts = []
    for k in range(MOE_EB):
        gcol = jnp.sum(jnp.where(lane == lane0 + k, gate, 0.0), axis=-1, keepdims=True)
        acts.append((_silu(h1[k]) * h3[k] * gcol).astype(BF16))
    return _dot(jnp.concatenate(acts, axis=1), w2_ref[0].reshape(MOE_EB * D_EXPERT, D_MODEL))


def _experts_kernel(cnt_ref, xn_ref, gate_ref, slot_ref, slot_t_ref, w1_ref, w3_ref, w2_ref, h_ref, mod_ref,
                    o_ref, xg_ref, gg_ref, yg_ref):
    i = pl.program_id(0)
    hg = pl.program_id(1)
    g = hg // 2
    nsub = MOE_TM // TB
    fits = cnt_ref[i * nsub, g] <= MOE_CAP
    for sb in range(1, nsub):
        fits = jnp.logical_and(fits, cnt_ref[i * nsub + sb, g] <= MOE_CAP)
    lane0 = hg * MOE_EB

    @pl.when(hg == 0)
    def _():
        o_ref[...] = jnp.zeros_like(o_ref)

    @pl.when(jnp.logical_and(fits, hg % 2 == 0))
    def _():
        sub8 = lax.broadcasted_iota(jnp.int32, (8, MOE_TM), 0)
        srow = jnp.max(jnp.where(sub8 == g, slot_t_ref[...], -2.0), axis=0, keepdims=True)
        rid = lax.broadcasted_iota(jnp.int32, (MOE_CAP, TB), 0).astype(F32)
        for sb in range(nsub):
            tok = slice(sb * TB, (sb + 1) * TB)
            sel = jnp.where(rid == srow[:, tok], 1.0, 0.0).astype(BF16)
            xg_ref[sb * MOE_CAP:(sb + 1) * MOE_CAP] = _dot(sel, xn_ref[tok, :]).astype(BF16)
            gh, gl = _split16(gate_ref[tok, :])
            gg = _dot(sel, jnp.concatenate([gh, gl], axis=1))
            gg_ref[sb * MOE_CAP:(sb + 1) * MOE_CAP] = gg[:, :LANE] + gg[:, LANE:]
        yg_ref[...] = _expert_block(xg_ref[...], gg_ref[...], lane0, w1_ref, w3_ref, w2_ref)

    @pl.when(jnp.logical_and(fits, hg % 2 == 1))
    def _():
        yg_ref[...] += _expert_block(xg_ref[...], gg_ref[...], lane0, w1_ref, w3_ref, w2_ref)
        lane = lax.broadcasted_iota(jnp.int32, (TB, LANE), 1)
        cid = lax.broadcasted_iota(jnp.int32, (TB, 2 * MOE_CAP), 1)
        cid = jnp.where(cid < MOE_CAP, cid, cid - MOE_CAP).astype(F32)
        for sb in range(nsub):
            tok = slice(sb * TB, (sb + 1) * TB)
            scol = jnp.max(jnp.where(lane == g, slot_ref[tok, :], -2.0), axis=-1, keepdims=True)
            sel2 = jnp.where(cid == scol, 1.0, 0.0).astype(BF16)
            yh, yl = _split16(yg_ref[sb * MOE_CAP:(sb + 1) * MOE_CAP])
            o_ref[tok, :] += _dot(sel2, jnp.concatenate([yh, yl], axis=0))

    @pl.when(jnp.logical_not(fits))
    def _():
        o_ref[...] += _expert_block(xn_ref[...], gate_ref[...], lane0, w1_ref, w3_ref, w2_ref)

    @pl.when(hg == 2 * MOE_GROUPS - 1)
    def _():
        g2 = mod_ref[0][:, 5 * D_MODEL:6 * D_MODEL]
        o_ref[...] = h_ref[...] + g2 * o_ref[...]


def _experts(cnt, xn, gate, slot, slot_t, w1, w3, w2, h, mod, l):
    tm = MOE_TM
    grid_spec = pltpu.PrefetchScalarGridSpec(
        num_scalar_prefetch=1,
        grid=(T_ALL // tm, MOE_EXPERTS // MOE_EB),
        in_specs=[pl.BlockSpec((tm, D_MODEL), lambda i, e, c: (i, 0)),
                  pl.BlockSpec((tm, LANE), lambda i, e, c: (i, 0)),
                  pl.BlockSpec((tm, LANE), lambda i, e, c: (i, 0)),
                  pl.BlockSpec((8, tm), lambda i, e, c: (0, i)),
                  pl.BlockSpec((1, MOE_EB, D_MODEL, D_EXPERT), lambda i, e, c: (l, e, 0, 0)),
                  pl.BlockSpec((1, MOE_EB, D_MODEL, D_EXPERT), lambda i, e, c: (l, e, 0, 0)),
                  pl.BlockSpec((1, MOE_EB, D_EXPERT, D_MODEL), lambda i, e, c: (l, e, 0, 0)),
                  pl.BlockSpec((tm, D_MODEL), lambda i, e, c: (i, 0)),
                  pl.BlockSpec((1, 1, 6 * D_MODEL), lambda i, e, c: (_blk_type((tm // TB) * i), 0, 0))],
        out_specs=pl.BlockSpec((tm, D_MODEL), lambda i, e, c: (i, 0)),
        scratch_shapes=[pltpu.VMEM((MOE_ROWS, D_MODEL), BF16), pltpu.VMEM((MOE_ROWS, LANE), F32),
                        pltpu.VMEM((MOE_ROWS, D_MODEL), F32)])
    return pl.pallas_call(
        _experts_kernel,
        grid_spec=grid_spec,
        out_shape=jax.ShapeDtypeStruct((T_ALL, D_MODEL), F32),
        compiler_params=_cparams(("parallel", "arbitrary")),
        name="moe_experts",
    )(cnt, xn, gate, slot, slot_t, w1, w3, w2, h, mod)


def _final_norm_kernel(h_ref, w_ref, o_ref):
    x = h_ref[...]
    o_ref[...] = x * lax.rsqrt(jnp.mean(x * x, axis=-1, keepdims=True) + NORM_EPS) * w_ref[...]


def _final_norm(h, w, row0, rows, name):
    tm = 2 * TB
    return pl.pallas_call(
        _final_norm_kernel,
        grid=(rows // tm,),
        in_specs=[pl.BlockSpec((tm, D_MODEL), lambda i: (row0 // tm + i, 0)), _full((1, D_MODEL))],
        out_specs=pl.BlockSpec((tm, D_MODEL), lambda i: (i, 0)),
        out_shape=jax.ShapeDtypeStruct((rows, D_MODEL), F32),
        compiler_params=_cparams(("parallel",)),
        name=name,
    )(h, w)


def _rope_tables():
    pos = jnp.arange(LAT_LEN)
    row = (pos // GRID_W).astype(F32)
    col = (pos % GRID_W).astype(F32)
    half = HD // 2
    inv = 1.0 / (ROPE_BASE ** (jnp.arange(0, half, 2, dtype=F32) / half))
    ar, ac = row[:, None] * inv[None, :], col[:, None] * inv[None, :]
    cos = jnp.concatenate([jnp.cos(ar), jnp.cos(ar), jnp.cos(ac), jnp.cos(ac)], axis=1)
    sin = jnp.concatenate([-jnp.sin(ar), jnp.sin(ar), -jnp.sin(ac), jnp.sin(ac)], axis=1)
    return jnp.tile(cos, (1, 2)), jnp.tile(sin, (1, 2))


def _dup_heads(x):
    x = jnp.swapaxes(x, -2, -3)
    return jnp.concatenate([x, x], axis=-1)


def _proj_weights(w_in):
    o_ssd, o_att = RWKV_COLS, RWKV_COLS + 1296
    o_gate = o_att + BR + 2 * ATT_KVH * HD
    pad = jnp.zeros((DEPTH, D_MODEL, LANE - HEADS), F32)
    dt0 = o_ssd + BR + SSD_XBC
    w_ssd = jnp.concatenate([w_in[:, :, o_ssd:dt0], w_in[:, :, dt0:dt0 + HEADS], pad,
                             w_in[:, :, dt0 + HEADS:dt0 + 2 * HEADS], pad], axis=2)
    kv = [w_in[:, :, o_att + BR + n * HD:o_att + BR + (n + 1) * HD] for n in range(2 * ATT_KVH)]
    w_att = jnp.concatenate([w_in[:, :, o_att:o_att + BR]] + [x for c in kv for x in (c, c)], axis=2)
    return (w_in[:, :, :o_ssd].astype(BF16), w_ssd.astype(BF16), w_att.astype(BF16),
            w_in[:, :, o_gate:].astype(BF16))


def _layer_params(l, prm):
    lp = {}
    row = lambda name: prm[name][l].reshape(1, -1)
    for name in ('norm1_w', 'norm2_w', 'rwkv_mu', 'rwkv_k_k', 'rwkv_k_a', 'rwkv_r_k', 'rwkv_lnx_w', 'rwkv_lnx_b',
                 'ssd_conv_b', 'ssd_norm_w'):
        lp[name] = row(name)
    lp['rwkv_w0'] = prm['rwkv_w0'][l]
    lp['rwkv_a0'] = prm['rwkv_a0'][l]
    z64 = jnp.zeros((HD, BR), F32)
    w2, a2 = prm['rwkv_w2'][l], prm['rwkv_a2'][l]
    lp['rwkv_w2p'] = jnp.stack([jnp.concatenate([w2[0], z64]), jnp.concatenate([z64, w2[1]])])
    lp['rwkv_a2p'] = jnp.stack([jnp.concatenate([a2[0], z64]), jnp.concatenate([z64, a2[1]])])
    lp['rwkv_g2'] = prm['rwkv_g2'][l]
    lp['ssd_conv_w8'] = jnp.concatenate([prm['ssd_conv_w'][l], jnp.zeros((8 - SSD_CONV, SSD_XBC), F32)])
    dtb = prm['ssd_dt_bias'][l]
    zp = jnp.zeros((LANE - HEADS,), F32)
    lp['ssd_dt_bias_p'] = jnp.concatenate([dtb[0], zp, dtb[1], zp]).reshape(1, 2 * LANE)
    lp['ssd_a_log_p'] = jnp.pad(prm['ssd_a_log'][l], ((0, 0), (0, LANE - HEADS))).reshape(2, 1, LANE)
    lp['ssd_d_vec'] = jnp.repeat(prm['ssd_d'][l], HD).reshape(1, BR)
    lp['att_sink'] = prm['att_sink'][l]
    wr = jnp.concatenate([prm['moe_w_expert'][l].reshape(D_MODEL, MOE_EXPERTS), prm['moe_w_group'][l]], axis=1)
    lp['w_route'] = jnp.pad(wr, ((0, 0), (0, LANE - MOE_EXPERTS - MOE_GROUPS)))
    return lp


def kernel(x_prompt, x_sample, cache_attn_k, cache_attn_v, state_rwkv, state_ssd, c, c_ctx, w_ada, b_ada, norm1_w, norm2_w, w_in, rwkv_mu, rwkv_w0, rwkv_w2, rwkv_a0, rwkv_a2, rwkv_g2, rwkv_k_k, rwkv_k_a, rwkv_r_k, rwkv_lnx_w, rwkv_lnx_b, ssd_conv_w, ssd_conv_b, ssd_dt_bias, ssd_a_log, ssd_d, ssd_norm_w, att_sink, w_branch, w_out, moe_w_group, moe_w_expert, moe_w1, moe_w3, moe_w2, final_norm_w):
    prm = dict(norm1_w=norm1_w, norm2_w=norm2_w, rwkv_mu=rwkv_mu, rwkv_w0=rwkv_w0, rwkv_w2=rwkv_w2,
               rwkv_a0=rwkv_a0, rwkv_a2=rwkv_a2, rwkv_g2=rwkv_g2, rwkv_k_k=rwkv_k_k, rwkv_k_a=rwkv_k_a,
               rwkv_r_k=rwkv_r_k.reshape(DEPTH, BR), rwkv_lnx_w=rwkv_lnx_w, rwkv_lnx_b=rwkv_lnx_b,
               ssd_conv_w=ssd_conv_w, ssd_conv_b=ssd_conv_b, ssd_dt_bias=ssd_dt_bias, ssd_a_log=ssd_a_log,
               ssd_d=ssd_d, ssd_norm_w=ssd_norm_w, att_sink=att_sink,
               moe_w_group=moe_w_group, moe_w_expert=moe_w_expert)
    w_rwkv, w_ssd, w_att, w_gate = _proj_weights(w_in)
    wb16, wo16 = w_branch.astype(BF16), w_out.astype(BF16)
    w1_16, w3_16, w2_16 = moe_w1.astype(BF16), moe_w3.astype(BF16), moe_w2.astype(BF16)
    kc_dup, vc_dup = _dup_heads(cache_attn_k), _dup_heads(cache_attn_v)
    cond8 = jnp.concatenate([c_ctx[None, :], c, jnp.zeros((8 - 1 - N_LAT, D_MODEL), F32)], axis=0)
    mod_all = _adaln(cond8, w_ada, b_ada)
    rope_cos, rope_sin = _rope_tables()
    h = jnp.concatenate([x_prompt.reshape(T_CTX, D_MODEL), x_sample.reshape(N_LAT * LAT_LEN, D_MODEL)], axis=0)
    ks, vs, srs, sss = [], [], [], []
    for l in range(DEPTH):
        lp = _layer_params(l, prm)
        mod = mod_all[l].reshape(8, 1, 6 * D_MODEL)
        p_rwkv = _normproj(h, mod, lp['norm1_w'], w_rwkv, l, RWKV_COLS, "proj_rwkv")
        p_ssd = _normproj(h, mod, lp['norm1_w'], w_ssd, l, SSD_PCOLS, "proj_ssd")
        p_att = _normproj(h, mod, lp['norm1_w'], w_att, l, ATT_PCOLS, "proj_att")
        r, v, kk, g, bonus, lw, kd, b = _rwkv_prep(p_rwkv, lp)
        yr, sf = _rwkv_scan(r, v, kk, lw, kd, b, state_rwkv[:, l])
        srs.append(sf)
        xs, bc, dt = _ssd_prep(p_ssd, lp)
        ys, hf = _ssd_scan(xs, bc, dt, lp['ssd_a_log_p'], state_ssd[:, l])
        sss.append(hf)
        ya_ctx = _ctx_attn(p_att, lp['att_sink'])
        ya_lat = _lat_attn(p_att, lp['att_sink'], rope_cos, rope_sin, kc_dup[:, l], vc_dup[:, l])
        kv = [p_att[:T_CTX, BR + n * LANE:BR + n * LANE + HD].reshape(N_CTX, TB, HD) for n in range(2 * ATT_KVH)]
        ks.append(jnp.stack(kv[:ATT_KVH], axis=2))
        vs.append(jnp.stack(kv[ATT_KVH:], axis=2))
        h = _merge(h, mod, yr, bonus, g, ys, xs, p_ssd, ya_ctx, ya_lat, lp, w_gate, wb16, wo16, l)
        xn, gate, slot, slot_t, cnt = _route(h, mod, lp['norm2_w'], lp['w_route'])
        cnt = cnt[:, 0, :MOE_GROUPS].astype(jnp.int32)
        h = _experts(cnt, xn, gate, slot, slot_t, w1_16, w3_16, w2_16, h, mod, l)
    fw = final_norm_w.reshape(1, D_MODEL)
    y_ctx = _final_norm(h, fw, 0, T_CTX, "final_norm_ctx")
    y_lat = _final_norm(h, fw, T_CTX, N_LAT * LAT_LEN, "final_norm_lat")
    return (y_ctx.reshape(N_CTX, TB, D_MODEL), y_lat.reshape(N_LAT, LAT_LEN, D_MODEL),
            jnp.stack(ks, axis=1), jnp.stack(vs, axis=1),
            jnp.stack(srs, axis=1)[:N_CTX], jnp.stack(sss, axis=1)[:N_CTX])
```
